```python
import jax, jax.numpy as jnp
from jax import lax
import numpy as np

D_MODEL = 2048
BATCH = 2
SEQ = 8192
DEPTH = 2

HEAD_DIM = 128
N_HEADS = 16
N_KV = 4
GQA_REP = N_HEADS // N_KV
CMP_BLOCK = 32
CMP_STRIDE = 16
CMP_HIDDEN = 128
SLC_BLOCK = 64
SLC_TOPK = 16
WINDOW = 512
Q_BLOCK = 128
SLC_Q_BLOCK = 16
N_GROUPS = 4
EXPERTS_PER_GROUP = 8
N_EXPERTS = N_GROUPS * EXPERTS_PER_GROUP
TOP_K_IN_GROUP = 2
D_EXPERT = 512
N_MIXERS = 2
N_LAYERS_A = (DEPTH + 1) // 2
N_LAYERS_B = DEPTH // 2
Q_DIM = N_HEADS * HEAD_DIM
KV_DIM = N_KV * HEAD_DIM
NSA_IN = Q_DIM + 6 * KV_DIM + 3 * N_HEADS
SB_IN = 3 * Q_DIM
EPS = 1e-6
NEG = -1e30

kernel_name = "hybrid_nsa_stickbreak_hmoe"


def rms_norm(x, g):
    xf = x.astype(jnp.float32)
    y = xf * lax.rsqrt(jnp.mean(xf * xf, axis=-1, keepdims=True) + EPS)
    return (y * g.astype(jnp.float32)).astype(x.dtype)


def masked_softmax(s, mask):
    p = jax.nn.softmax(jnp.where(mask, s, NEG), axis=-1)
    return jnp.where(mask, p, 0.0)


def alibi_slopes():
    m = jnp.exp2(-8.0 * jnp.arange(1, N_HEADS + 1, dtype=jnp.float32) / N_HEADS)
    return m.reshape(N_KV, GQA_REP)


def cmp_to_slc_matrix(n_cmp, n_slc):
    cs = jnp.arange(n_cmp) * CMP_STRIDE
    ss = jnp.arange(n_slc) * SLC_BLOCK
    lo = jnp.maximum(cs[:, None], ss[None, :])
    hi = jnp.minimum(cs[:, None] + CMP_BLOCK, ss[None, :] + SLC_BLOCK)
    return jnp.maximum(hi - lo, 0).astype(jnp.float32) / CMP_BLOCK


def nsa_mixer(h, w_in, qk_gain, cmp_pe, cmp_w1, cmp_w2, w_out):
    B, T, _ = h.shape
    n_cmp = (T - CMP_BLOCK) // CMP_STRIDE + 1
    n_slc = T // SLC_BLOCK
    top_k = min(SLC_TOPK, n_slc)
    n_qb = T // Q_BLOCK
    n_sb = T // SLC_Q_BLOCK
    scale = HEAD_DIM ** -0.5
    proj = h @ w_in
    q, kv_c, k_s, v_s, k_w, v_w, gates = jnp.split(
        proj, [Q_DIM, Q_DIM + 2 * KV_DIM, Q_DIM + 3 * KV_DIM, Q_DIM + 4 * KV_DIM,
               Q_DIM + 5 * KV_DIM, Q_DIM + 6 * KV_DIM], axis=-1)
    q = rms_norm(q.reshape(B, T, N_KV, GQA_REP, HEAD_DIM), qk_gain[0]).transpose(0, 2, 3, 1, 4)
    kv_heads = lambda a: a.reshape(B, T, N_KV, HEAD_DIM).transpose(0, 2, 1, 3)
    k_s = rms_norm(kv_heads(k_s), qk_gain[2])
    v_s = kv_heads(v_s)
    k_w = rms_norm(kv_heads(k_w), qk_gain[3])
    v_w = kv_heads(v_w)

    kv_c = kv_c.reshape(B, T, 2, N_KV, HEAD_DIM)
    tok = jnp.arange(n_cmp)[:, None] * CMP_STRIDE + jnp.arange(CMP_BLOCK)[None, :]
    blocks = kv_c[:, tok] + cmp_pe.transpose(1, 0, 2)[None, None, :, :, None, :]
    hid = jax.nn.gelu(jnp.einsum('bnlkgd,kldf->bnkgf', blocks, cmp_w1))
    cmp = jnp.einsum('bnkgf,kfd->kbgnd', hid, cmp_w2)
    k_c = rms_norm(cmp[0], qk_gain[1])
    v_c = cmp[1]
    cmp_end = jnp.arange(n_cmp) * CMP_STRIDE + CMP_BLOCK - 1

    slopes = alibi_slopes()[None, :, :, None, None]
    m_sel = cmp_to_slc_matrix(n_cmp, n_slc)
    blk_ids = jnp.arange(n_slc)
    pad = ((0, 0), (0, 0), (WINDOW, 0), (0, 0))
    k_w_pad = jnp.pad(k_w, pad)
    v_w_pad = jnp.pad(v_w, pad)
    q_blocks = jnp.moveaxis(q.reshape(B, N_KV, GQA_REP, n_qb, Q_BLOCK, HEAD_DIM), 3, 0)

    def cmp_win_block(args):
        qb, c = args
        t = c * Q_BLOCK + jnp.arange(Q_BLOCK)
        s_c = jnp.einsum('bgrqd,bgnd->bgrqn', qb, k_c).astype(jnp.float32) * scale
        dist_c = (t[:, None] - cmp_end[None, :]).astype(jnp.float32)
        p_c = masked_softmax(s_c - slopes * dist_c, cmp_end[None, :] <= t[:, None])
        o_c = jnp.einsum('bgrqn,bgnd->bgrqd', p_c.astype(v_c.dtype), v_c)
        imp = jnp.einsum('bgqn,nj->bgqj', p_c.sum(axis=2), m_sel)
        causal = blk_ids[None, :] * SLC_BLOCK <= t[:, None]
        forced = (blk_ids[None, :] == 0) | (blk_ids[None, :] == (t // SLC_BLOCK)[:, None])
        score = jnp.where(forced, -NEG, jnp.where(causal, imp, NEG))
        vals, idx = lax.top_k(score, top_k)
        valid = vals > 0.5 * NEG
        kw = lax.dynamic_slice_in_dim(k_w_pad, c * Q_BLOCK, Q_BLOCK + WINDOW, axis=2)
        vw = lax.dynamic_slice_in_dim(v_w_pad, c * Q_BLOCK, Q_BLOCK + WINDOW, axis=2)
        spos = c * Q_BLOCK - WINDOW + jnp.arange(Q_BLOCK + WINDOW)
        dist_w = t[:, None] - spos[None, :]
        mask_w = (dist_w >= 0) & (dist_w < WINDOW) & (spos[None, :] >= 0)
        s_w = jnp.einsum('bgrqd,bgkd->bgrqk', qb, kw).astype(jnp.float32) * scale - slopes * dist_w.astype(jnp.float32)
        o_w = jnp.einsum('bgrqk,bgkd->bgrqd', masked_softmax(s_w, mask_w).astype(vw.dtype), vw)
        return o_c, o_w, idx, valid

    o_c, o_w, idx, valid = lax.map(cmp_win_block, (q_blocks, jnp.arange(n_qb)))
    unblock = lambda o: jnp.moveaxis(o, 0, 3).reshape(B, N_KV, GQA_REP, T, HEAD_DIM)
    o_c = unblock(o_c)
    o_w = unblock(o_w)
    idx = jnp.moveaxis(idx, 0, 2).reshape(B, N_KV, T, top_k)
    valid = jnp.moveaxis(valid, 0, 2).reshape(B, N_KV, T, top_k)

    k_blk = k_s.reshape(B, N_KV, n_slc, SLC_BLOCK, HEAD_DIM)
    v_blk = v_s.reshape(B, N_KV, n_slc, SLC_BLOCK, HEAD_DIM)
    bi = jnp.arange(B)[:, None, None, None]
    gi = jnp.arange(N_KV)[None, :, None, None]
    q_sb = jnp.moveaxis(q.reshape(B, N_KV, GQA_REP, n_sb, SLC_Q_BLOCK, HEAD_DIM), 3, 0)
    idx_sb = jnp.moveaxis(idx.reshape(B, N_KV, n_sb, SLC_Q_BLOCK, top_k), 2, 0)
    val_sb = jnp.moveaxis(valid.reshape(B, N_KV, n_sb, SLC_Q_BLOCK, top_k), 2, 0)
    n_keys = top_k * SLC_BLOCK

    def slc_block(args):
        qb, ib, vb, c = args
        t = c * SLC_Q_BLOCK + jnp.arange(SLC_Q_BLOCK)
        ks = k_blk[bi, gi, ib].reshape(B, N_KV, SLC_Q_BLOCK, n_keys, HEAD_DIM)
        vs = v_blk[bi, gi, ib].reshape(B, N_KV, SLC_Q_BLOCK, n_keys, HEAD_DIM)
        pos = (ib[..., None] * SLC_BLOCK + jnp.arange(SLC_BLOCK)).reshape(B, N_KV, SLC_Q_BLOCK, n_keys)
        dist = t[None, None, :, None] - pos
        mask = (jnp.repeat(vb, SLC_BLOCK, axis=-1) & (dist >= 0))[:, :, None]
        s = jnp.einsum('bgrqd,bgqkd->bgrqk', qb, ks).astype(jnp.float32) * scale - slopes * dist[:, :, None].astype(jnp.float32)
        p = masked_softmax(s, mask)
        return jnp.einsum('bgrqk,bgqkd->bgrqd', p.astype(vs.dtype), vs)

    o_s = unblock(lax.map(slc_block, (q_sb, idx_sb, val_sb, jnp.arange(n_sb))))

    g = jax.nn.sigmoid(gates.reshape(B, T, 3, N_KV, GQA_REP).astype(jnp.float32)).astype(h.dtype)
    g = g.transpose(2, 0, 3, 4, 1)[..., None]
    o = g[0] * o_c + g[1] * o_s + g[2] * o_w
    return o.transpose(0, 3, 1, 2, 4).reshape(B, T, Q_DIM) @ w_out


def stick_breaking_mixer(h, w_in, w_out):
    B, T, _ = h.shape
    n_qb = T // Q_BLOCK
    scale = HEAD_DIM ** -0.5
    q, k, v = jnp.split(h @ w_in, 3, axis=-1)
    to_heads = lambda a: a.reshape(B, T, N_HEADS, HEAD_DIM).transpose(0, 2, 1, 3)
    q, k, v = to_heads(q), to_heads(k), to_heads(v)
    q_blocks = jnp.moveaxis(q.reshape(B, N_HEADS, n_qb, Q_BLOCK, HEAD_DIM), 2, 0)
    spos = jnp.arange(T)

    def sb_block(args):
        qb, c = args
        t = c * Q_BLOCK + jnp.arange(Q_BLOCK)
        z = jnp.einsum('bhqd,bhkd->bhqk', qb, k).astype(jnp.float32) * scale
        mask = spos[None, :] < t[:, None]
        log_fail = jnp.where(mask, jax.nn.log_sigmoid(-z), 0.0)
        after = lax.cumsum(log_fail, axis=3, reverse=True) - log_fail
        a = jnp.where(mask, jnp.exp(jax.nn.log_sigmoid(z) + after), 0.0)
        return jnp.einsum('bhqk,bhkd->bhqd', a.astype(v.dtype), v)

    o = lax.map(sb_block, (q_blocks, jnp.arange(n_qb)))
    o = jnp.moveaxis(o, 0, 2).reshape(B, N_HEADS, T, HEAD_DIM).transpose(0, 2, 1, 3).reshape(B, T, Q_DIM)
    return o @ w_out


def hier_moe(h, w_group, b_group, w_router, b_router, w_gate, w_up, w_down):
    B, T, D = h.shape
    hf = h.reshape(B * T, D)
    p_group = jax.nn.softmax((hf @ w_group + b_group).astype(jnp.float32), axis=-1)
    pg, gsel = lax.top_k(p_group, 1)
    logits = (hf @ w_router + b_router).astype(jnp.float32).reshape(-1, N_GROUPS, EXPERTS_PER_GROUP)
    in_group = jnp.take_along_axis(logits, gsel[:, :, None], axis=1)[:, 0]
    pe, esel = lax.top_k(jax.nn.softmax(in_group, axis=-1), TOP_K_IN_GROUP)
    weight = pg * pe / jnp.sum(pe, axis=-1, keepdims=True)
    expert_id = gsel * EXPERTS_PER_GROUP + esel
    combine = jnp.sum(jax.nn.one_hot(expert_id, N_EXPERTS, dtype=jnp.float32) * weight[..., None], axis=1)
    combine = combine.astype(hf.dtype)
    out = jnp.zeros_like(hf)
    for e in range(N_EXPERTS):
        y = (jax.nn.silu(hf @ w_gate[e]) * (hf @ w_up[e])) @ w_down[e]
        out = out + combine[:, e:e + 1] * y
    return out.reshape(B, T, D)


def setup_inputs(seed: int = 0) -> dict:
    key = jax.random.key(seed)
    ks = jax.random.split(key, 20)
    nrm = lambda k, shape, s: jax.random.normal(k, shape, jnp.float32) * s
    return {
        "x": nrm(ks[0], (BATCH, SEQ, D_MODEL), 1.0),
        "norm_mix": 1.0 + nrm(ks[1], (DEPTH, D_MODEL), 0.02),
        "norm_ffn": 1.0 + nrm(ks[2], (DEPTH, D_MODEL), 0.02),
        "nsa_w_in": nrm(ks[3], (N_LAYERS_A, D_MODEL, NSA_IN), D_MODEL ** -0.5),
        "nsa_qk_gain": 1.0 + nrm(ks[4], (N_LAYERS_A, 4, HEAD_DIM), 0.02),
        "nsa_cmp_pe": nrm(ks[5], (N_LAYERS_A, 2, CMP_BLOCK, HEAD_DIM), 0.1),
        "nsa_cmp_w1": nrm(ks[6], (N_LAYERS_A, 2, CMP_BLOCK, HEAD_DIM, CMP_HIDDEN), (CMP_BLOCK * HEAD_DIM) ** -0.5),
        "nsa_cmp_w2": nrm(ks[7], (N_LAYERS_A, 2, CMP_HIDDEN, HEAD_DIM), CMP_HIDDEN ** -0.5),
        "nsa_w_out": nrm(ks[8], (N_LAYERS_A, Q_DIM, D_MODEL), Q_DIM ** -0.5),
        "sb_w_in": nrm(ks[9], (N_LAYERS_B, D_MODEL, SB_IN), D_MODEL ** -0.5),
        "sb_w_out": nrm(ks[10], (N_LAYERS_B, Q_DIM, D_MODEL), Q_DIM ** -0.5),
        "moe_w_group": nrm(ks[11], (DEPTH, D_MODEL, N_GROUPS), D_MODEL ** -0.5),
        "moe_b_group": nrm(ks[12], (DEPTH, N_GROUPS), 0.01),
        "moe_w_router": nrm(ks[13], (DEPTH, D_MODEL, N_EXPERTS), D_MODEL ** -0.5),
        "moe_b_router": nrm(ks[14], (DEPTH, N_EXPERTS), 0.01),
        "moe_w_gate": nrm(ks[15], (DEPTH, N_EXPERTS, D_MODEL, D_EXPERT), D_MODEL ** -0.5),
        "moe_w_up": nrm(ks[16], (DEPTH, N_EXPERTS, D_MODEL, D_EXPERT), D_MODEL ** -0.5),
        "moe_w_down": nrm(ks[17], (DEPTH, N_EXPERTS, D_EXPERT, D_MODEL), D_EXPERT ** -0.5),
    }


def reference(x, norm_mix, norm_ffn, nsa_w_in, nsa_qk_gain, nsa_cmp_pe, nsa_cmp_w1, nsa_cmp_w2, nsa_w_out,
              sb_w_in, sb_w_out, moe_w_group, moe_b_group, moe_w_router, moe_b_router,
              moe_w_gate, moe_w_up, moe_w_down):
    for i in range(DEPTH):
        h = rms_norm(x, norm_mix[i])
        j = i // N_MIXERS
        if i % N_MIXERS == 0:
            mix = nsa_mixer(h, nsa_w_in[j], nsa_qk_gain[j], nsa_cmp_pe[j], nsa_cmp_w1[j], nsa_cmp_w2[j], nsa_w_out[j])
        else:
            mix = stick_breaking_mixer(h, sb_w_in[j], sb_w_out[j])
        x = x + mix
        x = x + hier_moe(rms_norm(x, norm_ffn[i]), moe_w_group[i], moe_b_group[i], moe_w_router[i], moe_b_router[i],
                         moe_w_gate[i], moe_w_up[i], moe_w_down[i])
    return x
```

```python
import functools

import jax
import jax.numpy as jnp
import numpy as np
from jax import lax
from jax.experimental import pallas as pl
from jax.experimental.pallas import tpu as pltpu

F32 = jnp.float32
BF16 = jnp.bfloat16

HEAD_DIM = 128
N_HEADS = 16
N_KV = 4
GQA_REP = N_HEADS // N_KV
CMP_BLOCK = 32
CMP_STRIDE = 16
SLC_BLOCK = 64
SLC_TOPK = 16
WINDOW = 512
Q_BLOCK = 128
N_GROUPS = 4
EXPERTS_PER_GROUP = 8
N_EXPERTS = N_GROUPS * EXPERTS_PER_GROUP
EPS = 1e-6
NEG = -1e30
LANES = 128
VMEM_LIMIT = 56 * 1024 * 1024

SLC_LANES = 128
SB_TILE = 256
SLC_KEY_TILE = 512
MOE_TILE = 256


def _dot(a, b):
    return jnp.dot(a, b, preferred_element_type=F32)


def _dot_nt(a, b):
    return lax.dot_general(a, b, (((1,), (1,)), ((), ())), preferred_element_type=F32)


def _params(sem):
    return pltpu.CompilerParams(dimension_semantics=sem, vmem_limit_bytes=VMEM_LIMIT)


def _norm_matmul_kernel(x_ref, g_ref, w_ref, cg_ref, nf_ref, o_ref, hn_ref):
    @pl.when(pl.program_id(1) == 0)
    def _():
        x = x_ref[...]
        ms = jnp.mean(x * x, axis=-1, keepdims=True)
        hn_ref[...] = (x * lax.rsqrt(ms + EPS) * g_ref[...]).astype(BF16)

    y = _dot(hn_ref[...], w_ref[...])
    for hh in range(y.shape[1] // LANES):
        sl = slice(hh * LANES, (hh + 1) * LANES)
        yh = y[:, sl]
        ms = jnp.mean(yh * yh, axis=-1, keepdims=True)
        nf = nf_ref[0, :, sl]
        fac = nf * lax.rsqrt(ms + EPS) + (1.0 - nf)
        o_ref[:, sl] = (yh * fac * cg_ref[0, :, sl]).astype(o_ref.dtype)


def _norm_matmul(x, g, w, cgain, nflag, *, tm, tn, out_dtype):
    n, d = x.shape
    m = w.shape[1]
    nj = m // tn
    return pl.pallas_call(
        _norm_matmul_kernel,
        grid=(n // tm, nj),
        in_specs=[
            pl.BlockSpec((tm, d), lambda i, j: (i, 0)),
            pl.BlockSpec((1, d), lambda i, j: (0, 0)),
            pl.BlockSpec((d, tn), lambda i, j: (0, j)),
            pl.BlockSpec((1, 1, tn), lambda i, j: (j, 0, 0)),
            pl.BlockSpec((1, 1, tn), lambda i, j: (j, 0, 0)),
        ],
        out_specs=pl.BlockSpec((tm, tn), lambda i, j: (i, j)),
        out_shape=jax.ShapeDtypeStruct((n, m), out_dtype),
        scratch_shapes=[pltpu.VMEM((tm, d), BF16)],
        compiler_params=_params(("parallel", "arbitrary")),
        name="norm_matmul",
    )(x, g.reshape(1, d), w, cgain.reshape(nj, 1, tn), nflag.reshape(nj, 1, tn))


def _proj_residual_kernel(*refs, n_in):
    a_refs = refs[:n_in]
    w_ref, r_ref, o_ref, a_scr = refs[n_in:]

    @pl.when(pl.program_id(1) == 0)
    def _():
        acc = a_refs[0][...].astype(F32)
        for a in a_refs[1:]:
            acc = acc + a[...].astype(F32)
        a_scr[...] = acc.astype(BF16)

    o_ref[...] = r_ref[...] + _dot(a_scr[...], w_ref[...])


def _proj_residual(branches, w, res, *, tm, tn):
    n, k = branches[0].shape
    m = w.shape[1]
    n_in = len(branches)
    return pl.pallas_call(
        functools.partial(_proj_residual_kernel, n_in=n_in),
        grid=(n // tm, m // tn),
        in_specs=[pl.BlockSpec((tm, k), lambda i, j: (i, 0)) for _ in range(n_in)]
        + [pl.BlockSpec((k, tn), lambda i, j: (0, j)), pl.BlockSpec((tm, tn), lambda i, j: (i, j))],
        out_specs=pl.BlockSpec((tm, tn), lambda i, j: (i, j)),
        out_shape=jax.ShapeDtypeStruct((n, m), F32),
        scratch_shapes=[pltpu.VMEM((tm, k), BF16)],
        compiler_params=_params(("parallel", "arbitrary")),
        name="proj_residual",
    )(*branches, w, res)


def _cmp_kernel(ch_ref, w1_ref, pe_ref, w2_ref, gain_ref, o_ref):
    nc = ch_ref.shape[3]
    w1 = w1_ref[0]
    p = _dot(ch_ref[0, 0, 0], w1)
    pb = _dot(pe_ref[0], w1)
    bias = pb[0:1, :HEAD_DIM] + pb[1:2, HEAD_DIM:]
    second = pltpu.roll(p[:, HEAD_DIM:], shift=nc - 1, axis=0)
    hid = jax.nn.gelu(p[:, :HEAD_DIM] + second + bias)
    c = _dot(hid.astype(BF16), w2_ref[0])
    nf = (pl.program_id(1) == 0).astype(F32)
    ms = jnp.mean(c * c, axis=-1, keepdims=True)
    fac = nf * lax.rsqrt(ms + EPS) + (1.0 - nf)
    o_ref[0, 0, 0] = (c * fac * gain_ref[0]).astype(o_ref.dtype)


def _compress(chunks, w1cat, pe_rows, w2, gains):
    b, _, g, nc, ck = chunks.shape
    return pl.pallas_call(
        _cmp_kernel,
        grid=(b, 2, g),
        in_specs=[
            pl.BlockSpec((1, 1, 1, nc, ck), lambda bi, k, gi: (bi, k, gi, 0, 0)),
            pl.BlockSpec((1, ck, 2 * HEAD_DIM), lambda bi, k, gi: (k, 0, 0)),
            pl.BlockSpec((1, 8, ck), lambda bi, k, gi: (k, 0, 0)),
            pl.BlockSpec((1, HEAD_DIM, HEAD_DIM), lambda bi, k, gi: (k, 0, 0)),
            pl.BlockSpec((1, 1, HEAD_DIM), lambda bi, k, gi: (k, 0, 0)),
        ],
        out_specs=pl.BlockSpec((1, 1, 1, nc, HEAD_DIM), lambda bi, k, gi: (bi, k, gi, 0, 0)),
        out_shape=jax.ShapeDtypeStruct((b, 2, g, nc, HEAD_DIM), BF16),
        compiler_params=_params(("parallel", "parallel", "parallel")),
        name="nsa_compress",
    )(chunks, w1cat, pe_rows, w2, gains)


def _cmp_attn_kernel(slopes_ref, q_ref, kc_ref, vc_ref, gate_ref, msel_ref, oc_ref, sb_ref, *, top_k):
    g = pl.program_id(1)
    t0 = pl.program_id(2) * Q_BLOCK
    kc = kc_ref[0, 0, 0]
    vc = vc_ref[0, 0, 0]
    nc = kc.shape[0]
    t_col = t0 + lax.broadcasted_iota(jnp.int32, (Q_BLOCK, 1), 0)
    cend = lax.broadcasted_iota(jnp.int32, (1, nc), 1) * CMP_STRIDE + (CMP_BLOCK - 1)
    dist_i = t_col - cend
    mask = dist_i >= 0
    dist = dist_i.astype(F32)
    gates = jax.nn.sigmoid(gate_ref[0])
    psum = jnp.zeros((Q_BLOCK, nc), F32)
    for r in range(GQA_REP):
        slope = slopes_ref[g * GQA_REP + r]
        s = _dot_nt(q_ref[:, r * HEAD_DIM:(r + 1) * HEAD_DIM], kc)
        s = jnp.where(mask, s - slope * dist, NEG)
        e = jnp.exp(s - jnp.max(s, axis=-1, keepdims=True))
        p = jnp.where(mask, e / jnp.sum(e, axis=-1, keepdims=True), 0.0)
        psum = psum + p
        o = _dot(p.astype(BF16), vc)
        oc_ref[:, r * HEAD_DIM:(r + 1) * HEAD_DIM] = (o * gates[:, r:r + 1]).astype(oc_ref.dtype)

    msel = msel_ref[...]
    p_hi = psum.astype(BF16)
    rem = psum - p_hi.astype(F32)
    p_mid = rem.astype(BF16)
    p_lo = (rem - p_mid.astype(F32)).astype(BF16)
    imp = _dot(p_hi, msel) + _dot(p_mid, msel) + _dot(p_lo, msel)

    imp_t = imp.T
    j_col = lax.broadcasted_iota(jnp.int32, (SLC_LANES, 1), 0)
    t_row = t0 + lax.broadcasted_iota(jnp.int32, (1, Q_BLOCK), 1)
    forced = (j_col == 0) | (j_col == (t_row >> 6))
    causal = (j_col << 6) <= t_row
    score = jnp.where(forced, -NEG, jnp.where(causal, imp_t, NEG))
    keep = jnp.zeros((SLC_LANES, Q_BLOCK), F32)
    for _ in range(top_k):
        m = jnp.max(score, axis=0, keepdims=True)
        first = jnp.min(jnp.where(score == m, j_col, SLC_LANES), axis=0, keepdims=True)
        hit = j_col == first
        keep = jnp.where(hit, (m > 0.5 * NEG).astype(F32), keep)
        score = jnp.where(hit, -3e38, score)
    bias_t = jnp.where(keep > 0.5, 0.0, NEG)
    sb_ref[0] = bias_t.T.astype(sb_ref.dtype)


def _cmp_attention(slopes, proj, cmp_kv, gates_r, msel, *, b, t, top_k):
    n = b * t
    nqb = t // Q_BLOCK
    nc = cmp_kv.shape[3]
    rowblk = lambda bi, gi, c: bi * nqb + c
    return pl.pallas_call(
        functools.partial(_cmp_attn_kernel, top_k=top_k),
        grid=(b, N_KV, nqb),
        in_specs=[
            pl.BlockSpec(memory_space=pltpu.SMEM),
            pl.BlockSpec((Q_BLOCK, GQA_REP * HEAD_DIM), lambda bi, gi, c: (rowblk(bi, gi, c), gi)),
            pl.BlockSpec((1, 1, 1, nc, HEAD_DIM), lambda bi, gi, c: (bi, 0, gi, 0, 0)),
            pl.BlockSpec((1, 1, 1, nc, HEAD_DIM), lambda bi, gi, c: (bi, 1, gi, 0, 0)),
            pl.BlockSpec((1, Q_BLOCK, LANES), lambda bi, gi, c: (gi, rowblk(bi, gi, c), 0)),
            pl.BlockSpec((nc, SLC_LANES), lambda bi, gi, c: (0, 0)),
        ],
        out_specs=[
            pl.BlockSpec((Q_BLOCK, GQA_REP * HEAD_DIM), lambda bi, gi, c: (rowblk(bi, gi, c), gi)),
            pl.BlockSpec((1, Q_BLOCK, SLC_LANES), lambda bi, gi, c: (gi, rowblk(bi, gi, c), 0)),
        ],
        out_shape=[
            jax.ShapeDtypeStruct((n, N_HEADS * HEAD_DIM), BF16),
            jax.ShapeDtypeStruct((N_KV, n, SLC_LANES), BF16),
        ],
        compiler_params=_params(("parallel", "parallel", "parallel")),
        name="nsa_cmp_attn",
    )(slopes, proj, cmp_kv, cmp_kv, gates_r, msel)


def _win_attn_kernel(slopes_ref, q_ref, k_ref, v_ref, gate_ref, o_ref):
    g = pl.program_id(1)
    t0 = pl.program_id(2) * Q_BLOCK
    span = Q_BLOCK + WINDOW
    start = pl.multiple_of(jnp.maximum(t0 - WINDOW, 0), Q_BLOCK)
    k = k_ref[pl.ds(start, span), :]
    v = v_ref[pl.ds(start, span), :]
    t_col = t0 + lax.broadcasted_iota(jnp.int32, (Q_BLOCK, 1), 0)
    pos = start + lax.broadcasted_iota(jnp.int32, (1, span), 1)
    dist_i = t_col - pos
    mask = (dist_i >= 0) & (dist_i < WINDOW)
    dist = dist_i.astype(F32)
    gates = jax.nn.sigmoid(gate_ref[0])
    for r in range(GQA_REP):
        slope = slopes_ref[g * GQA_REP + r]
        s = _dot_nt(q_ref[:, r * HEAD_DIM:(r + 1) * HEAD_DIM], k)
        s = jnp.where(mask, s - slope * dist, NEG)
        e = jnp.exp(s - jnp.max(s, axis=-1, keepdims=True))
        p = jnp.where(mask, e / jnp.sum(e, axis=-1, keepdims=True), 0.0)
        o = _dot(p.astype(BF16), v)
        gate = gates[:, 2 * GQA_REP + r:2 * GQA_REP + r + 1]
        o_ref[:, r * HEAD_DIM:(r + 1) * HEAD_DIM] = (o * gate).astype(o_ref.dtype)


def _win_attention(slopes, proj, gates_r, *, b, t, k_col, v_col):
    n = b * t
    nqb = t // Q_BLOCK
    rowblk = lambda bi, gi, c: bi * nqb + c
    return pl.pallas_call(
        _win_attn_kernel,
        grid=(b, N_KV, nqb),
        in_specs=[
            pl.BlockSpec(memory_space=pltpu.SMEM),
            pl.BlockSpec((Q_BLOCK, GQA_REP * HEAD_DIM), lambda bi, gi, c: (rowblk(bi, gi, c), gi)),
            pl.BlockSpec((t, HEAD_DIM), lambda bi, gi, c: (bi, k_col + gi)),
            pl.BlockSpec((t, HEAD_DIM), lambda bi, gi, c: (bi, v_col + gi)),
            pl.BlockSpec((1, Q_BLOCK, LANES), lambda bi, gi, c: (gi, rowblk(bi, gi, c), 0)),
        ],
        out_specs=pl.BlockSpec((Q_BLOCK, GQA_REP * HEAD_DIM), lambda bi, gi, c: (rowblk(bi, gi, c), gi)),
        out_shape=jax.ShapeDtypeStruct((n, N_HEADS * HEAD_DIM), BF16),
        compiler_params=_params(("parallel", "parallel", "arbitrary")),
        name="nsa_win_attn",
    )(slopes, proj, proj, proj, gates_r)


def _slc_attn_kernel(slopes_ref, q_ref, k_ref, v_ref, sb_ref, gate_ref, o_ref, qa_ref, m_ref, l_ref, acc_ref):
    g = pl.program_id(1)
    t0 = pl.program_id(2) * Q_BLOCK
    rows = GQA_REP * Q_BLOCK
    tk = SLC_KEY_TILE
    sb = sb_ref[0]
    for r in range(GQA_REP):
        qa_ref[r * Q_BLOCK:(r + 1) * Q_BLOCK, :HEAD_DIM] = q_ref[:, r * HEAD_DIM:(r + 1) * HEAD_DIM]
        qa_ref[r * Q_BLOCK:(r + 1) * Q_BLOCK, HEAD_DIM:] = sb
    m_ref[...] = jnp.full((rows, 1), NEG, F32)
    l_ref[...] = jnp.zeros((rows, 1), F32)
    acc_ref[...] = jnp.zeros((rows, HEAD_DIM), F32)

    row = lax.broadcasted_iota(jnp.int32, (rows, 1), 0)
    t_col = t0 + (row & (Q_BLOCK - 1))
    head = row >> 7
    slope_col = jnp.zeros((rows, 1), F32)
    for r in range(GQA_REP):
        slope_col = jnp.where(head == r, slopes_ref[g * GQA_REP + r], slope_col)
    kk = lax.broadcasted_iota(jnp.int32, (tk, 1), 0)
    j_row = lax.broadcasted_iota(jnp.int32, (1, SLC_LANES), 1)
    k_iota = lax.broadcasted_iota(jnp.int32, (1, tk), 1)

    def tile(kt, carry):
        k0 = pl.multiple_of(kt * tk, tk)
        onehot = (((k0 + kk) >> 6) == j_row).astype(BF16)
        ka = jnp.concatenate([k_ref[pl.ds(k0, tk), :], onehot], axis=1)
        pos = k0 + k_iota
        s = _dot_nt(qa_ref[...], ka) + slope_col * (pos - t0).astype(F32)
        s = jnp.where(pos <= t_col, s, NEG)
        m_old = m_ref[...]
        m_new = jnp.maximum(m_old, jnp.max(s, axis=-1, keepdims=True))
        alpha = jnp.exp(m_old - m_new)
        p = jnp.exp(s - m_new)
        l_ref[...] = alpha * l_ref[...] + jnp.sum(p, axis=-1, keepdims=True)
        acc_ref[...] = alpha * acc_ref[...] + _dot(p.astype(BF16), v_ref[pl.ds(k0, tk), :])
        m_ref[...] = m_new
        return carry

    lax.fori_loop(0, lax.div(t0, tk) + 1, tile, 0)
    gates = jax.nn.sigmoid(gate_ref[0])
    o = acc_ref[...] / l_ref[...]
    for r in range(GQA_REP):
        gate = gates[:, GQA_REP + r:GQA_REP + r + 1]
        o_ref[:, r * HEAD_DIM:(r + 1) * HEAD_DIM] = (o[r * Q_BLOCK:(r + 1) * Q_BLOCK] * gate).astype(o_ref.dtype)


def _slc_attention(slopes, proj, selbias, gates_r, *, b, t, k_col, v_col):
    n = b * t
    nqb = t // Q_BLOCK
    rows = GQA_REP * Q_BLOCK
    rowblk = lambda bi, gi, c: bi * nqb + c
    return pl.pallas_call(
        _slc_attn_kernel,
        grid=(b, N_KV, nqb),
        in_specs=[
            pl.BlockSpec(memory_space=pltpu.SMEM),
            pl.BlockSpec((Q_BLOCK, GQA_REP * HEAD_DIM), lambda bi, gi, c: (rowblk(bi, gi, c), gi)),
            pl.BlockSpec((t, HEAD_DIM), lambda bi, gi, c: (bi, k_col + gi)),
            pl.BlockSpec((t, HEAD_DIM), lambda bi, gi, c: (bi, v_col + gi)),
            pl.BlockSpec((1, Q_BLOCK, SLC_LANES), lambda bi, gi, c: (gi, rowblk(bi, gi, c), 0)),
            pl.BlockSpec((1, Q_BLOCK, LANES), lambda bi, gi, c: (gi, rowblk(bi, gi, c), 0)),
        ],
        out_specs=pl.BlockSpec((Q_BLOCK, GQA_REP * HEAD_DIM), lambda bi, gi, c: (rowblk(bi, gi, c), gi)),
        out_shape=jax.ShapeDtypeStruct((n, N_HEADS * HEAD_DIM), BF16),
        scratch_shapes=[
            pltpu.VMEM((rows, 2 * HEAD_DIM), BF16),
            pltpu.VMEM((rows, 1), F32),
            pltpu.VMEM((rows, 1), F32),
            pltpu.VMEM((rows, HEAD_DIM), F32),
        ],
        compiler_params=_params(("parallel", "parallel", "arbitrary")),
        name="nsa_slc_attn",
    )(slopes, proj, proj, proj, selbias, gates_r)


def _sb_attn_kernel(q_ref, k_ref, v_ref, o_ref, acc_ref, carry_ref):
    ts = SB_TILE
    c = pl.program_id(2)
    q = q_ref[...]
    acc_ref[...] = jnp.zeros((ts, HEAD_DIM), F32)
    carry_ref[...] = jnp.zeros((ts, 1), F32)
    r_idx = lax.broadcasted_iota(jnp.int32, (ts, 1), 0)
    c_idx = lax.broadcasted_iota(jnp.int32, (1, ts), 1)
    tri = (r_idx > c_idx).astype(BF16)
    diag = c_idx < r_idx

    def tile(jt, valid):
        k0 = pl.multiple_of(jt * ts, ts)
        z = _dot_nt(q, k_ref[pl.ds(k0, ts), :])
        lf = -(jnp.maximum(z, 0.0) + jnp.log(1.0 + jnp.exp(-jnp.abs(z))))
        if valid is not None:
            lf = jnp.where(valid, lf, 0.0)
        hi = lf.astype(BF16)
        lo = (lf - hi.astype(F32)).astype(BF16)
        after = _dot(hi, tri) + _dot(lo, tri) + carry_ref[...]
        a = jnp.exp(z + lf + after)
        if valid is not None:
            a = jnp.where(valid, a, 0.0)
        acc_ref[...] += _dot(a.astype(BF16), v_ref[pl.ds(k0, ts), :])
        carry_ref[...] += jnp.sum(lf, axis=-1, keepdims=True)

    tile(c, diag)

    def body(i, carry):
        tile(c - 1 - i, None)
        return carry

    lax.fori_loop(0, c, body, 0)
    o_ref[...] = acc_ref[...].astype(o_ref.dtype)


def _sb_attention(proj, *, b, t):
    n = b * t
    ts = SB_TILE
    nq = t // ts
    return pl.pallas_call(
        _sb_attn_kernel,
        grid=(b, N_HEADS, nq),
        in_specs=[
            pl.BlockSpec((ts, HEAD_DIM), lambda bi, h, c: (bi * nq + c, h)),
            pl.BlockSpec((t, HEAD_DIM), lambda bi, h, c: (bi, N_HEADS + h)),
            pl.BlockSpec((t, HEAD_DIM), lambda bi, h, c: (bi, 2 * N_HEADS + h)),
        ],
        out_specs=pl.BlockSpec((ts, HEAD_DIM), lambda bi, h, c: (bi * nq + c, h)),
        out_shape=jax.ShapeDtypeStruct((n, N_HEADS * HEAD_DIM), BF16),
        scratch_shapes=[pltpu.VMEM((ts, HEAD_DIM), F32), pltpu.VMEM((ts, 1), F32)],
        compiler_params=_params(("parallel", "parallel", "arbitrary")),
        name="sb_attn",
    )(proj, proj, proj)


def _first_max(vals, lane):
    m = jnp.max(vals, axis=-1, keepdims=True)
    idx = jnp.min(jnp.where(vals == m, lane, LANES), axis=-1, keepdims=True)
    return m, idx


def _router_kernel(x_ref, g_ref, w_ref, b_ref, o_ref):
    x = x_ref[...]
    ms = jnp.mean(x * x, axis=-1, keepdims=True)
    h = x * lax.rsqrt(ms + EPS) * g_ref[...]
    logits = jnp.dot(h, w_ref[...], preferred_element_type=F32, precision=lax.Precision.HIGHEST) + b_ref[...]
    lane = lax.broadcasted_iota(jnp.int32, (1, LANES), 1)
    gl = jnp.where(lane < N_GROUPS, logits, NEG)
    ge = jnp.exp(gl - jnp.max(gl, axis=-1, keepdims=True))
    pg_all = ge / jnp.sum(ge, axis=-1, keepdims=True)
    pg, gsel = _first_max(jnp.where(lane < N_GROUPS, pg_all, -1.0), lane)
    lo = N_GROUPS + gsel * EXPERTS_PER_GROUP
    in_group = (lane >= lo) & (lane < lo + EXPERTS_PER_GROUP)
    el = jnp.where(in_group, logits, NEG)
    ee = jnp.exp(el - jnp.max(el, axis=-1, keepdims=True))
    pe_all = jnp.where(in_group, ee / jnp.sum(ee, axis=-1, keepdims=True), -1.0)
    p1, i1 = _first_max(pe_all, lane)
    p2, i2 = _first_max(jnp.where(lane == i1, -1.0, pe_all), lane)
    denom = p1 + p2
    w1 = pg * p1 / denom
    w2 = pg * p2 / denom
    e1 = (i1 - N_GROUPS).astype(F32)
    e2 = (i2 - N_GROUPS).astype(F32)
    o_ref[...] = jnp.where(lane == 0, e1, jnp.where(lane == 1, e2, jnp.where(lane == 2, w1, jnp.where(lane == 3, w2, 0.0))))


def _router(x, g, w, bias, *, tm):
    n, d = x.shape
    return pl.pallas_call(
        _router_kernel,
        grid=(n // tm,),
        in_specs=[
            pl.BlockSpec((tm, d), lambda i: (i, 0)),
            pl.BlockSpec((1, d), lambda i: (0, 0)),
            pl.BlockSpec((d, LANES), lambda i: (0, 0)),
            pl.BlockSpec((1, LANES), lambda i: (0, 0)),
        ],
        out_specs=pl.BlockSpec((tm, LANES), lambda i: (i, 0)),
        out_shape=jax.ShapeDtypeStruct((n, LANES), F32),
        compiler_params=_params(("parallel",)),
        name="moe_router",
    )(x, g.reshape(1, d), w, bias)


def _row_gather_start(src_hbm, dst_ref, sem, idx_ref, base, count):
    def body(r, carry):
        pltpu.make_async_copy(src_hbm.at[pl.ds(idx_ref[base + r], 1), :], dst_ref.at[pl.ds(r, 1), :], sem).start()
        return carry

    lax.fori_loop(0, count, body, 0)


def _row_gather_wait(src_hbm, dst_ref, sem):
    pltpu.make_async_copy(src_hbm.at[pl.ds(0, dst_ref.shape[0]), :], dst_ref, sem).wait()


def _expert_kernel(src_ref, te_ref, nu_ref, x_hbm, g_ref, wg_ref, wu_ref, wd_ref, o_ref, xbuf, sem):
    tm = MOE_TILE
    i = pl.program_id(0)
    n_used = nu_ref[0]

    @pl.when((i == 0) & (n_used > 0))
    def _():
        _row_gather_start(x_hbm, xbuf.at[0], sem.at[0], src_ref, 0, tm)

    @pl.when(i + 1 < n_used)
    def _():
        nxt = (i + 1) % 2
        _row_gather_start(x_hbm, xbuf.at[nxt], sem.at[nxt], src_ref, (i + 1) * tm, tm)

    @pl.when(i < n_used)
    def _():
        slot = i % 2
        _row_gather_wait(x_hbm, xbuf.at[slot], sem.at[slot])
        x = xbuf[slot]
        ms = jnp.mean(x * x, axis=-1, keepdims=True)
        h = (x * lax.rsqrt(ms + EPS) * g_ref[...]).astype(BF16)
        gate = _dot(h, wg_ref[0])
        up = _dot(h, wu_ref[0])
        act = (gate * jax.nn.sigmoid(gate) * up).astype(BF16)
        o_ref[...] = _dot(act, wd_ref[0])

    @pl.when(i >= n_used)
    def _():
        o_ref[...] = jnp.zeros(o_ref.shape, o_ref.dtype)


def _experts(src_tok, tile_expert, n_used, x, g, wg, wu, wd):
    n, d = x.shape
    tm = MOE_TILE
    p_rows = src_tok.shape[0]
    de = wg.shape[2]
    grid_spec = pltpu.PrefetchScalarGridSpec(
        num_scalar_prefetch=3,
        grid=(p_rows // tm,),
        in_specs=[
            pl.BlockSpec(memory_space=pl.ANY),
            pl.BlockSpec((1, d), lambda i, s, te, nu: (0, 0)),
            pl.BlockSpec((1, d, de), lambda i, s, te, nu: (te[i], 0, 0)),
            pl.BlockSpec((1, d, de), lambda i, s, te, nu: (te[i], 0, 0)),
            pl.BlockSpec((1, de, d), lambda i, s, te, nu: (te[i], 0, 0)),
        ],
        out_specs=pl.BlockSpec((tm, d), lambda i, s, te, nu: (i, 0)),
        scratch_shapes=[pltpu.VMEM((2, tm, d), F32), pltpu.SemaphoreType.DMA((2,))],
    )
    return pl.pallas_call(
        _expert_kernel,
        grid_spec=grid_spec,
        out_shape=jax.ShapeDtypeStruct((p_rows, d), F32),
        compiler_params=_params(("arbitrary",)),
        name="moe_experts",
    )(src_tok, tile_expert, n_used, x, g.reshape(1, d), wg, wu, wd)


def _combine_kernel(pos_ref, x_ref, rt_ref, y_hbm, o_ref, ybuf, sem):
    tm = x_ref.shape[0]
    i = pl.program_id(0)
    nt = pl.num_programs(0)

    def start(tile, slot):
        for s in range(2):
            def body(r, carry):
                src = pos_ref[2 * (tile * tm + r) + s]
                pltpu.make_async_copy(y_hbm.at[pl.ds(src, 1), :], ybuf.at[slot, s, pl.ds(r, 1), :], sem.at[slot]).start()
                return carry

            lax.fori_loop(0, tm, body, 0)

    @pl.when(i == 0)
    def _():
        start(0, 0)

    @pl.when(i + 1 < nt)
    def _():
        start(i + 1, (i + 1) % 2)

    slot = i % 2
    for s in range(2):
        _row_gather_wait(y_hbm, ybuf.at[slot, s], sem.at[slot])
    rt = rt_ref[...]
    o_ref[...] = x_ref[...] + rt[:, 2:3] * ybuf[slot, 0] + rt[:, 3:4] * ybuf[slot, 1]


def _combine(pos, x, routing, ys, *, tm):
    n, d = x.shape
    grid_spec = pltpu.PrefetchScalarGridSpec(
        num_scalar_prefetch=1,
        grid=(n // tm,),
        in_specs=[
            pl.BlockSpec((tm, d), lambda i, p: (i, 0)),
            pl.BlockSpec((tm, LANES), lambda i, p: (i, 0)),
            pl.BlockSpec(memory_space=pl.ANY),
        ],
        out_specs=pl.BlockSpec((tm, d), lambda i, p: (i, 0)),
        scratch_shapes=[pltpu.VMEM((2, 2, tm, d), F32), pltpu.SemaphoreType.DMA((2,))],
    )
    return pl.pallas_call(
        _combine_kernel,
        grid_spec=grid_spec,
        out_shape=jax.ShapeDtypeStruct((n, d), F32),
        compiler_params=_params(("arbitrary",)),
        name="moe_combine",
    )(pos, x, routing, ys)


def _hier_moe(x, g, w_group, b_group, w_router, b_router, w_gate, w_up, w_down):
    n, d = x.shape
    tm = MOE_TILE
    n_logit = N_GROUPS + N_EXPERTS
    w_r = jnp.pad(jnp.concatenate([w_group, w_router], axis=1), ((0, 0), (0, LANES - n_logit)))
    b_r = jnp.pad(jnp.concatenate([b_group, b_router]), (0, LANES - n_logit)).reshape(1, LANES)
    routing = _router(x, g, w_r, b_r, tm=512 if n % 512 == 0 else n)

    e_flat = routing[:, :2].astype(jnp.int32).reshape(-1)
    onehot = (e_flat[:, None] == jnp.arange(N_EXPERTS, dtype=jnp.int32)[None, :]).astype(jnp.int32)
    counts = jnp.sum(onehot, axis=0)
    rank = jnp.sum((jnp.cumsum(onehot, axis=0) - onehot) * onehot, axis=1)
    padded = ((counts + tm - 1) // tm) * tm
    ends = jnp.cumsum(padded)
    pos = (ends - padded)[e_flat] + rank
    p_rows = 2 * n + N_EXPERTS * tm
    src_tok = jnp.zeros((p_rows,), jnp.int32).at[pos].set(jnp.arange(2 * n, dtype=jnp.int32) // 2)
    tile_start = jnp.arange(p_rows // tm, dtype=jnp.int32) * tm
    tile_expert = jnp.minimum(jnp.searchsorted(ends, tile_start, side="right"), N_EXPERTS - 1).astype(jnp.int32)
    n_used = (ends[-1] // tm).astype(jnp.int32).reshape(1)

    ys = _experts(src_tok, tile_expert, n_used, x, g, w_gate.astype(BF16), w_up.astype(BF16), w_down.astype(BF16))
    return _combine(pos.astype(jnp.int32), x, routing, ys, tm=256 if n % 256 == 0 else n)


def _alibi_slopes():
    return jnp.exp2(-8.0 * jnp.arange(1, N_HEADS + 1, dtype=F32) / N_HEADS)


def _cmp_to_slc(n_cmp_pad, n_slc):
    cs = np.arange(n_cmp_pad) * CMP_STRIDE
    ss = np.arange(SLC_LANES) * SLC_BLOCK
    lo = np.maximum(cs[:, None], ss[None, :])
    hi = np.minimum(cs[:, None] + CMP_BLOCK, ss[None, :] + SLC_BLOCK)
    m = np.maximum(hi - lo, 0).astype(np.float32) / CMP_BLOCK
    m[:, n_slc:] = 0.0
    return m


def _nsa_mixer(x, g_norm, w_in, qk_gain, cmp_pe, cmp_w1, cmp_w2, w_out, *, b, t):
    n, d = x.shape
    q_dim = N_HEADS * HEAD_DIM
    kv_dim = N_KV * HEAD_DIM
    scale = HEAD_DIM ** -0.5
    tn = 512
    main = q_dim + 6 * kv_dim
    ones = jnp.ones((kv_dim,), F32)
    zeros = jnp.zeros((kv_dim,), F32)
    tile4 = lambda v: jnp.tile(v, N_KV)
    cgain = jnp.concatenate([jnp.tile(qk_gain[0], N_HEADS) * scale, ones, ones, tile4(qk_gain[2]), ones, tile4(qk_gain[3]), ones])
    nflag = jnp.concatenate([jnp.ones((q_dim,), F32), zeros, zeros, ones, zeros, ones, zeros])
    proj = _norm_matmul(x, g_norm, w_in[:, :main].astype(BF16), cgain, nflag, tm=512, tn=tn, out_dtype=BF16)
    n_gate = 3 * N_HEADS
    w_gate = jnp.pad(w_in[:, main:], ((0, 0), (0, LANES - n_gate))).astype(BF16)
    gate_logits = _norm_matmul(x, g_norm, w_gate, jnp.ones((LANES,), F32), jnp.zeros((LANES,), F32), tm=512, tn=LANES, out_dtype=F32)
    gr = gate_logits[:, :n_gate].reshape(n, 3, N_KV, GQA_REP).transpose(2, 0, 1, 3).reshape(N_KV, n, 3 * GQA_REP)
    gates_r = jnp.pad(gr, ((0, 0), (0, 0), (0, LANES - 3 * GQA_REP)))

    nc = t // CMP_STRIDE
    kvc = proj[:, q_dim:q_dim + 2 * kv_dim].reshape(b, nc, CMP_STRIDE, 2, N_KV, HEAD_DIM)
    chunks = kvc.transpose(0, 3, 4, 1, 2, 5).reshape(b, 2, N_KV, nc, CMP_STRIDE * HEAD_DIM)
    half = CMP_BLOCK // 2
    w1cat = jnp.concatenate([cmp_w1[:, :half].reshape(2, half * HEAD_DIM, -1), cmp_w1[:, half:].reshape(2, half * HEAD_DIM, -1)], axis=2)
    pe_rows = jnp.pad(cmp_pe.reshape(2, 2, half * HEAD_DIM), ((0, 0), (0, 6), (0, 0)))
    cmp_gain = jnp.stack([qk_gain[1], jnp.ones((HEAD_DIM,), F32)]).reshape(2, 1, HEAD_DIM)
    cmp_kv = _compress(chunks, w1cat.astype(BF16), pe_rows.astype(BF16), cmp_w2.astype(BF16), cmp_gain)

    slopes = _alibi_slopes()
    n_slc = t // SLC_BLOCK
    msel = jnp.asarray(_cmp_to_slc(nc, n_slc), BF16)
    q_blocks = q_dim // HEAD_DIM
    o_c, selbias = _cmp_attention(slopes, proj, cmp_kv, gates_r, msel, b=b, t=t, top_k=min(SLC_TOPK, n_slc))
    o_s = _slc_attention(slopes, proj, selbias, gates_r, b=b, t=t, k_col=q_blocks + 2 * N_KV, v_col=q_blocks + 3 * N_KV)
    o_w = _win_attention(slopes, proj, gates_r, b=b, t=t, k_col=q_blocks + 4 * N_KV, v_col=q_blocks + 5 * N_KV)
    return _proj_residual([o_c, o_s, o_w], w_out.astype(BF16), x, tm=512, tn=512)


def _sb_mixer(x, g_norm, w_in, w_out, *, b, t):
    q_dim = N_HEADS * HEAD_DIM
    scale = HEAD_DIM ** -0.5
    cgain = jnp.concatenate([jnp.full((q_dim,), scale, F32), jnp.ones((2 * q_dim,), F32)])
    proj = _norm_matmul(x, g_norm, w_in.astype(BF16), cgain, jnp.zeros((3 * q_dim,), F32), tm=512, tn=512, out_dtype=BF16)
    o = _sb_attention(proj, b=b, t=t)
    return _proj_residual([o], w_out.astype(BF16), x, tm=512, tn=512)


def kernel(x, norm_mix, norm_ffn, nsa_w_in, nsa_qk_gain, nsa_cmp_pe, nsa_cmp_w1, nsa_cmp_w2, nsa_w_out, sb_w_in, sb_w_out, moe_w_group, moe_b_group, moe_w_router, moe_b_router, moe_w_gate, moe_w_up, moe_w_down):
    b, t, d = x.shape
    depth = norm_mix.shape[0]
    xf = x.reshape(b * t, d)
    for i in range(depth):
        j = i // 2
        if i % 2 == 0:
            xf = _nsa_mixer(xf, norm_mix[i], nsa_w_in[j], nsa_qk_gain[j], nsa_cmp_pe[j], nsa_cmp_w1[j], nsa_cmp_w2[j], nsa_w_out[j], b=b, t=t)
        else:
            xf = _sb_mixer(xf, norm_mix[i], sb_w_in[j], sb_w_out[j], b=b, t=t)
        xf = _hier_moe(xf, norm_ffn[i], moe_w_group[i], moe_b_group[i], moe_w_router[i], moe_b_router[i], moe_w_gate[i], moe_w_up[i], moe_w_down[i])
    return xf.reshape(b, t, d)
```

```python
import functools

import jax
import jax.numpy as jnp
import numpy as np
from jax import lax
from jax.experimental import pallas as pl
from jax.experimental.pallas import tpu as pltpu

F32 = jnp.float32
BF16 = jnp.bfloat16

HEAD_DIM = 128
N_HEADS = 16
N_KV = 4
GQA_REP = N_HEADS // N_KV
CMP_BLOCK = 32
CMP_STRIDE = 16
SLC_BLOCK = 64
SLC_TOPK = 16
WINDOW = 512
Q_BLOCK = 128
N_GROUPS = 4
EXPERTS_PER_GROUP = 8
N_EXPERTS = N_GROUPS * EXPERTS_PER_GROUP
EPS = 1e-6
NEG = -1e30
LANES = 128
VMEM_LIMIT = 56 * 1024 * 1024

SLC_LANES = 128
SB_TILE = 256
SB_HEADS_PER_STEP = 4
SLC_KEY_TILE = 1024
LOG2E = 1.4426950408889634
SB_UNDERFLOW = -105.0
MOE_TILE = 256


def _dot(a, b):
    return jnp.dot(a, b, preferred_element_type=F32)


def _dot_nt(a, b):
    return lax.dot_general(a, b, (((1,), (1,)), ((), ())), preferred_element_type=F32)


def _params(sem):
    return pltpu.CompilerParams(dimension_semantics=sem, vmem_limit_bytes=VMEM_LIMIT)


def _norm_matmul_kernel(x_ref, g_ref, w_ref, cg_ref, nf_ref, o_ref, hn_ref):
    @pl.when(pl.program_id(1) == 0)
    def _():
        x = x_ref[...]
        ms = jnp.mean(x * x, axis=-1, keepdims=True)
        hn_ref[...] = (x * lax.rsqrt(ms + EPS) * g_ref[...]).astype(BF16)

    y = _dot(hn_ref[...], w_ref[...])
    for hh in range(y.shape[1] // LANES):
        sl = slice(hh * LANES, (hh + 1) * LANES)
        yh = y[:, sl]
        ms = jnp.mean(yh * yh, axis=-1, keepdims=True)
        nf = nf_ref[0, :, sl]
        fac = nf * lax.rsqrt(ms + EPS) + (1.0 - nf)
        o_ref[:, sl] = (yh * fac * cg_ref[0, :, sl]).astype(o_ref.dtype)


def _norm_matmul(x, g, w, cgain, nflag, *, tm, tn, out_dtype):
    n, d = x.shape
    m = w.shape[1]
    nj = m // tn
    return pl.pallas_call(
        _norm_matmul_kernel,
        grid=(n // tm, nj),
        in_specs=[
            pl.BlockSpec((tm, d), lambda i, j: (i, 0)),
            pl.BlockSpec((1, d), lambda i, j: (0, 0)),
            pl.BlockSpec((d, tn), lambda i, j: (0, j)),
            pl.BlockSpec((1, 1, tn), lambda i, j: (j, 0, 0)),
            pl.BlockSpec((1, 1, tn), lambda i, j: (j, 0, 0)),
        ],
        out_specs=pl.BlockSpec((tm, tn), lambda i, j: (i, j)),
        out_shape=jax.ShapeDtypeStruct((n, m), out_dtype),
        scratch_shapes=[pltpu.VMEM((tm, d), BF16)],
        compiler_params=_params(("parallel", "arbitrary")),
        name="norm_matmul",
    )(x, g.reshape(1, d), w, cgain.reshape(nj, 1, tn), nflag.reshape(nj, 1, tn))


def _proj_residual_kernel(*refs, n_in):
    a_refs = refs[:n_in]
    w_ref, r_ref, o_ref, a_scr = refs[n_in:]

    @pl.when(pl.program_id(1) == 0)
    def _():
        acc = a_refs[0][...].astype(F32)
        for a in a_refs[1:]:
            acc = acc + a[...].astype(F32)
        a_scr[...] = acc.astype(BF16)

    o_ref[...] = r_ref[...] + _dot(a_scr[...], w_ref[...])


def _proj_residual(branches, w, res, *, tm, tn):
    n, k = branches[0].shape
    m = w.shape[1]
    n_in = len(branches)
    return pl.pallas_call(
        functools.partial(_proj_residual_kernel, n_in=n_in),
        grid=(n // tm, m // tn),
        in_specs=[pl.BlockSpec((tm, k), lambda i, j: (i, 0)) for _ in range(n_in)]
        + [pl.BlockSpec((k, tn), lambda i, j: (0, j)), pl.BlockSpec((tm, tn), lambda i, j: (i, j))],
        out_specs=pl.BlockSpec((tm, tn), lambda i, j: (i, j)),
        out_shape=jax.ShapeDtypeStruct((n, m), F32),
        scratch_shapes=[pltpu.VMEM((tm, k), BF16)],
        compiler_params=_params(("parallel", "arbitrary")),
        name="proj_residual",
    )(*branches, w, res)


def _cmp_kernel(ch_ref, w1_ref, pe_ref, w2_ref, gain_ref, o_ref):
    nc = ch_ref.shape[3]
    w1 = w1_ref[0]
    p = _dot(ch_ref[0, 0, 0], w1)
    pb = _dot(pe_ref[0], w1)
    bias = pb[0:1, :HEAD_DIM] + pb[1:2, HEAD_DIM:]
    second = pltpu.roll(p[:, HEAD_DIM:], shift=nc - 1, axis=0)
    hid = jax.nn.gelu(p[:, :HEAD_DIM] + second + bias)
    c = _dot(hid.astype(BF16), w2_ref[0])
    nf = (pl.program_id(1) == 0).astype(F32)
    ms = jnp.mean(c * c, axis=-1, keepdims=True)
    fac = nf * lax.rsqrt(ms + EPS) + (1.0 - nf)
    o_ref[0, 0, 0] = (c * fac * gain_ref[0]).astype(o_ref.dtype)


def _compress(chunks, w1cat, pe_rows, w2, gains):
    b, _, g, nc, ck = chunks.shape
    return pl.pallas_call(
        _cmp_kernel,
        grid=(b, 2, g),
        in_specs=[
            pl.BlockSpec((1, 1, 1, nc, ck), lambda bi, k, gi: (bi, k, gi, 0, 0)),
            pl.BlockSpec((1, ck, 2 * HEAD_DIM), lambda bi, k, gi: (k, 0, 0)),
            pl.BlockSpec((1, 8, ck), lambda bi, k, gi: (k, 0, 0)),
            pl.BlockSpec((1, HEAD_DIM, HEAD_DIM), lambda bi, k, gi: (k, 0, 0)),
            pl.BlockSpec((1, 1, HEAD_DIM), lambda bi, k, gi: (k, 0, 0)),
        ],
        out_specs=pl.BlockSpec((1, 1, 1, nc, HEAD_DIM), lambda bi, k, gi: (bi, k, gi, 0, 0)),
        out_shape=jax.ShapeDtypeStruct((b, 2, g, nc, HEAD_DIM), BF16),
        compiler_params=_params(("parallel", "parallel", "parallel")),
        name="nsa_compress",
    )(chunks, w1cat, pe_rows, w2, gains)


def _cmp_attn_kernel(slopes_ref, q_ref, kc_ref, vc_ref, gate_ref, msel_ref, oc_ref, sb_ref, *, top_k):
    g = pl.program_id(1)
    t0 = pl.program_id(2) * Q_BLOCK
    kc = kc_ref[0, 0, 0]
    vc = vc_ref[0, 0, 0]
    nc = kc.shape[0]
    t_col = t0 + lax.broadcasted_iota(jnp.int32, (Q_BLOCK, 1), 0)
    cend = lax.broadcasted_iota(jnp.int32, (1, nc), 1) * CMP_STRIDE + (CMP_BLOCK - 1)
    dist_i = t_col - cend
    mask = dist_i >= 0
    dist = dist_i.astype(F32)
    gates = jax.nn.sigmoid(gate_ref[0])
    psum = jnp.zeros((Q_BLOCK, nc), F32)
    for r in range(GQA_REP):
        slope = slopes_ref[g * GQA_REP + r]
        s = _dot_nt(q_ref[:, r * HEAD_DIM:(r + 1) * HEAD_DIM], kc)
        s = jnp.where(mask, s - slope * dist, NEG)
        e = jnp.exp2(s - jnp.max(s, axis=-1, keepdims=True))
        p = jnp.where(mask, e / jnp.sum(e, axis=-1, keepdims=True), 0.0)
        psum = psum + p
        o = _dot(p.astype(BF16), vc)
        oc_ref[:, r * HEAD_DIM:(r + 1) * HEAD_DIM] = (o * gates[:, r:r + 1]).astype(oc_ref.dtype)

    msel = msel_ref[...]
    p_hi = psum.astype(BF16)
    rem = psum - p_hi.astype(F32)
    p_mid = rem.astype(BF16)
    p_lo = (rem - p_mid.astype(F32)).astype(BF16)
    imp = _dot(p_hi, msel) + _dot(p_mid, msel) + _dot(p_lo, msel)

    imp_t = imp.T
    j_col = lax.broadcasted_iota(jnp.int32, (SLC_LANES, 1), 0)
    t_row = t0 + lax.broadcasted_iota(jnp.int32, (1, Q_BLOCK), 1)
    forced = (j_col == 0) | (j_col == (t_row >> 6))
    causal = (j_col << 6) <= t_row
    score = jnp.where(forced, -NEG, jnp.where(causal, imp_t, NEG))
    keep = jnp.zeros((SLC_LANES, Q_BLOCK), F32)
    for _ in range(top_k):
        m = jnp.max(score, axis=0, keepdims=True)
        first = jnp.min(jnp.where(score == m, j_col, SLC_LANES), axis=0, keepdims=True)
        hit = j_col == first
        keep = jnp.where(hit, (m > 0.5 * NEG).astype(F32), keep)
        score = jnp.where(hit, -3e38, score)
    bias_t = jnp.where(keep > 0.5, 0.0, NEG)
    sb_ref[0] = bias_t.T.astype(sb_ref.dtype)


def _cmp_attention(slopes, proj, cmp_kv, gates_r, msel, *, b, t, top_k):
    n = b * t
    nqb = t // Q_BLOCK
    nc = cmp_kv.shape[3]
    rowblk = lambda bi, gi, c: bi * nqb + c
    return pl.pallas_call(
        functools.partial(_cmp_attn_kernel, top_k=top_k),
        grid=(b, N_KV, nqb),
        in_specs=[
            pl.BlockSpec(memory_space=pltpu.SMEM),
            pl.BlockSpec((Q_BLOCK, GQA_REP * HEAD_DIM), lambda bi, gi, c: (rowblk(bi, gi, c), gi)),
            pl.BlockSpec((1, 1, 1, nc, HEAD_DIM), lambda bi, gi, c: (bi, 0, gi, 0, 0)),
            pl.BlockSpec((1, 1, 1, nc, HEAD_DIM), lambda bi, gi, c: (bi, 1, gi, 0, 0)),
            pl.BlockSpec((1, Q_BLOCK, LANES), lambda bi, gi, c: (gi, rowblk(bi, gi, c), 0)),
            pl.BlockSpec((nc, SLC_LANES), lambda bi, gi, c: (0, 0)),
        ],
        out_specs=[
            pl.BlockSpec((Q_BLOCK, GQA_REP * HEAD_DIM), lambda bi, gi, c: (rowblk(bi, gi, c), gi)),
            pl.BlockSpec((1, Q_BLOCK, SLC_LANES), lambda bi, gi, c: (gi, rowblk(bi, gi, c), 0)),
        ],
        out_shape=[
            jax.ShapeDtypeStruct((n, N_HEADS * HEAD_DIM), BF16),
            jax.ShapeDtypeStruct((N_KV, n, SLC_LANES), BF16),
        ],
        compiler_params=_params(("parallel", "parallel", "parallel")),
        name="nsa_cmp_attn",
    )(slopes, proj, cmp_kv, cmp_kv, gates_r, msel)


def _win_attn_kernel(slopes_ref, q_ref, k_ref, v_ref, gate_ref, o_ref):
    g = pl.program_id(1)
    t0 = pl.program_id(2) * Q_BLOCK
    span = Q_BLOCK + WINDOW
    start = pl.multiple_of(jnp.maximum(t0 - WINDOW, 0), Q_BLOCK)
    k = k_ref[pl.ds(start, span), :]
    v = v_ref[pl.ds(start, span), :]
    t_col = t0 + lax.broadcasted_iota(jnp.int32, (Q_BLOCK, 1), 0)
    pos = start + lax.broadcasted_iota(jnp.int32, (1, span), 1)
    dist_i = t_col - pos
    mask = (dist_i >= 0) & (dist_i < WINDOW)
    dist = dist_i.astype(F32)
    gates = jax.nn.sigmoid(gate_ref[0])
    for r in range(GQA_REP):
        slope = slopes_ref[g * GQA_REP + r]
        s = _dot_nt(q_ref[:, r * HEAD_DIM:(r + 1) * HEAD_DIM], k)
        s = jnp.where(mask, s - slope * dist, NEG)
        e = jnp.exp2(s - jnp.max(s, axis=-1, keepdims=True))
        p = jnp.where(mask, e / jnp.sum(e, axis=-1, keepdims=True), 0.0)
        o = _dot(p.astype(BF16), v)
        gate = gates[:, 2 * GQA_REP + r:2 * GQA_REP + r + 1]
        o_ref[:, r * HEAD_DIM:(r + 1) * HEAD_DIM] = (o * gate).astype(o_ref.dtype)


def _win_attention(slopes, proj, gates_r, *, b, t, k_col, v_col):
    n = b * t
    nqb = t // Q_BLOCK
    rowblk = lambda bi, gi, c: bi * nqb + c
    return pl.pallas_call(
        _win_attn_kernel,
        grid=(b, N_KV, nqb),
        in_specs=[
            pl.BlockSpec(memory_space=pltpu.SMEM),
            pl.BlockSpec((Q_BLOCK, GQA_REP * HEAD_DIM), lambda bi, gi, c: (rowblk(bi, gi, c), gi)),
            pl.BlockSpec((t, HEAD_DIM), lambda bi, gi, c: (bi, k_col + gi)),
            pl.BlockSpec((t, HEAD_DIM), lambda bi, gi, c: (bi, v_col + gi)),
            pl.BlockSpec((1, Q_BLOCK, LANES), lambda bi, gi, c: (gi, rowblk(bi, gi, c), 0)),
        ],
        out_specs=pl.BlockSpec((Q_BLOCK, GQA_REP * HEAD_DIM), lambda bi, gi, c: (rowblk(bi, gi, c), gi)),
        out_shape=jax.ShapeDtypeStruct((n, N_HEADS * HEAD_DIM), BF16),
        compiler_params=_params(("parallel", "parallel", "arbitrary")),
        name="nsa_win_attn",
    )(slopes, proj, proj, proj, gates_r)


def _slc_attn_kernel(q_ref, k_ref, v_ref, sb_ref, gate_ref, srow_ref, scol_ref, o_ref, qa_ref, ka_ref, va_ref, m_ref, acc_ref, s_ref):
    c = pl.program_id(2)
    t0 = c * Q_BLOCK
    rows = GQA_REP * Q_BLOCK
    tk = SLC_KEY_TILE
    nb = tk // SLC_BLOCK
    t_len = k_ref.shape[0]

    @pl.when(c == 0)
    def _():
        pos = lax.broadcasted_iota(jnp.int32, (t_len, LANES), 0)
        lane = lax.broadcasted_iota(jnp.int32, (t_len, LANES), 1)
        off = pos & (tk - 1)
        blk = (pos >> 6) & (nb - 1)
        aug = jnp.where(lane < nb, (blk == lane).astype(F32),
                        jnp.where(lane < nb + 3, ((off >> 3) << 3).astype(F32),
                                  jnp.where(lane < nb + 6, (off & 7).astype(F32), 0.0)))
        ka_ref[:, :HEAD_DIM] = k_ref[...]
        ka_ref[:, HEAD_DIM:] = aug.astype(BF16)
        va_ref[:, :HEAD_DIM] = v_ref[...]
        va_ref[:, HEAD_DIM:] = (lane == 0).astype(BF16)

    for r in range(GQA_REP):
        qa_ref[r * Q_BLOCK:(r + 1) * Q_BLOCK, :HEAD_DIM] = q_ref[:, r * HEAD_DIM:(r + 1) * HEAD_DIM]
    m_ref[...] = jnp.full(m_ref.shape, NEG, F32)
    acc_ref[...] = jnp.zeros(acc_ref.shape, F32)
    sb = sb_ref[0]
    scol = scol_ref[0]
    t_col = t0 + (lax.broadcasted_iota(jnp.int32, (rows, 1), 0) & (Q_BLOCK - 1))
    k_iota = lax.broadcasted_iota(jnp.int32, (1, tk), 1)
    p_row = lax.broadcasted_iota(jnp.int32, (SLC_LANES, LANES), 0)
    p_col = lax.broadcasted_iota(jnp.int32, (SLC_LANES, LANES), 1)

    def scores(kt):
        k0 = pl.multiple_of(kt * tk, tk)
        pick = ((p_row == kt * nb + p_col) & (p_col < nb)).astype(BF16)
        sbt = _dot(sb, pick).astype(BF16)
        for r in range(GQA_REP):
            qa_ref[r * Q_BLOCK:(r + 1) * Q_BLOCK, HEAD_DIM:] = sbt + srow_ref[0, r:r + 1, :]
        return _dot_nt(qa_ref[...], ka_ref[pl.ds(k0, tk), :])

    def softmax_pv(kt, s, causal):
        k0 = pl.multiple_of(kt * tk, tk)
        if causal:
            s = jnp.where(k0 + k_iota <= t_col, s, NEG)
        shift = scol * (k0 - t0).astype(F32)
        m_old = m_ref[...]
        m_new = jnp.maximum(m_old, jnp.max(s, axis=-1, keepdims=True) + shift)
        alpha = jnp.exp2(m_old - m_new)
        p = jnp.exp2(s - (m_new - shift))
        acc_ref[...] = alpha * acc_ref[...] + _dot(p.astype(BF16), va_ref[pl.ds(k0, tk), :])
        m_ref[...] = m_new

    n_below = lax.div(t0, tk)
    s_ref[...] = scores(0)

    def body(kt, carry):
        s_cur = s_ref[...]
        s_next = scores(kt + 1)
        softmax_pv(kt, s_cur, False)
        s_ref[...] = s_next
        return carry

    lax.fori_loop(0, n_below, body, 0)
    softmax_pv(n_below, s_ref[...], True)
    gates = jax.nn.sigmoid(gate_ref[0])
    acc = acc_ref[...]
    o = acc[:, :HEAD_DIM] / acc[:, HEAD_DIM:HEAD_DIM + 1]
    for r in range(GQA_REP):
        gate = gates[:, GQA_REP + r:GQA_REP + r + 1]
        o_ref[:, r * HEAD_DIM:(r + 1) * HEAD_DIM] = (o[r * Q_BLOCK:(r + 1) * Q_BLOCK] * gate).astype(o_ref.dtype)


def _slc_attention(slopes, proj, selbias, gates_r, *, b, t, k_col, v_col):
    n = b * t
    nqb = t // Q_BLOCK
    rows = GQA_REP * Q_BLOCK
    nb = SLC_KEY_TILE // SLC_BLOCK
    rowblk = lambda bi, gi, c: bi * nqb + c
    s_hi = slopes.astype(BF16)
    s_mid = (slopes - s_hi.astype(F32)).astype(BF16)
    s_lo = (slopes - s_hi.astype(F32) - s_mid.astype(F32)).astype(BF16)
    pieces = jnp.stack([s_hi, s_mid, s_lo, s_hi, s_mid, s_lo], axis=1)
    srow = jnp.pad(pieces, ((0, 0), (nb, LANES - nb - 6))).reshape(N_KV, GQA_REP, LANES)
    srow = jnp.pad(srow, ((0, 0), (0, 8 - GQA_REP), (0, 0)))
    scol = jnp.repeat(slopes.reshape(N_KV, GQA_REP), Q_BLOCK, axis=1).reshape(N_KV, rows, 1)
    return pl.pallas_call(
        _slc_attn_kernel,
        grid=(b, N_KV, nqb),
        in_specs=[
            pl.BlockSpec((Q_BLOCK, GQA_REP * HEAD_DIM), lambda bi, gi, c: (rowblk(bi, gi, c), gi)),
            pl.BlockSpec((t, HEAD_DIM), lambda bi, gi, c: (bi, k_col + gi)),
            pl.BlockSpec((t, HEAD_DIM), lambda bi, gi, c: (bi, v_col + gi)),
            pl.BlockSpec((1, Q_BLOCK, SLC_LANES), lambda bi, gi, c: (gi, rowblk(bi, gi, c), 0)),
            pl.BlockSpec((1, Q_BLOCK, LANES), lambda bi, gi, c: (gi, rowblk(bi, gi, c), 0)),
            pl.BlockSpec((1, 8, LANES), lambda bi, gi, c: (gi, 0, 0)),
            pl.BlockSpec((1, rows, 1), lambda bi, gi, c: (gi, 0, 0)),
        ],
        out_specs=pl.BlockSpec((Q_BLOCK, GQA_REP * HEAD_DIM), lambda bi, gi, c: (rowblk(bi, gi, c), gi)),
        out_shape=jax.ShapeDtypeStruct((n, N_HEADS * HEAD_DIM), BF16),
        scratch_shapes=[
            pltpu.VMEM((rows, 2 * HEAD_DIM), BF16),
            pltpu.VMEM((t, 2 * HEAD_DIM), BF16),
            pltpu.VMEM((t, 2 * HEAD_DIM), BF16),
            pltpu.VMEM((rows, 1), F32),
            pltpu.VMEM((rows, 2 * HEAD_DIM), F32),
            pltpu.VMEM((rows, SLC_KEY_TILE), F32),
        ],
        compiler_params=_params(("parallel", "parallel", "arbitrary")),
        name="nsa_slc_attn",
    )(proj, proj, proj, selbias, gates_r, srow, scol)


def _sb_attn_kernel(q_ref, k_ref, v_ref, o_ref, acc_ref, carry_ref):
    ts = SB_TILE
    c = pl.program_id(2)
    acc_ref[...] = jnp.zeros(acc_ref.shape, F32)
    carry_ref[...] = jnp.zeros(carry_ref.shape, F32)
    r_idx = lax.broadcasted_iota(jnp.int32, (ts, 1), 0)
    c_idx = lax.broadcasted_iota(jnp.int32, (1, ts), 1)
    tri = (r_idx > c_idx).astype(BF16)
    diag = c_idx < r_idx

    def tile(jt, valid):
        k0 = pl.multiple_of(jt * ts, ts)
        heads = range(SB_HEADS_PER_STEP)
        cols = [slice(e * HEAD_DIM, (e + 1) * HEAD_DIM) for e in heads]
        carries = [carry_ref[e] for e in heads]
        zs = [_dot_nt(q_ref[:, cols[e]], k_ref[pl.ds(k0, ts), cols[e]]) for e in heads]
        lfs = []
        for e in heads:
            lf = -(jnp.maximum(zs[e], 0.0) + jnp.log2(1.0 + jnp.exp2(-jnp.abs(zs[e]))))
            lfs.append(lf if valid is None else jnp.where(valid, lf, 0.0))
        afters = []
        for e in heads:
            hi = lfs[e].astype(BF16)
            lo = (lfs[e] - hi.astype(F32)).astype(BF16)
            afters.append(_dot(hi, tri) + _dot(lo, tri))
        avs = []
        for e in heads:
            a = jnp.exp2(zs[e] + lfs[e] + afters[e] + carries[e])
            if valid is not None:
                a = jnp.where(valid, a, 0.0)
            avs.append(_dot(a.astype(BF16), v_ref[pl.ds(k0, ts), cols[e]]))
        for e in heads:
            acc_ref[e] += avs[e]
            carry_ref[e] = carries[e] + jnp.sum(lfs[e], axis=-1, keepdims=True)

    def live():
        return (jnp.max(carry_ref[...]) > SB_UNDERFLOW * LOG2E).astype(jnp.int32)

    tile(c, diag)

    def cond(state):
        jt, go = state
        return (jt >= 0) & (go > 0)

    def body(state):
        jt, _ = state
        tile(jt, None)
        return jt - 1, live()

    lax.while_loop(cond, body, (c - 1, live()))
    for e in range(SB_HEADS_PER_STEP):
        o_ref[:, e * HEAD_DIM:(e + 1) * HEAD_DIM] = acc_ref[e].astype(o_ref.dtype)


def _sb_attention(proj, *, b, t):
    n = b * t
    ts = SB_TILE
    nq = t // ts
    hs = SB_HEADS_PER_STEP
    width = hs * HEAD_DIM
    nhp = N_HEADS // hs
    return pl.pallas_call(
        _sb_attn_kernel,
        grid=(b, nhp, nq),
        in_specs=[
            pl.BlockSpec((ts, width), lambda bi, h, c: (bi * nq + c, h)),
            pl.BlockSpec((t, width), lambda bi, h, c: (bi, nhp + h)),
            pl.BlockSpec((t, width), lambda bi, h, c: (bi, 2 * nhp + h)),
        ],
        out_specs=pl.BlockSpec((ts, width), lambda bi, h, c: (bi * nq + c, h)),
        out_shape=jax.ShapeDtypeStruct((n, N_HEADS * HEAD_DIM), BF16),
        scratch_shapes=[pltpu.VMEM((hs, ts, HEAD_DIM), F32), pltpu.VMEM((hs, ts, 1), F32)],
        compiler_params=_params(("parallel", "parallel", "arbitrary")),
        name="sb_attn",
    )(proj, proj, proj)


def _first_max(vals, lane):
    m = jnp.max(vals, axis=-1, keepdims=True)
    idx = jnp.min(jnp.where(vals == m, lane, LANES), axis=-1, keepdims=True)
    return m, idx


def _router_kernel(x_ref, g_ref, w_ref, b_ref, o_ref):
    x = x_ref[...]
    ms = jnp.mean(x * x, axis=-1, keepdims=True)
    h = x * lax.rsqrt(ms + EPS) * g_ref[...]
    logits = jnp.dot(h, w_ref[...], preferred_element_type=F32, precision=lax.Precision.HIGHEST) + b_ref[...]
    lane = lax.broadcasted_iota(jnp.int32, (1, LANES), 1)
    gl = jnp.where(lane < N_GROUPS, logits, NEG)
    ge = jnp.exp(gl - jnp.max(gl, axis=-1, keepdims=True))
    pg_all = ge / jnp.sum(ge, axis=-1, keepdims=True)
    pg, gsel = _first_max(jnp.where(lane < N_GROUPS, pg_all, -1.0), lane)
    lo = N_GROUPS + gsel * EXPERTS_PER_GROUP
    in_group = (lane >= lo) & (lane < lo + EXPERTS_PER_GROUP)
    el = jnp.where(in_group, logits, NEG)
    ee = jnp.exp(el - jnp.max(el, axis=-1, keepdims=True))
    pe_all = jnp.where(in_group, ee / jnp.sum(ee, axis=-1, keepdims=True), -1.0)
    p1, i1 = _first_max(pe_all, lane)
    p2, i2 = _first_max(jnp.where(lane == i1, -1.0, pe_all), lane)
    denom = p1 + p2
    w1 = pg * p1 / denom
    w2 = pg * p2 / denom
    e1 = (i1 - N_GROUPS).astype(F32)
    e2 = (i2 - N_GROUPS).astype(F32)
    o_ref[...] = jnp.where(lane == 0, e1, jnp.where(lane == 1, e2, jnp.where(lane == 2, w1, jnp.where(lane == 3, w2, 0.0))))


def _router(x, g, w, bias, *, tm):
    n, d = x.shape
    return pl.pallas_call(
        _router_kernel,
        grid=(n // tm,),
        in_specs=[
            pl.BlockSpec((tm, d), lambda i: (i, 0)),
            pl.BlockSpec((1, d), lambda i: (0, 0)),
            pl.BlockSpec((d, LANES), lambda i: (0, 0)),
            pl.BlockSpec((1, LANES), lambda i: (0, 0)),
        ],
        out_specs=pl.BlockSpec((tm, LANES), lambda i: (i, 0)),
        out_shape=jax.ShapeDtypeStruct((n, LANES), F32),
        compiler_params=_params(("parallel",)),
        name="moe_router",
    )(x, g.reshape(1, d), w, bias)


def _row_gather_start(src_hbm, dst_ref, sem, idx_ref, base, count):
    def body(r, carry):
        pltpu.make_async_copy(src_hbm.at[pl.ds(idx_ref[base + r], 1), :], dst_ref.at[pl.ds(r, 1), :], sem).start()
        return carry

    lax.fori_loop(0, count, body, 0, unroll=8)


def _row_gather_wait(src_hbm, dst_ref, sem):
    pltpu.make_async_copy(src_hbm.at[pl.ds(0, dst_ref.shape[0]), :], dst_ref, sem).wait()


def _expert_kernel(src_ref, te_ref, nu_ref, x_hbm, g_ref, wg_ref, wu_ref, wd_ref, o_ref, xbuf, wgb, wub, wdb, sem):
    tm = MOE_TILE
    i = pl.program_id(0)
    n_used = nu_ref[0]

    @pl.when((i < n_used) & ((i == 0) | (te_ref[i] != te_ref[jnp.maximum(i - 1, 0)])))
    def _():
        wgb[...] = wg_ref[0].astype(BF16)
        wub[...] = wu_ref[0].astype(BF16)
        wdb[...] = wd_ref[0].astype(BF16)

    @pl.when((i == 0) & (n_used > 0))
    def _():
        _row_gather_start(x_hbm, xbuf.at[0], sem.at[0], src_ref, 0, tm)

    @pl.when(i + 1 < n_used)
    def _():
        nxt = (i + 1) % 2
        _row_gather_start(x_hbm, xbuf.at[nxt], sem.at[nxt], src_ref, (i + 1) * tm, tm)

    @pl.when(i < n_used)
    def _():
        slot = i % 2
        _row_gather_wait(x_hbm, xbuf.at[slot], sem.at[slot])
        x = xbuf[slot]
        ms = jnp.mean(x * x, axis=-1, keepdims=True)
        h = (x * lax.rsqrt(ms + EPS) * g_ref[...]).astype(BF16)
        gate = _dot(h, wgb[...])
        up = _dot(h, wub[...])
        act = (gate * jax.nn.sigmoid(gate) * up).astype(BF16)
        o_ref[...] = _dot(act, wdb[...])

    @pl.when(i >= n_used)
    def _():
        o_ref[...] = jnp.zeros(o_ref.shape, o_ref.dtype)


def _experts(src_tok, tile_expert, n_used, x, g, wg, wu, wd):
    n, d = x.shape
    tm = MOE_TILE
    p_rows = src_tok.shape[0]
    de = wg.shape[2]
    grid_spec = pltpu.PrefetchScalarGridSpec(
        num_scalar_prefetch=3,
        grid=(p_rows // tm,),
        in_specs=[
            pl.BlockSpec(memory_space=pl.ANY),
            pl.BlockSpec((1, d), lambda i, s, te, nu: (0, 0)),
            pl.BlockSpec((1, d, de), lambda i, s, te, nu: (te[i], 0, 0)),
            pl.BlockSpec((1, d, de), lambda i, s, te, nu: (te[i], 0, 0)),
            pl.BlockSpec((1, de, d), lambda i, s, te, nu: (te[i], 0, 0)),
        ],
        out_specs=pl.BlockSpec((tm, d), lambda i, s, te, nu: (i, 0)),
        scratch_shapes=[
            pltpu.VMEM((2, tm, d), F32),
            pltpu.VMEM((d, de), BF16),
            pltpu.VMEM((d, de), BF16),
            pltpu.VMEM((de, d), BF16),
            pltpu.SemaphoreType.DMA((2,)),
        ],
    )
    return pl.pallas_call(
        _expert_kernel,
        grid_spec=grid_spec,
        out_shape=jax.ShapeDtypeStruct((p_rows, d), F32),
        compiler_params=_params(("arbitrary",)),
        name="moe_experts",
    )(src_tok, tile_expert, n_used, x, g.reshape(1, d), wg, wu, wd)


def _combine_kernel(pos_ref, x_ref, rt_ref, y_hbm, o_ref, ybuf, sem):
    tm = x_ref.shape[0]
    i = pl.program_id(0)
    nt = pl.num_programs(0)

    def start(tile, slot):
        for s in range(2):
            def body(r, carry):
                src = pos_ref[2 * (tile * tm + r) + s]
                pltpu.make_async_copy(y_hbm.at[pl.ds(src, 1), :], ybuf.at[slot, s, pl.ds(r, 1), :], sem.at[slot]).start()
                return carry

            lax.fori_loop(0, tm, body, 0, unroll=8)

    @pl.when(i == 0)
    def _():
        start(0, 0)

    @pl.when(i + 1 < nt)
    def _():
        start(i + 1, (i + 1) % 2)

    slot = i % 2
    for s in range(2):
        _row_gather_wait(y_hbm, ybuf.at[slot, s], sem.at[slot])
    rt = rt_ref[...]
    o_ref[...] = x_ref[...] + rt[:, 2:3] * ybuf[slot, 0] + rt[:, 3:4] * ybuf[slot, 1]


def _combine(pos, x, routing, ys, *, tm):
    n, d = x.shape
    grid_spec = pltpu.PrefetchScalarGridSpec(
        num_scalar_prefetch=1,
        grid=(n // tm,),
        in_specs=[
            pl.BlockSpec((tm, d), lambda i, p: (i, 0)),
            pl.BlockSpec((tm, LANES), lambda i, p: (i, 0)),
            pl.BlockSpec(memory_space=pl.ANY),
        ],
        out_specs=pl.BlockSpec((tm, d), lambda i, p: (i, 0)),
        scratch_shapes=[pltpu.VMEM((2, 2, tm, d), F32), pltpu.SemaphoreType.DMA((2,))],
    )
    return pl.pallas_call(
        _combine_kernel,
        grid_spec=grid_spec,
        out_shape=jax.ShapeDtypeStruct((n, d), F32),
        compiler_params=_params(("arbitrary",)),
        name="moe_combine",
    )(pos, x, routing, ys)


def _hier_moe(x, g, w_group, b_group, w_router, b_router, w_gate, w_up, w_down):
    n, d = x.shape
    tm = MOE_TILE
    n_logit = N_GROUPS + N_EXPERTS
    w_r = jnp.pad(jnp.concatenate([w_group, w_router], axis=1), ((0, 0), (0, LANES - n_logit)))
    b_r = jnp.pad(jnp.concatenate([b_group, b_router]), (0, LANES - n_logit)).reshape(1, LANES)
    routing = _router(x, g, w_r, b_r, tm=512 if n % 512 == 0 else n)

    e_flat = routing[:, :2].astype(jnp.int32).reshape(-1)
    onehot = (e_flat[:, None] == jnp.arange(N_EXPERTS, dtype=jnp.int32)[None, :]).astype(jnp.int32)
    counts = jnp.sum(onehot, axis=0)
    rank = jnp.sum((jnp.cumsum(onehot, axis=0) - onehot) * onehot, axis=1)
    padded = ((counts + tm - 1) // tm) * tm
    ends = jnp.cumsum(padded)
    pos = (ends - padded)[e_flat] + rank
    p_rows = 2 * n + N_EXPERTS * tm
    src_tok = jnp.zeros((p_rows,), jnp.int32).at[pos].set(jnp.arange(2 * n, dtype=jnp.int32) // 2)
    tile_start = jnp.arange(p_rows // tm, dtype=jnp.int32) * tm
    tile_expert = jnp.minimum(jnp.sum((ends[None, :] <= tile_start[:, None]).astype(jnp.int32), axis=1), N_EXPERTS - 1)
    n_used = (ends[-1] // tm).astype(jnp.int32).reshape(1)

    ys = _experts(src_tok, tile_expert, n_used, x, g, w_gate, w_up, w_down)
    return _combine(pos.astype(jnp.int32), x, routing, ys, tm=256 if n % 256 == 0 else n)


def _alibi_slopes():
    return jnp.exp2(-8.0 * jnp.arange(1, N_HEADS + 1, dtype=F32) / N_HEADS)


def _cmp_to_slc(n_cmp_pad, n_slc):
    cs = np.arange(n_cmp_pad) * CMP_STRIDE
    ss = np.arange(SLC_LANES) * SLC_BLOCK
    lo = np.maximum(cs[:, None], ss[None, :])
    hi = np.minimum(cs[:, None] + CMP_BLOCK, ss[None, :] + SLC_BLOCK)
    m = np.maximum(hi - lo, 0).astype(np.float32) / CMP_BLOCK
    m[:, n_slc:] = 0.0
    return m


def _nsa_mixer(x, g_norm, w_in, qk_gain, cmp_pe, cmp_w1, cmp_w2, w_out, *, b, t):
    n, d = x.shape
    q_dim = N_HEADS * HEAD_DIM
    kv_dim = N_KV * HEAD_DIM
    scale = HEAD_DIM ** -0.5 * LOG2E
    tn = 512
    main = q_dim + 6 * kv_dim
    ones = jnp.ones((kv_dim,), F32)
    zeros = jnp.zeros((kv_dim,), F32)
    tile4 = lambda v: jnp.tile(v, N_KV)
    cgain = jnp.concatenate([jnp.tile(qk_gain[0], N_HEADS) * scale, ones, ones, tile4(qk_gain[2]), ones, tile4(qk_gain[3]), ones])
    nflag = jnp.concatenate([jnp.ones((q_dim,), F32), zeros, zeros, ones, zeros, ones, zeros])
    proj = _norm_matmul(x, g_norm, w_in[:, :main].astype(BF16), cgain, nflag, tm=512, tn=tn, out_dtype=BF16)
    n_gate = 3 * N_HEADS
    w_gate = jnp.pad(w_in[:, main:], ((0, 0), (0, LANES - n_gate))).astype(BF16)
    gate_logits = _norm_matmul(x, g_norm, w_gate, jnp.ones((LANES,), F32), jnp.zeros((LANES,), F32), tm=512, tn=LANES, out_dtype=F32)
    gr = gate_logits[:, :n_gate].reshape(n, 3, N_KV, GQA_REP).transpose(2, 0, 1, 3).reshape(N_KV, n, 3 * GQA_REP)
    gates_r = jnp.pad(gr, ((0, 0), (0, 0), (0, LANES - 3 * GQA_REP)))

    nc = t // CMP_STRIDE
    kvc = proj[:, q_dim:q_dim + 2 * kv_dim].reshape(b, nc, CMP_STRIDE, 2, N_KV, HEAD_DIM)
    chunks = kvc.transpose(0, 3, 4, 1, 2, 5).reshape(b, 2, N_KV, nc, CMP_STRIDE * HEAD_DIM)
    half = CMP_BLOCK // 2
    w1cat = jnp.concatenate([cmp_w1[:, :half].reshape(2, half * HEAD_DIM, -1), cmp_w1[:, half:].reshape(2, half * HEAD_DIM, -1)], axis=2)
    pe_rows = jnp.pad(cmp_pe.reshape(2, 2, half * HEAD_DIM), ((0, 0), (0, 6), (0, 0)))
    cmp_gain = jnp.stack([qk_gain[1], jnp.ones((HEAD_DIM,), F32)]).reshape(2, 1, HEAD_DIM)
    cmp_kv = _compress(chunks, w1cat.astype(BF16), pe_rows.astype(BF16), cmp_w2.astype(BF16), cmp_gain)

    slopes = _alibi_slopes() * LOG2E
    n_slc = t // SLC_BLOCK
    msel = jnp.asarray(_cmp_to_slc(nc, n_slc), BF16)
    q_blocks = q_dim // HEAD_DIM
    o_c, selbias = _cmp_attention(slopes, proj, cmp_kv, gates_r, msel, b=b, t=t, top_k=min(SLC_TOPK, n_slc))
    o_s = _slc_attention(slopes, proj, selbias, gates_r, b=b, t=t, k_col=q_blocks + 2 * N_KV, v_col=q_blocks + 3 * N_KV)
    o_w = _win_attention(slopes, proj, gates_r, b=b, t=t, k_col=q_blocks + 4 * N_KV, v_col=q_blocks + 5 * N_KV)
    return _proj_residual([o_c, o_s, o_w], w_out.astype(BF16), x, tm=512, tn=512)


def _sb_mixer(x, g_norm, w_in, w_out, *, b, t):
    q_dim = N_HEADS * HEAD_DIM
    scale = HEAD_DIM ** -0.5 * LOG2E
    cgain = jnp.concatenate([jnp.full((q_dim,), scale, F32), jnp.ones((2 * q_dim,), F32)])
    proj = _norm_matmul(x, g_norm, w_in.astype(BF16), cgain, jnp.zeros((3 * q_dim,), F32), tm=512, tn=512, out_dtype=BF16)
    o = _sb_attention(proj, b=b, t=t)
    return _proj_residual([o], w_out.astype(BF16), x, tm=512, tn=512)


def kernel(x, norm_mix, norm_ffn, nsa_w_in, nsa_qk_gain, nsa_cmp_pe, nsa_cmp_w1, nsa_cmp_w2, nsa_w_out, sb_w_in, sb_w_out, moe_w_group, moe_b_group, moe_w_router, moe_b_router, moe_w_gate, moe_w_up, moe_w_down):
    b, t, d = x.shape
    depth = norm_mix.shape[0]
    xf = x.reshape(b * t, d)
    for i in range(depth):
        j = i // 2
        if i % 2 == 0:
            xf = _nsa_mixer(xf, norm_mix[i], nsa_w_in[j], nsa_qk_gain[j], nsa_cmp_pe[j], nsa_cmp_w1[j], nsa_cmp_w2[j], nsa_w_out[j], b=b, t=t)
        else:
            xf = _sb_mixer(xf, norm_mix[i], sb_w_in[j], sb_w_out[j], b=b, t=t)
        xf = _hier_moe(xf, norm_ffn[i], moe_w_group[i], moe_b_group[i], moe_w_router[i], moe_b_router[i], moe_w_gate[i], moe_w_up[i], moe_w_down[i])
    return xf.reshape(b, t, d)
```

```python
import functools

import jax
import jax.numpy as jnp
import numpy as np
from jax import lax
from jax.experimental import pallas as pl
from jax.experimental.pallas import tpu as pltpu

F32 = jnp.float32
BF16 = jnp.bfloat16

HEAD_DIM = 128
N_HEADS = 16
N_KV = 4
GQA_REP = N_HEADS // N_KV
CMP_BLOCK = 32
CMP_STRIDE = 16
SLC_BLOCK = 64
SLC_TOPK = 16
WINDOW = 512
Q_BLOCK = 128
N_GROUPS = 4
EXPERTS_PER_GROUP = 8
N_EXPERTS = N_GROUPS * EXPERTS_PER_GROUP
EPS = 1e-6
NEG = -1e30
LANES = 128
VMEM_LIMIT = 56 * 1024 * 1024

PROJ_ROWS = 512
PROJ_COLS = 512
CMP_Q_BLOCK = 256
SLC_LANES = 128
SB_TILE = 256
SB_HEADS_PER_STEP = 4
SLC_KEY_TILE = 1024
LOG2E = 1.4426950408889634
SB_UNDERFLOW = -105.0
MOE_TILE = 256
AUG_PAD_LANE = 6


def _dot(a, b):
    return jnp.dot(a, b, preferred_element_type=F32)


def _dot_nt(a, b):
    return lax.dot_general(a, b, (((1,), (1,)), ((), ())), preferred_element_type=F32)


def _params(sem):
    return pltpu.CompilerParams(dimension_semantics=sem, vmem_limit_bytes=VMEM_LIMIT)


def _resident(shape):
    return pl.BlockSpec(shape, lambda *_: (0,) * len(shape), pipeline_mode=pl.Buffered(1))


def _norm_matmul_kernel(x_ref, g_ref, w_ref, cg_ref, *rest, norm_tiles, has_extra):
    if has_extra:
        wx_ref, o_ref, ox_ref = rest
    else:
        (o_ref,) = rest
    x = x_ref[...]
    ms = jnp.mean(x * x, axis=-1, keepdims=True)
    h = (x * lax.rsqrt(ms + EPS) * g_ref[...]).astype(BF16)
    tn = PROJ_COLS
    for j in range(w_ref.shape[1] // tn):
        y = _dot(h, w_ref[:, j * tn:(j + 1) * tn])
        for hh in range(tn // LANES):
            yh = y[:, hh * LANES:(hh + 1) * LANES]
            if j in norm_tiles:
                yh = yh * lax.rsqrt(jnp.mean(yh * yh, axis=-1, keepdims=True) + EPS)
            sl = slice(j * tn + hh * LANES, j * tn + (hh + 1) * LANES)
            o_ref[:, sl] = (yh * cg_ref[:, sl]).astype(o_ref.dtype)
    if has_extra:
        ox_ref[...] = _dot(h, wx_ref[...])


def _norm_matmul(x, g, w, cgain, norm_tiles, extra_w=None):
    n, d = x.shape
    m = w.shape[1]
    tm = PROJ_ROWS
    has_extra = extra_w is not None
    in_specs = [pl.BlockSpec((tm, d), lambda i: (i, 0)), _resident((1, d)), _resident((d, m)), _resident((1, m))]
    out_specs = [pl.BlockSpec((tm, m), lambda i: (i, 0))]
    out_shape = [jax.ShapeDtypeStruct((n, m), BF16)]
    args = [x, g.reshape(1, d), w, cgain.reshape(1, m)]
    if has_extra:
        mx = extra_w.shape[1]
        in_specs.append(_resident((d, mx)))
        out_specs.append(pl.BlockSpec((tm, mx), lambda i: (i, 0)))
        out_shape.append(jax.ShapeDtypeStruct((n, mx), F32))
        args.append(extra_w)
    out = pl.pallas_call(
        functools.partial(_norm_matmul_kernel, norm_tiles=tuple(norm_tiles), has_extra=has_extra),
        grid=(n // tm,),
        in_specs=in_specs,
        out_specs=out_specs,
        out_shape=out_shape,
        compiler_params=_params(("parallel",)),
        name="norm_matmul",
    )(*args)
    return out if has_extra else out[0]


def _proj_residual_kernel(*refs, n_in):
    a_refs = refs[:n_in]
    w_ref, r_ref, o_ref = refs[n_in:]
    if n_in == 1:
        a = a_refs[0][...]
    else:
        acc = a_refs[0][...].astype(F32)
        for a_ref in a_refs[1:]:
            acc = acc + a_ref[...].astype(F32)
        a = acc.astype(BF16)
    tn = PROJ_COLS
    for j in range(w_ref.shape[1] // tn):
        sl = slice(j * tn, (j + 1) * tn)
        o_ref[:, sl] = r_ref[:, sl] + _dot(a, w_ref[:, sl])


def _proj_residual(branches, w, res):
    n, k = branches[0].shape
    m = w.shape[1]
    tm = PROJ_ROWS
    n_in = len(branches)
    return pl.pallas_call(
        functools.partial(_proj_residual_kernel, n_in=n_in),
        grid=(n // tm,),
        in_specs=[pl.BlockSpec((tm, k), lambda i: (i, 0)) for _ in range(n_in)]
        + [_resident((k, m)), pl.BlockSpec((tm, m), lambda i: (i, 0))],
        out_specs=pl.BlockSpec((tm, m), lambda i: (i, 0)),
        out_shape=jax.ShapeDtypeStruct((n, m), F32),
        compiler_params=_params(("parallel",)),
        name="proj_residual",
    )(*branches, w, res)


def _cmp_kernel(ch_ref, w1_ref, pe_ref, w2_ref, gain_ref, o_ref):
    nc = ch_ref.shape[4]
    p = jnp.zeros((nc, 2 * HEAD_DIM), F32)
    pb = jnp.zeros((8, 2 * HEAD_DIM), F32)
    for l in range(CMP_STRIDE):
        w1 = w1_ref[0, l]
        p = p + _dot(ch_ref[0, 0, 0, l], w1)
        pb = pb + _dot(pe_ref[0, l], w1)
    bias = pb[0:1, :HEAD_DIM] + pb[1:2, HEAD_DIM:]
    second = pltpu.roll(p[:, HEAD_DIM:], shift=nc - 1, axis=0)
    hid = jax.nn.gelu(p[:, :HEAD_DIM] + second + bias)
    c = _dot(hid.astype(BF16), w2_ref[0])
    nf = (pl.program_id(1) == 0).astype(F32)
    ms = jnp.mean(c * c, axis=-1, keepdims=True)
    fac = nf * lax.rsqrt(ms + EPS) + (1.0 - nf)
    o_ref[0, 0, 0] = (c * fac * gain_ref[0]).astype(o_ref.dtype)


def _compress(chunks, w1cat, pe_rows, w2, gains):
    b, _, g, cs, nc, hd = chunks.shape
    return pl.pallas_call(
        _cmp_kernel,
        grid=(b, 2, g),
        in_specs=[
            pl.BlockSpec((1, 1, 1, cs, nc, hd), lambda bi, k, gi: (bi, k, gi, 0, 0, 0)),
            pl.BlockSpec((1, cs, hd, 2 * HEAD_DIM), lambda bi, k, gi: (k, 0, 0, 0)),
            pl.BlockSpec((1, cs, 8, hd), lambda bi, k, gi: (k, 0, 0, 0)),
            pl.BlockSpec((1, HEAD_DIM, HEAD_DIM), lambda bi, k, gi: (k, 0, 0)),
            pl.BlockSpec((1, 1, HEAD_DIM), lambda bi, k, gi: (k, 0, 0)),
        ],
        out_specs=pl.BlockSpec((1, 1, 1, nc, HEAD_DIM), lambda bi, k, gi: (bi, k, gi, 0, 0)),
        out_shape=jax.ShapeDtypeStruct((b, 2, g, nc, HEAD_DIM), BF16),
        compiler_params=_params(("parallel", "parallel", "parallel")),
        name="nsa_compress",
    )(chunks, w1cat, pe_rows, w2, gains)


def _stack_heads(qa_ref, q_ref, srow_ref, qb):
    for r in range(GQA_REP):
        qa_ref[r * qb:(r + 1) * qb, :HEAD_DIM] = q_ref[:, r * HEAD_DIM:(r + 1) * HEAD_DIM]
        qa_ref[r * qb:(r + 1) * qb, HEAD_DIM:] = jnp.broadcast_to(srow_ref[0, r:r + 1, :], (qb, LANES))


def _cmp_attn_kernel(q_ref, kc_ref, vc_ref, gate_ref, msel_ref, srow_ref, oc_ref, sb_ref, qa_ref, ka_ref, va_ref, *, top_k):
    c = pl.program_id(2)
    qb = CMP_Q_BLOCK
    t0 = c * qb
    rows = GQA_REP * qb
    nc = ka_ref.shape[0]

    @pl.when(c == 0)
    def _():
        n_idx = lax.broadcasted_iota(jnp.int32, (nc, LANES), 0)
        lane = lax.broadcasted_iota(jnp.int32, (nc, LANES), 1)
        aug = jnp.where(lane < 3, ((n_idx >> 7) << 11).astype(F32), jnp.where(lane < 6, ((n_idx & 127) << 4).astype(F32), 0.0))
        ka_ref[:, :HEAD_DIM] = kc_ref[0, 0, 0]
        ka_ref[:, HEAD_DIM:] = aug.astype(BF16)
        va_ref[:, :HEAD_DIM] = vc_ref[0, 0, 0]
        va_ref[:, HEAD_DIM:] = (lane == 0).astype(BF16)

    _stack_heads(qa_ref, q_ref, srow_ref, qb)
    s = _dot_nt(qa_ref[...], ka_ref[...])
    t_col = t0 + (lax.broadcasted_iota(jnp.int32, (rows, 1), 0) & (qb - 1))
    cend = lax.broadcasted_iota(jnp.int32, (1, nc), 1) * CMP_STRIDE + (CMP_BLOCK - 1)
    s = jnp.where(cend <= t_col, s, NEG)
    m = jnp.max(s, axis=-1, keepdims=True)
    e = jnp.exp2(s - m)
    acc = _dot(e.astype(BF16), va_ref[...])
    inv = jnp.where(m > 0.5 * NEG, 1.0 / acc[:, HEAD_DIM:HEAD_DIM + 1], 0.0)
    gates = jax.nn.sigmoid(gate_ref[0])
    psum = jnp.zeros((qb, nc), F32)
    for r in range(GQA_REP):
        rs = slice(r * qb, (r + 1) * qb)
        psum = psum + e[rs] * inv[rs]
        oc_ref[:, r * HEAD_DIM:(r + 1) * HEAD_DIM] = (acc[rs, :HEAD_DIM] * (inv[rs] * gates[:, r:r + 1])).astype(oc_ref.dtype)

    msel = msel_ref[...]
    p_hi = psum.astype(BF16)
    rem = psum - p_hi.astype(F32)
    p_mid = rem.astype(BF16)
    p_lo = (rem - p_mid.astype(F32)).astype(BF16)
    imp = _dot(p_hi, msel) + _dot(p_mid, msel) + _dot(p_lo, msel)

    imp_t = imp.T
    j_col = lax.broadcasted_iota(jnp.int32, (SLC_LANES, 1), 0)
    t_row = t0 + lax.broadcasted_iota(jnp.int32, (1, qb), 1)
    forced = (j_col == 0) | (j_col == (t_row >> 6))
    causal = (j_col << 6) <= t_row
    score = jnp.where(forced, -NEG, jnp.where(causal, imp_t, NEG))
    keep = jnp.zeros((SLC_LANES, qb), F32)
    for _ in range(top_k):
        mx = jnp.max(score, axis=0, keepdims=True)
        first = jnp.min(jnp.where(score == mx, j_col, SLC_LANES), axis=0, keepdims=True)
        hit = j_col == first
        keep = jnp.where(hit, (mx > 0.5 * NEG).astype(F32), keep)
        score = jnp.where(hit, -3e38, score)
    bias_t = jnp.where(keep > 0.5, 0.0, NEG)
    sb_ref[0] = bias_t.T.astype(sb_ref.dtype)


def _cmp_attention(proj, cmp_kv, gates_r, msel, srow, *, b, t, top_k):
    n = b * t
    qb = CMP_Q_BLOCK
    nqb = t // qb
    rows = GQA_REP * qb
    nc = cmp_kv.shape[3]
    rowblk = lambda bi, gi, c: bi * nqb + c
    return pl.pallas_call(
        functools.partial(_cmp_attn_kernel, top_k=top_k),
        grid=(b, N_KV, nqb),
        in_specs=[
            pl.BlockSpec((qb, GQA_REP * HEAD_DIM), lambda bi, gi, c: (rowblk(bi, gi, c), gi)),
            pl.BlockSpec((1, 1, 1, nc, HEAD_DIM), lambda bi, gi, c: (bi, 0, gi, 0, 0)),
            pl.BlockSpec((1, 1, 1, nc, HEAD_DIM), lambda bi, gi, c: (bi, 1, gi, 0, 0)),
            pl.BlockSpec((1, qb, LANES), lambda bi, gi, c: (gi, rowblk(bi, gi, c), 0)),
            pl.BlockSpec((nc, SLC_LANES), lambda bi, gi, c: (0, 0)),
            pl.BlockSpec((1, 8, LANES), lambda bi, gi, c: (gi, 0, 0)),
        ],
        out_specs=[
            pl.BlockSpec((qb, GQA_REP * HEAD_DIM), lambda bi, gi, c: (rowblk(bi, gi, c), gi)),
            pl.BlockSpec((1, qb, SLC_LANES), lambda bi, gi, c: (gi, rowblk(bi, gi, c), 0)),
        ],
        out_shape=[
            jax.ShapeDtypeStruct((n, N_HEADS * HEAD_DIM), BF16),
            jax.ShapeDtypeStruct((N_KV, n, SLC_LANES), BF16),
        ],
        scratch_shapes=[
            pltpu.VMEM((rows, 2 * HEAD_DIM), BF16),
            pltpu.VMEM((nc, 2 * HEAD_DIM), BF16),
            pltpu.VMEM((nc, 2 * HEAD_DIM), BF16),
        ],
        compiler_params=_params(("parallel", "parallel", "arbitrary")),
        name="nsa_cmp_attn",
    )(proj, cmp_kv, cmp_kv, gates_r, msel, srow)


def _win_attn_kernel(q_ref, k_ref, v_ref, gate_ref, srow_ref, o_ref, qa_ref, ka_ref, va_ref):
    c = pl.program_id(2)
    t0 = pl.multiple_of(c * Q_BLOCK, Q_BLOCK)
    rows = GQA_REP * Q_BLOCK
    span = Q_BLOCK + WINDOW
    t_len = k_ref.shape[0]

    @pl.when(c == 0)
    def _():
        pos = lax.broadcasted_iota(jnp.int32, (t_len, LANES), 0)
        lane = lax.broadcasted_iota(jnp.int32, (t_len, LANES), 1)
        aug = jnp.where(lane < 3, ((pos >> 6) << 6).astype(F32), jnp.where(lane < 6, (pos & 63).astype(F32), 0.0))
        pad_lane = lax.broadcasted_iota(jnp.int32, (WINDOW, LANES), 1)
        ka_ref[:WINDOW, :HEAD_DIM] = jnp.zeros((WINDOW, HEAD_DIM), BF16)
        ka_ref[:WINDOW, HEAD_DIM:] = (pad_lane == AUG_PAD_LANE).astype(BF16)
        ka_ref[WINDOW:, :HEAD_DIM] = k_ref[...]
        ka_ref[WINDOW:, HEAD_DIM:] = aug.astype(BF16)
        va_ref[:WINDOW, :] = jnp.zeros((WINDOW, 2 * HEAD_DIM), BF16)
        va_ref[WINDOW:, :HEAD_DIM] = v_ref[...]
        va_ref[WINDOW:, HEAD_DIM:] = (lane == 0).astype(BF16)

    _stack_heads(qa_ref, q_ref, srow_ref, Q_BLOCK)
    s = _dot_nt(qa_ref[...], ka_ref[pl.ds(t0, span), :])
    i_col = lax.broadcasted_iota(jnp.int32, (rows, 1), 0) & (Q_BLOCK - 1)
    kk = lax.broadcasted_iota(jnp.int32, (1, Q_BLOCK), 1)
    first = jnp.where(kk > i_col, s[:, :Q_BLOCK], NEG)
    mid = s[:, Q_BLOCK:WINDOW]
    last = jnp.where(kk <= i_col, s[:, WINDOW:], NEG)
    m = jnp.maximum(jnp.maximum(jnp.max(first, axis=-1, keepdims=True), jnp.max(mid, axis=-1, keepdims=True)),
                    jnp.max(last, axis=-1, keepdims=True))
    acc = (_dot(jnp.exp2(first - m).astype(BF16), va_ref[pl.ds(t0, Q_BLOCK), :])
           + _dot(jnp.exp2(mid - m).astype(BF16), va_ref[pl.ds(t0 + Q_BLOCK, WINDOW - Q_BLOCK), :])
           + _dot(jnp.exp2(last - m).astype(BF16), va_ref[pl.ds(t0 + WINDOW, Q_BLOCK), :]))
    gates = jax.nn.sigmoid(gate_ref[0])
    o = acc[:, :HEAD_DIM] / acc[:, HEAD_DIM:HEAD_DIM + 1]
    for r in range(GQA_REP):
        gate = gates[:, 2 * GQA_REP + r:2 * GQA_REP + r + 1]
        o_ref[:, r * HEAD_DIM:(r + 1) * HEAD_DIM] = (o[r * Q_BLOCK:(r + 1) * Q_BLOCK] * gate).astype(o_ref.dtype)


def _win_attention(proj, gates_r, srow, *, b, t, k_col, v_col):
    n = b * t
    nqb = t // Q_BLOCK
    rows = GQA_REP * Q_BLOCK
    rowblk = lambda bi, gi, c: bi * nqb + c
    return pl.pallas_call(
        _win_attn_kernel,
        grid=(b, N_KV, nqb),
        in_specs=[
            pl.BlockSpec((Q_BLOCK, GQA_REP * HEAD_DIM), lambda bi, gi, c: (rowblk(bi, gi, c), gi)),
            pl.BlockSpec((t, HEAD_DIM), lambda bi, gi, c: (bi, k_col + gi)),
            pl.BlockSpec((t, HEAD_DIM), lambda bi, gi, c: (bi, v_col + gi)),
            pl.BlockSpec((1, Q_BLOCK, LANES), lambda bi, gi, c: (gi, rowblk(bi, gi, c), 0)),
            pl.BlockSpec((1, 8, LANES), lambda bi, gi, c: (gi, 0, 0)),
        ],
        out_specs=pl.BlockSpec((Q_BLOCK, GQA_REP * HEAD_DIM), lambda bi, gi, c: (rowblk(bi, gi, c), gi)),
        out_shape=jax.ShapeDtypeStruct((n, N_HEADS * HEAD_DIM), BF16),
        scratch_shapes=[
            pltpu.VMEM((rows, 2 * HEAD_DIM), BF16),
            pltpu.VMEM((t + WINDOW, 2 * HEAD_DIM), BF16),
            pltpu.VMEM((t + WINDOW, 2 * HEAD_DIM), BF16),
        ],
        compiler_params=_params(("parallel", "parallel", "arbitrary")),
        name="nsa_win_attn",
    )(proj, proj, proj, gates_r, srow)


def _slc_attn_kernel(q_ref, k_ref, v_ref, sb_ref, gate_ref, srow_ref, scol_ref, o_ref, qa_ref, ka_ref, va_ref, m_ref, acc_ref, s_ref):
    c = pl.program_id(2)
    t0 = c * Q_BLOCK
    rows = GQA_REP * Q_BLOCK
    tk = SLC_KEY_TILE
    nb = tk // SLC_BLOCK
    t_len = k_ref.shape[0]

    @pl.when(c == 0)
    def _():
        pos = lax.broadcasted_iota(jnp.int32, (t_len, LANES), 0)
        lane = lax.broadcasted_iota(jnp.int32, (t_len, LANES), 1)
        off = pos & (tk - 1)
        blk = (pos >> 6) & (nb - 1)
        aug = jnp.where(lane < nb, (blk == lane).astype(F32),
                        jnp.where(lane < nb + 3, ((off >> 3) << 3).astype(F32),
                                  jnp.where(lane < nb + 6, (off & 7).astype(F32), 0.0)))
        ka_ref[:, :HEAD_DIM] = k_ref[...]
        ka_ref[:, HEAD_DIM:] = aug.astype(BF16)
        va_ref[:, :HEAD_DIM] = v_ref[...]
        va_ref[:, HEAD_DIM:] = (lane == 0).astype(BF16)

    for r in range(GQA_REP):
        qa_ref[r * Q_BLOCK:(r + 1) * Q_BLOCK, :HEAD_DIM] = q_ref[:, r * HEAD_DIM:(r + 1) * HEAD_DIM]
    m_ref[...] = jnp.full(m_ref.shape, NEG, F32)
    acc_ref[...] = jnp.zeros(acc_ref.shape, F32)
    sb = sb_ref[0]
    scol = scol_ref[0]
    t_col = t0 + (lax.broadcasted_iota(jnp.int32, (rows, 1), 0) & (Q_BLOCK - 1))
    k_iota = lax.broadcasted_iota(jnp.int32, (1, tk), 1)
    p_row = lax.broadcasted_iota(jnp.int32, (SLC_LANES, LANES), 0)
    p_col = lax.broadcasted_iota(jnp.int32, (SLC_LANES, LANES), 1)

    def scores(kt):
        k0 = pl.multiple_of(kt * tk, tk)
        pick = ((p_row == kt * nb + p_col) & (p_col < nb)).astype(BF16)
        sbt = _dot(sb, pick).astype(BF16)
        for r in range(GQA_REP):
            qa_ref[r * Q_BLOCK:(r + 1) * Q_BLOCK, HEAD_DIM:] = sbt + srow_ref[0, r:r + 1, :]
        return _dot_nt(qa_ref[...], ka_ref[pl.ds(k0, tk), :])

    def softmax_pv(kt, s, causal):
        k0 = pl.multiple_of(kt * tk, tk)
        if causal:
            s = jnp.where(k0 + k_iota <= t_col, s, NEG)
        shift = scol * (k0 - t0).astype(F32)
        m_old = m_ref[...]
        m_new = jnp.maximum(m_old, jnp.max(s, axis=-1, keepdims=True) + shift)
        alpha = jnp.exp2(m_old - m_new)
        p = jnp.exp2(s - (m_new - shift))
        acc_ref[...] = alpha * acc_ref[...] + _dot(p.astype(BF16), va_ref[pl.ds(k0, tk), :])
        m_ref[...] = m_new

    n_below = lax.div(t0, tk)
    s_ref[...] = scores(0)

    def body(kt, carry):
        s_cur = s_ref[...]
        s_next = scores(kt + 1)
        softmax_pv(kt, s_cur, False)
        s_ref[...] = s_next
        return carry

    lax.fori_loop(0, n_below, body, 0)
    softmax_pv(n_below, s_ref[...], True)
    gates = jax.nn.sigmoid(gate_ref[0])
    acc = acc_ref[...]
    o = acc[:, :HEAD_DIM] / acc[:, HEAD_DIM:HEAD_DIM + 1]
    for r in range(GQA_REP):
        gate = gates[:, GQA_REP + r:GQA_REP + r + 1]
        o_ref[:, r * HEAD_DIM:(r + 1) * HEAD_DIM] = (o[r * Q_BLOCK:(r + 1) * Q_BLOCK] * gate).astype(o_ref.dtype)


def _slope_pieces(slopes):
    s_hi = slopes.astype(BF16)
    s_mid = (slopes - s_hi.astype(F32)).astype(BF16)
    s_lo = (slopes - s_hi.astype(F32) - s_mid.astype(F32)).astype(BF16)
    return jnp.stack([s_hi, s_mid, s_lo, s_hi, s_mid, s_lo], axis=1)


def _slope_rows(pieces, first_lane):
    rows = jnp.pad(pieces, ((0, 0), (first_lane, LANES - first_lane - pieces.shape[1]))).reshape(N_KV, GQA_REP, LANES)
    return jnp.pad(rows, ((0, 0), (0, 8 - GQA_REP), (0, 0)))


def _slc_attention(slopes, proj, selbias, gates_r, *, b, t, k_col, v_col):
    n = b * t
    nqb = t // Q_BLOCK
    rows = GQA_REP * Q_BLOCK
    nb = SLC_KEY_TILE // SLC_BLOCK
    rowblk = lambda bi, gi, c: bi * nqb + c
    srow = _slope_rows(_slope_pieces(slopes), nb)
    scol = jnp.repeat(slopes.reshape(N_KV, GQA_REP), Q_BLOCK, axis=1).reshape(N_KV, rows, 1)
    return pl.pallas_call(
        _slc_attn_kernel,
        grid=(b, N_KV, nqb),
        in_specs=[
            pl.BlockSpec((Q_BLOCK, GQA_REP * HEAD_DIM), lambda bi, gi, c: (rowblk(bi, gi, c), gi)),
            pl.BlockSpec((t, HEAD_DIM), lambda bi, gi, c: (bi, k_col + gi)),
            pl.BlockSpec((t, HEAD_DIM), lambda bi, gi, c: (bi, v_col + gi)),
            pl.BlockSpec((1, Q_BLOCK, SLC_LANES), lambda bi, gi, c: (gi, rowblk(bi, gi, c), 0)),
            pl.BlockSpec((1, Q_BLOCK, LANES), lambda bi, gi, c: (gi, rowblk(bi, gi, c), 0)),
            pl.BlockSpec((1, 8, LANES), lambda bi, gi, c: (gi, 0, 0)),
            pl.BlockSpec((1, rows, 1), lambda bi, gi, c: (gi, 0, 0)),
        ],
        out_specs=pl.BlockSpec((Q_BLOCK, GQA_REP * HEAD_DIM), lambda bi, gi, c: (rowblk(bi, gi, c), gi)),
        out_shape=jax.ShapeDtypeStruct((n, N_HEADS * HEAD_DIM), BF16),
        scratch_shapes=[
            pltpu.VMEM((rows, 2 * HEAD_DIM), BF16),
            pltpu.VMEM((t, 2 * HEAD_DIM), BF16),
            pltpu.VMEM((t, 2 * HEAD_DIM), BF16),
            pltpu.VMEM((rows, 1), F32),
            pltpu.VMEM((rows, 2 * HEAD_DIM), F32),
            pltpu.VMEM((rows, SLC_KEY_TILE), F32),
        ],
        compiler_params=_params(("parallel", "parallel", "arbitrary")),
        name="nsa_slc_attn",
    )(proj, proj, proj, selbias, gates_r, srow, scol)


def _sb_attn_kernel(q_ref, k_ref, v_ref, o_ref, acc_ref, carry_ref):
    ts = SB_TILE
    c = pl.program_id(2)
    acc_ref[...] = jnp.zeros(acc_ref.shape, F32)
    carry_ref[...] = jnp.zeros(carry_ref.shape, F32)
    r_idx = lax.broadcasted_iota(jnp.int32, (ts, 1), 0)
    c_idx = lax.broadcasted_iota(jnp.int32, (1, ts), 1)
    tri = (r_idx > c_idx).astype(BF16)
    diag = c_idx < r_idx

    def tile(jt, valid):
        k0 = pl.multiple_of(jt * ts, ts)
        heads = range(SB_HEADS_PER_STEP)
        cols = [slice(e * HEAD_DIM, (e + 1) * HEAD_DIM) for e in heads]
        carries = [carry_ref[e] for e in heads]
        zs = [_dot_nt(q_ref[:, cols[e]], k_ref[pl.ds(k0, ts), cols[e]]) for e in heads]
        lfs = []
        for e in heads:
            lf = -(jnp.maximum(zs[e], 0.0) + jnp.log2(1.0 + jnp.exp2(-jnp.abs(zs[e]))))
            lfs.append(lf if valid is None else jnp.where(valid, lf, 0.0))
        afters = []
        for e in heads:
            hi = lfs[e].astype(BF16)
            lo = (lfs[e] - hi.astype(F32)).astype(BF16)
            afters.append(_dot(hi, tri) + _dot(lo, tri))
        avs = []
        for e in heads:
            a = jnp.exp2(zs[e] + lfs[e] + afters[e] + carries[e])
            if valid is not None:
                a = jnp.where(valid, a, 0.0)
            avs.append(_dot(a.astype(BF16), v_ref[pl.ds(k0, ts), cols[e]]))
        for e in heads:
            acc_ref[e] += avs[e]
            carry_ref[e] = carries[e] + jnp.sum(lfs[e], axis=-1, keepdims=True)

    def live():
        return (jnp.max(carry_ref[...]) > SB_UNDERFLOW * LOG2E).astype(jnp.int32)

    tile(c, diag)

    def cond(state):
        jt, go = state
        return (jt >= 0) & (go > 0)

    def body(state):
        jt, _ = state
        tile(jt, None)
        return jt - 1, live()

    lax.while_loop(cond, body, (c - 1, live()))
    for e in range(SB_HEADS_PER_STEP):
        o_ref[:, e * HEAD_DIM:(e + 1) * HEAD_DIM] = acc_ref[e].astype(o_ref.dtype)


def _sb_attention(proj, *, b, t):
    n = b * t
    ts = SB_TILE
    nq = t // ts
    hs = SB_HEADS_PER_STEP
    width = hs * HEAD_DIM
    nhp = N_HEADS // hs
    return pl.pallas_call(
        _sb_attn_kernel,
        grid=(b, nhp, nq),
        in_specs=[
            pl.BlockSpec((ts, width), lambda bi, h, c: (bi * nq + c, h)),
            pl.BlockSpec((t, width), lambda bi, h, c: (bi, nhp + h)),
            pl.BlockSpec((t, width), lambda bi, h, c: (bi, 2 * nhp + h)),
        ],
        out_specs=pl.BlockSpec((ts, width), lambda bi, h, c: (bi * nq + c, h)),
        out_shape=jax.ShapeDtypeStruct((n, N_HEADS * HEAD_DIM), BF16),
        scratch_shapes=[pltpu.VMEM((hs, ts, HEAD_DIM), F32), pltpu.VMEM((hs, ts, 1), F32)],
        compiler_params=_params(("parallel", "parallel", "arbitrary")),
        name="sb_attn",
    )(proj, proj, proj)


def _first_max(vals, lane):
    m = jnp.max(vals, axis=-1, keepdims=True)
    idx = jnp.min(jnp.where(vals == m, lane, LANES), axis=-1, keepdims=True)
    return m, idx


def _router_kernel(x_ref, g_ref, w_ref, b_ref, o_ref):
    x = x_ref[...]
    ms = jnp.mean(x * x, axis=-1, keepdims=True)
    h = x * lax.rsqrt(ms + EPS) * g_ref[...]
    logits = jnp.dot(h, w_ref[...], preferred_element_type=F32, precision=lax.Precision.HIGHEST) + b_ref[...]
    lane = lax.broadcasted_iota(jnp.int32, (1, LANES), 1)
    gl = jnp.where(lane < N_GROUPS, logits, NEG)
    ge = jnp.exp(gl - jnp.max(gl, axis=-1, keepdims=True))
    pg_all = ge / jnp.sum(ge, axis=-1, keepdims=True)
    pg, gsel = _first_max(jnp.where(lane < N_GROUPS, pg_all, -1.0), lane)
    lo = N_GROUPS + gsel * EXPERTS_PER_GROUP
    in_group = (lane >= lo) & (lane < lo + EXPERTS_PER_GROUP)
    el = jnp.where(in_group, logits, NEG)
    ee = jnp.exp(el - jnp.max(el, axis=-1, keepdims=True))
    pe_all = jnp.where(in_group, ee / jnp.sum(ee, axis=-1, keepdims=True), -1.0)
    p1, i1 = _first_max(pe_all, lane)
    p2, i2 = _first_max(jnp.where(lane == i1, -1.0, pe_all), lane)
    denom = p1 + p2
    w1 = pg * p1 / denom
    w2 = pg * p2 / denom
    e1 = (i1 - N_GROUPS).astype(F32)
    e2 = (i2 - N_GROUPS).astype(F32)
    o_ref[...] = jnp.where(lane == 0, e1, jnp.where(lane == 1, e2, jnp.where(lane == 2, w1, jnp.where(lane == 3, w2, 0.0))))


def _router(x, g, w, bias, *, tm):
    n, d = x.shape
    return pl.pallas_call(
        _router_kernel,
        grid=(n // tm,),
        in_specs=[
            pl.BlockSpec((tm, d), lambda i: (i, 0)),
            pl.BlockSpec((1, d), lambda i: (0, 0)),
            pl.BlockSpec((d, LANES), lambda i: (0, 0)),
            pl.BlockSpec((1, LANES), lambda i: (0, 0)),
        ],
        out_specs=pl.BlockSpec((tm, LANES), lambda i: (i, 0)),
        out_shape=jax.ShapeDtypeStruct((n, LANES), F32),
        compiler_params=_params(("parallel",)),
        name="moe_router",
    )(x, g.reshape(1, d), w, bias)


def _row_copy(src_hbm, dst_ref, sem, src_row, dst_row):
    return pltpu.make_async_copy(src_hbm.at[pl.ds(src_row, 1), :], dst_ref.at[pl.ds(dst_row, 1), :], sem)


def _row_gather_start(src_hbm, dst_ref, sem, idx_ref, base, count):
    def body(r, carry):
        _row_copy(src_hbm, dst_ref, sem, idx_ref[base + r], r).start()
        return carry

    lax.fori_loop(0, count, body, 0, unroll=8)


def _row_gather_wait(src_hbm, dst_ref, sem):
    pltpu.make_async_copy(src_hbm.at[pl.ds(0, dst_ref.shape[0]), :], dst_ref, sem).wait()


def _expert_kernel(src_ref, te_ref, nu_ref, x_hbm, g_ref, wg_ref, wu_ref, wd_ref, o_ref, xbuf, wgb, wub, wdb, sem):
    tm = MOE_TILE
    i = pl.program_id(0)
    nt = pl.num_programs(0)
    n_used = nu_ref[0]
    slot = i % 2
    nxt = 1 - slot
    next_base = jnp.where(i + 1 < nt, i + 1, 0) * tm

    @pl.when((i < n_used) & ((i == 0) | (te_ref[i] != te_ref[jnp.maximum(i - 1, 0)])))
    def _():
        wgb[...] = wg_ref[0].astype(BF16)
        wub[...] = wu_ref[0].astype(BF16)
        wdb[...] = wd_ref[0].astype(BF16)

    @pl.when(i == 0)
    def _():
        _row_gather_start(x_hbm, xbuf.at[0], sem.at[0], src_ref, 0, tm)

    _row_gather_wait(x_hbm, xbuf.at[slot], sem.at[slot])

    @pl.when(i < n_used)
    def _():
        x = xbuf[slot]
        ms = jnp.mean(x * x, axis=-1, keepdims=True)
        h = (x * lax.rsqrt(ms + EPS) * g_ref[...]).astype(BF16)
        for r in range(tm):
            _row_copy(x_hbm, xbuf.at[nxt], sem.at[nxt], src_ref[next_base + r], r).start()
        gate = _dot(h, wgb[...])
        up = _dot(h, wub[...])
        act = (gate * jax.nn.sigmoid(gate) * up).astype(BF16)
        o_ref[...] = _dot(act, wdb[...])

    @pl.when(i >= n_used)
    def _():
        _row_gather_start(x_hbm, xbuf.at[nxt], sem.at[nxt], src_ref, next_base, tm)
        o_ref[...] = jnp.zeros(o_ref.shape, o_ref.dtype)

    @pl.when(i == nt - 1)
    def _():
        _row_gather_wait(x_hbm, xbuf.at[nxt], sem.at[nxt])


def _experts(src_tok, tile_expert, n_used, x, g, wg, wu, wd):
    n, d = x.shape
    tm = MOE_TILE
    p_rows = src_tok.shape[0]
    de = wg.shape[2]
    grid_spec = pltpu.PrefetchScalarGridSpec(
        num_scalar_prefetch=3,
        grid=(p_rows // tm,),
        in_specs=[
            pl.BlockSpec(memory_space=pl.ANY),
            pl.BlockSpec((1, d), lambda i, s, te, nu: (0, 0)),
            pl.BlockSpec((1, d, de), lambda i, s, te, nu: (te[i], 0, 0)),
            pl.BlockSpec((1, d, de), lambda i, s, te, nu: (te[i], 0, 0)),
            pl.BlockSpec((1, de, d), lambda i, s, te, nu: (te[i], 0, 0)),
        ],
        out_specs=pl.BlockSpec((tm, d), lambda i, s, te, nu: (i, 0)),
        scratch_shapes=[
            pltpu.VMEM((2, tm, d), F32),
            pltpu.VMEM((d, de), BF16),
            pltpu.VMEM((d, de), BF16),
            pltpu.VMEM((de, d), BF16),
            pltpu.SemaphoreType.DMA((2,)),
        ],
    )
    return pl.pallas_call(
        _expert_kernel,
        grid_spec=grid_spec,
        out_shape=jax.ShapeDtypeStruct((p_rows, d), F32),
        compiler_params=_params(("arbitrary",)),
        name="moe_experts",
    )(src_tok, tile_expert, n_used, x, g.reshape(1, d), wg, wu, wd)


def _combine_kernel(pos_ref, x_ref, rt_ref, y_hbm, o_ref, ybuf, sem):
    tm = x_ref.shape[0]
    i = pl.program_id(0)
    nt = pl.num_programs(0)

    def start(tile, slot):
        for s in range(2):
            def body(r, carry):
                _row_copy(y_hbm, ybuf.at[slot, s], sem.at[slot], pos_ref[2 * (tile * tm + r) + s], r).start()
                return carry

            lax.fori_loop(0, tm, body, 0, unroll=8)

    @pl.when(i == 0)
    def _():
        start(0, 0)

    @pl.when(i + 1 < nt)
    def _():
        start(i + 1, (i + 1) % 2)

    slot = i % 2
    for s in range(2):
        _row_gather_wait(y_hbm, ybuf.at[slot, s], sem.at[slot])
    rt = rt_ref[...]
    o_ref[...] = x_ref[...] + rt[:, 2:3] * ybuf[slot, 0] + rt[:, 3:4] * ybuf[slot, 1]


def _combine(pos, x, routing, ys, *, tm):
    n, d = x.shape
    grid_spec = pltpu.PrefetchScalarGridSpec(
        num_scalar_prefetch=1,
        grid=(n // tm,),
        in_specs=[
            pl.BlockSpec((tm, d), lambda i, p: (i, 0)),
            pl.BlockSpec((tm, LANES), lambda i, p: (i, 0)),
            pl.BlockSpec(memory_space=pl.ANY),
        ],
        out_specs=pl.BlockSpec((tm, d), lambda i, p: (i, 0)),
        scratch_shapes=[pltpu.VMEM((2, 2, tm, d), F32), pltpu.SemaphoreType.DMA((2,))],
    )
    return pl.pallas_call(
        _combine_kernel,
        grid_spec=grid_spec,
        out_shape=jax.ShapeDtypeStruct((n, d), F32),
        compiler_params=_params(("arbitrary",)),
        name="moe_combine",
    )(pos, x, routing, ys)


def _hier_moe(x, g, w_group, b_group, w_router, b_router, w_gate, w_up, w_down):
    n, d = x.shape
    tm = MOE_TILE
    n_logit = N_GROUPS + N_EXPERTS
    w_r = jnp.pad(jnp.concatenate([w_group, w_router], axis=1), ((0, 0), (0, LANES - n_logit)))
    b_r = jnp.pad(jnp.concatenate([b_group, b_router]), (0, LANES - n_logit)).reshape(1, LANES)
    routing = _router(x, g, w_r, b_r, tm=512 if n % 512 == 0 else n)

    e_flat = routing[:, :2].astype(jnp.int32).reshape(-1)
    onehot = (e_flat[:, None] == jnp.arange(N_EXPERTS, dtype=jnp.int32)[None, :]).astype(jnp.int32)
    counts = jnp.sum(onehot, axis=0)
    rank = jnp.sum((jnp.cumsum(onehot, axis=0) - onehot) * onehot, axis=1)
    padded = ((counts + tm - 1) // tm) * tm
    ends = jnp.cumsum(padded)
    pos = (ends - padded)[e_flat] + rank
    p_rows = 2 * n + N_EXPERTS * tm
    src_tok = jnp.zeros((p_rows,), jnp.int32).at[pos].set(jnp.arange(2 * n, dtype=jnp.int32) // 2)
    tile_start = jnp.arange(p_rows // tm, dtype=jnp.int32) * tm
    tile_expert = jnp.minimum(jnp.sum((ends[None, :] <= tile_start[:, None]).astype(jnp.int32), axis=1), N_EXPERTS - 1)
    n_used = (ends[-1] // tm).astype(jnp.int32).reshape(1)

    ys = _experts(src_tok, tile_expert, n_used, x, g, w_gate, w_up, w_down)
    return _combine(pos.astype(jnp.int32), x, routing, ys, tm=256 if n % 256 == 0 else n)


def _alibi_slopes():
    return jnp.exp2(-8.0 * jnp.arange(1, N_HEADS + 1, dtype=F32) / N_HEADS)


def _cmp_to_slc(n_cmp_pad, n_slc):
    cs = np.arange(n_cmp_pad) * CMP_STRIDE
    ss = np.arange(SLC_LANES) * SLC_BLOCK
    lo = np.maximum(cs[:, None], ss[None, :])
    hi = np.minimum(cs[:, None] + CMP_BLOCK, ss[None, :] + SLC_BLOCK)
    m = np.maximum(hi - lo, 0).astype(np.float32) / CMP_BLOCK
    m[:, n_slc:] = 0.0
    return m


def _nsa_mixer(x, g_norm, w_in, qk_gain, cmp_pe, cmp_w1, cmp_w2, w_out, *, b, t):
    n, d = x.shape
    q_dim = N_HEADS * HEAD_DIM
    kv_dim = N_KV * HEAD_DIM
    scale = HEAD_DIM ** -0.5 * LOG2E
    main = q_dim + 6 * kv_dim
    ones = jnp.ones((kv_dim,), F32)
    tile4 = lambda v: jnp.tile(v, N_KV)
    cgain = jnp.concatenate([jnp.tile(qk_gain[0], N_HEADS) * scale, ones, ones, tile4(qk_gain[2]), ones, tile4(qk_gain[3]), ones])
    tiles_per = lambda cols: cols // PROJ_COLS
    q_tiles = tiles_per(q_dim)
    kv_tiles = tiles_per(kv_dim)
    norm_tiles = list(range(q_tiles)) + [q_tiles + 2 * kv_tiles + j for j in range(kv_tiles)] + [q_tiles + 4 * kv_tiles + j for j in range(kv_tiles)]
    n_gate = 3 * N_HEADS
    w_gate = jnp.pad(w_in[:, main:], ((0, 0), (0, LANES - n_gate))).astype(BF16)
    proj, gate_logits = _norm_matmul(x, g_norm, w_in[:, :main].astype(BF16), cgain, norm_tiles, extra_w=w_gate)
    gr = gate_logits[:, :n_gate].reshape(n, 3, N_KV, GQA_REP).transpose(2, 0, 1, 3).reshape(N_KV, n, 3 * GQA_REP)
    gates_r = jnp.pad(gr, ((0, 0), (0, 0), (0, LANES - 3 * GQA_REP)))

    nc = t // CMP_STRIDE
    half = CMP_BLOCK // 2
    kvc = proj[:, q_dim:q_dim + 2 * kv_dim].reshape(b, nc, CMP_STRIDE, 2, N_KV, HEAD_DIM)
    chunks = kvc.transpose(0, 3, 4, 2, 1, 5)
    w1cat = jnp.concatenate([cmp_w1[:, :half], cmp_w1[:, half:]], axis=-1)
    pe_rows = jnp.pad(jnp.stack([cmp_pe[:, :half], cmp_pe[:, half:]], axis=2), ((0, 0), (0, 0), (0, 6), (0, 0)))
    cmp_gain = jnp.stack([qk_gain[1], jnp.ones((HEAD_DIM,), F32)]).reshape(2, 1, HEAD_DIM)
    cmp_kv = _compress(chunks, w1cat.astype(BF16), pe_rows.astype(BF16), cmp_w2.astype(BF16), cmp_gain)

    slopes = _alibi_slopes() * LOG2E
    pieces = _slope_pieces(slopes)
    pad_flag = jnp.full((N_HEADS, 1), NEG, F32).astype(BF16)
    srow = _slope_rows(jnp.concatenate([pieces, pad_flag], axis=1), 0)
    n_slc = t // SLC_BLOCK
    msel = jnp.asarray(_cmp_to_slc(nc, n_slc), BF16)
    q_blocks = q_dim // HEAD_DIM
    o_c, selbias = _cmp_attention(proj, cmp_kv, gates_r, msel, srow, b=b, t=t, top_k=min(SLC_TOPK, n_slc))
    o_s = _slc_attention(slopes, proj, selbias, gates_r, b=b, t=t, k_col=q_blocks + 2 * N_KV, v_col=q_blocks + 3 * N_KV)
    o_w = _win_attention(proj, gates_r, srow, b=b, t=t, k_col=q_blocks + 4 * N_KV, v_col=q_blocks + 5 * N_KV)
    return _proj_residual([o_c, o_s, o_w], w_out.astype(BF16), x)


def _sb_mixer(x, g_norm, w_in, w_out, *, b, t):
    q_dim = N_HEADS * HEAD_DIM
    scale = HEAD_DIM ** -0.5 * LOG2E
    cgain = jnp.concatenate([jnp.full((q_dim,), scale, F32), jnp.ones((2 * q_dim,), F32)])
    proj = _norm_matmul(x, g_norm, w_in.astype(BF16), cgain, ())
    o = _sb_attention(proj, b=b, t=t)
    return _proj_residual([o], w_out.astype(BF16), x)


def kernel(x, norm_mix, norm_ffn, nsa_w_in, nsa_qk_gain, nsa_cmp_pe, nsa_cmp_w1, nsa_cmp_w2, nsa_w_out, sb_w_in, sb_w_out, moe_w_group, moe_b_group, moe_w_router, moe_b_router, moe_w_gate, moe_w_up, moe_w_down):
    b, t, d = x.shape
    depth = norm_mix.shape[0]
    xf = x.reshape(b * t, d)
    for i in range(depth):
        j = i // 2
        if i % 2 == 0:
            xf = _nsa_mixer(xf, norm_mix[i], nsa_w_in[j], nsa_qk_gain[j], nsa_cmp_pe[j], nsa_cmp_w1[j], nsa_cmp_w2[j], nsa_w_out[j], b=b, t=t)
        else:
            xf = _sb_mixer(xf, norm_mix[i], sb_w_in[j], sb_w_out[j], b=b, t=t)
        xf = _hier_moe(xf, norm_ffn[i], moe_w_group[i], moe_b_group[i], moe_w_router[i], moe_b_router[i], moe_w_gate[i], moe_w_up[i], moe_w_down[i])
    return xf.reshape(b, t, d)
```

```python
import functools

import jax
import jax.numpy as jnp
import numpy as np
from jax import lax
from jax.experimental import pallas as pl
from jax.experimental.pallas import tpu as pltpu

F32 = jnp.float32
BF16 = jnp.bfloat16

HEAD_DIM = 128
N_HEADS = 16
N_KV = 4
GQA_REP = N_HEADS // N_KV
CMP_BLOCK = 32
CMP_STRIDE = 16
SLC_BLOCK = 64
SLC_TOPK = 16
WINDOW = 512
Q_BLOCK = 128
N_GROUPS = 4
EXPERTS_PER_GROUP = 8
N_EXPERTS = N_GROUPS * EXPERTS_PER_GROUP
EPS = 1e-6
NEG = -1e30
LANES = 128
VMEM_LIMIT = 56 * 1024 * 1024

PROJ_ROWS = 512
PROJ_COLS = 512
CMP_Q_BLOCK = 256
SLC_Q_BLOCK = 256
SLC_LANES = 128
SB_TILE = 256
SB_HEADS_PER_STEP = 4
SLC_KEY_TILE = 1024
LOG2E = 1.4426950408889634
SB_UNDERFLOW = -105.0
MOE_TILE = 256
MOE_SLOTS = 3
EXP2_UNDERFLOW = 152.0
AUG_PAD_LANE = 6


def _dot(a, b):
    return jnp.dot(a, b, preferred_element_type=F32)


def _dot_nt(a, b):
    return lax.dot_general(a, b, (((1,), (1,)), ((), ())), preferred_element_type=F32)


def _params(sem):
    return pltpu.CompilerParams(dimension_semantics=sem, vmem_limit_bytes=VMEM_LIMIT)


def _resident(shape):
    return pl.BlockSpec(shape, lambda *_: (0,) * len(shape), pipeline_mode=pl.Buffered(1))


def _norm_matmul_kernel(x_ref, g_ref, w_ref, cg_ref, *rest, norm_tiles, f32_tiles, has_extra):
    if has_extra:
        wx_ref, o_ref, ox_ref, of_ref = rest
    else:
        (o_ref,) = rest
    x = x_ref[...]
    ms = jnp.mean(x * x, axis=-1, keepdims=True)
    h = (x * lax.rsqrt(ms + EPS) * g_ref[...]).astype(BF16)
    tn = PROJ_COLS
    for j in range(w_ref.shape[1] // tn):
        y = _dot(h, w_ref[:, j * tn:(j + 1) * tn])
        if j in f32_tiles:
            k = f32_tiles.index(j)
            of_ref[:, k * tn:(k + 1) * tn] = y
        for hh in range(tn // LANES):
            yh = y[:, hh * LANES:(hh + 1) * LANES]
            if j in norm_tiles:
                yh = yh * lax.rsqrt(jnp.mean(yh * yh, axis=-1, keepdims=True) + EPS)
            sl = slice(j * tn + hh * LANES, j * tn + (hh + 1) * LANES)
            o_ref[:, sl] = (yh * cg_ref[:, sl]).astype(o_ref.dtype)
    if has_extra:
        ox_ref[...] = _dot(h, wx_ref[...])


def _norm_matmul(x, g, w, cgain, norm_tiles, extra_w=None, f32_tiles=()):
    n, d = x.shape
    m = w.shape[1]
    tm = PROJ_ROWS
    has_extra = extra_w is not None
    in_specs = [pl.BlockSpec((tm, d), lambda i: (i, 0)), _resident((1, d)), _resident((d, m)), _resident((1, m))]
    out_specs = [pl.BlockSpec((tm, m), lambda i: (i, 0))]
    out_shape = [jax.ShapeDtypeStruct((n, m), BF16)]
    args = [x, g.reshape(1, d), w, cgain.reshape(1, m)]
    if has_extra:
        mx = extra_w.shape[1]
        mf = len(f32_tiles) * PROJ_COLS
        in_specs.append(_resident((d, mx)))
        out_specs += [pl.BlockSpec((tm, mx), lambda i: (i, 0)), pl.BlockSpec((tm, mf), lambda i: (i, 0))]
        out_shape += [jax.ShapeDtypeStruct((n, mx), F32), jax.ShapeDtypeStruct((n, mf), F32)]
        args.append(extra_w)
    out = pl.pallas_call(
        functools.partial(_norm_matmul_kernel, norm_tiles=tuple(norm_tiles), f32_tiles=tuple(f32_tiles), has_extra=has_extra),
        grid=(n // tm,),
        in_specs=in_specs,
        out_specs=out_specs,
        out_shape=out_shape,
        compiler_params=_params(("parallel",)),
        name="norm_matmul",
    )(*args)
    return out if has_extra else out[0]


def _proj_residual_kernel(*refs, n_in):
    a_refs = refs[:n_in]
    w_ref, r_ref, o_ref = refs[n_in:]
    if n_in == 1:
        a = a_refs[0][...]
    else:
        acc = a_refs[0][...].astype(F32)
        for a_ref in a_refs[1:]:
            acc = acc + a_ref[...].astype(F32)
        a = acc.astype(BF16)
    tn = PROJ_COLS
    for j in range(w_ref.shape[1] // tn):
        sl = slice(j * tn, (j + 1) * tn)
        o_ref[:, sl] = r_ref[:, sl] + _dot(a, w_ref[:, sl])


def _proj_residual(branches, w, res):
    n, k = branches[0].shape
    m = w.shape[1]
    tm = PROJ_ROWS
    n_in = len(branches)
    return pl.pallas_call(
        functools.partial(_proj_residual_kernel, n_in=n_in),
        grid=(n // tm,),
        in_specs=[pl.BlockSpec((tm, k), lambda i: (i, 0)) for _ in range(n_in)]
        + [_resident((k, m)), pl.BlockSpec((tm, m), lambda i: (i, 0))],
        out_specs=pl.BlockSpec((tm, m), lambda i: (i, 0)),
        out_shape=jax.ShapeDtypeStruct((n, m), F32),
        compiler_params=_params(("parallel",)),
        name="proj_residual",
    )(*branches, w, res)


def _cmp_kernel(kv_ref, w1_ref, pe_ref, w2_ref, gain_ref, o_ref):
    nc = kv_ref.shape[0] // CMP_STRIDE
    p = jnp.zeros((nc, 2 * HEAD_DIM), F32)
    pb = jnp.zeros((8, 2 * HEAD_DIM), F32)
    for l in range(CMP_STRIDE):
        w1 = w1_ref[0, l]
        p = p + _dot(kv_ref[pl.ds(l, nc, stride=CMP_STRIDE), :].astype(BF16), w1)
        pb = pb + _dot(pe_ref[0, l], w1)
    bias = pb[0:1, :HEAD_DIM] + pb[1:2, HEAD_DIM:]
    second = pltpu.roll(p[:, HEAD_DIM:], shift=nc - 1, axis=0)
    hid = jax.nn.gelu(p[:, :HEAD_DIM] + second + bias)
    c = _dot(hid.astype(BF16), w2_ref[0])
    nf = (pl.program_id(1) == 0).astype(F32)
    ms = jnp.mean(c * c, axis=-1, keepdims=True)
    fac = nf * lax.rsqrt(ms + EPS) + (1.0 - nf)
    o_ref[0, 0, 0] = (c * fac * gain_ref[0]).astype(o_ref.dtype)


def _compress(kv, w1cat, pe_rows, w2, gains, *, b, t):
    g, cs, hd = N_KV, CMP_STRIDE, HEAD_DIM
    nc = t // cs
    return pl.pallas_call(
        _cmp_kernel,
        grid=(b, 2, g),
        in_specs=[
            pl.BlockSpec((t, hd), lambda bi, k, gi: (bi, k * g + gi)),
            pl.BlockSpec((1, cs, hd, 2 * HEAD_DIM), lambda bi, k, gi: (k, 0, 0, 0)),
            pl.BlockSpec((1, cs, 8, hd), lambda bi, k, gi: (k, 0, 0, 0)),
            pl.BlockSpec((1, HEAD_DIM, HEAD_DIM), lambda bi, k, gi: (k, 0, 0)),
            pl.BlockSpec((1, 1, HEAD_DIM), lambda bi, k, gi: (k, 0, 0)),
        ],
        out_specs=pl.BlockSpec((1, 1, 1, nc, HEAD_DIM), lambda bi, k, gi: (bi, k, gi, 0, 0)),
        out_shape=jax.ShapeDtypeStruct((b, 2, g, nc, HEAD_DIM), BF16),
        compiler_params=_params(("parallel", "parallel", "parallel")),
        name="nsa_compress",
    )(kv, w1cat, pe_rows, w2, gains)


def _stack_heads(qa_ref, q_ref, srow_ref, qb):
    for r in range(GQA_REP):
        qa_ref[r * qb:(r + 1) * qb, :HEAD_DIM] = q_ref[:, r * HEAD_DIM:(r + 1) * HEAD_DIM]
        qa_ref[r * qb:(r + 1) * qb, HEAD_DIM:] = jnp.broadcast_to(srow_ref[0, r:r + 1, :], (qb, LANES))


def _cmp_attn_kernel(q_ref, kc_ref, vc_ref, gate_ref, msel_ref, srow_ref, oc_ref, sb_ref, qa_ref, ka_ref, va_ref, *, top_k):
    c = pl.program_id(2)
    qb = CMP_Q_BLOCK
    t0 = c * qb
    rows = GQA_REP * qb
    nc = ka_ref.shape[0]

    @pl.when(c == 0)
    def _():
        n_idx = lax.broadcasted_iota(jnp.int32, (nc, LANES), 0)
        lane = lax.broadcasted_iota(jnp.int32, (nc, LANES), 1)
        aug = jnp.where(lane < 3, ((n_idx >> 7) << 11).astype(F32), jnp.where(lane < 6, ((n_idx & 127) << 4).astype(F32), 0.0))
        ka_ref[:, :HEAD_DIM] = kc_ref[0, 0, 0]
        ka_ref[:, HEAD_DIM:] = aug.astype(BF16)
        va_ref[:, :HEAD_DIM] = vc_ref[0, 0, 0]
        va_ref[:, HEAD_DIM:] = (lane == 0).astype(BF16)

    _stack_heads(qa_ref, q_ref, srow_ref, qb)
    s = _dot_nt(qa_ref[...], ka_ref[...])
    t_col = t0 + (lax.broadcasted_iota(jnp.int32, (rows, 1), 0) & (qb - 1))
    cend = lax.broadcasted_iota(jnp.int32, (1, nc), 1) * CMP_STRIDE + (CMP_BLOCK - 1)
    s = jnp.where(cend <= t_col, s, NEG)
    m = jnp.max(s, axis=-1, keepdims=True)
    e = jnp.exp2(s - m)
    acc = _dot(e.astype(BF16), va_ref[...])
    inv = jnp.where(m > 0.5 * NEG, 1.0 / acc[:, HEAD_DIM:HEAD_DIM + 1], 0.0)
    gates = jax.nn.sigmoid(gate_ref[0])
    psum = jnp.zeros((qb, nc), F32)
    for r in range(GQA_REP):
        rs = slice(r * qb, (r + 1) * qb)
        psum = psum + e[rs] * inv[rs]
        oc_ref[:, r * HEAD_DIM:(r + 1) * HEAD_DIM] = (acc[rs, :HEAD_DIM] * (inv[rs] * gates[:, r:r + 1])).astype(oc_ref.dtype)

    msel = msel_ref[...]
    p_hi = psum.astype(BF16)
    rem = psum - p_hi.astype(F32)
    p_mid = rem.astype(BF16)
    p_lo = (rem - p_mid.astype(F32)).astype(BF16)
    imp = _dot(p_hi, msel) + _dot(p_mid, msel) + _dot(p_lo, msel)

    imp_t = imp.T
    j_col = lax.broadcasted_iota(jnp.int32, (SLC_LANES, 1), 0)
    t_row = t0 + lax.broadcasted_iota(jnp.int32, (1, qb), 1)
    forced = (j_col == 0) | (j_col == (t_row >> 6))
    causal = (j_col << 6) <= t_row
    score = jnp.where(forced, -NEG, jnp.where(causal, imp_t, NEG))
    keep = jnp.zeros((SLC_LANES, qb), F32)
    for _ in range(top_k):
        mx = jnp.max(score, axis=0, keepdims=True)
        first = jnp.min(jnp.where(score == mx, j_col, SLC_LANES), axis=0, keepdims=True)
        hit = j_col == first
        keep = jnp.where(hit, (mx > 0.5 * NEG).astype(F32), keep)
        score = jnp.where(hit, -3e38, score)
    bias_t = jnp.where(keep > 0.5, 0.0, NEG)
    sb_ref[0] = bias_t.T.astype(sb_ref.dtype)


def _cmp_attention(proj, cmp_kv, gates_r, msel, srow, *, b, t, top_k):
    n = b * t
    qb = CMP_Q_BLOCK
    nqb = t // qb
    rows = GQA_REP * qb
    nc = cmp_kv.shape[3]
    rowblk = lambda bi, gi, c: bi * nqb + c
    return pl.pallas_call(
        functools.partial(_cmp_attn_kernel, top_k=top_k),
        grid=(b, N_KV, nqb),
        in_specs=[
            pl.BlockSpec((qb, GQA_REP * HEAD_DIM), lambda bi, gi, c: (rowblk(bi, gi, c), gi)),
            pl.BlockSpec((1, 1, 1, nc, HEAD_DIM), lambda bi, gi, c: (bi, 0, gi, 0, 0)),
            pl.BlockSpec((1, 1, 1, nc, HEAD_DIM), lambda bi, gi, c: (bi, 1, gi, 0, 0)),
            pl.BlockSpec((1, qb, LANES), lambda bi, gi, c: (gi, rowblk(bi, gi, c), 0)),
            pl.BlockSpec((nc, SLC_LANES), lambda bi, gi, c: (0, 0)),
            pl.BlockSpec((1, 8, LANES), lambda bi, gi, c: (gi, 0, 0)),
        ],
        out_specs=[
            pl.BlockSpec((qb, GQA_REP * HEAD_DIM), lambda bi, gi, c: (rowblk(bi, gi, c), gi)),
            pl.BlockSpec((1, qb, SLC_LANES), lambda bi, gi, c: (gi, rowblk(bi, gi, c), 0)),
        ],
        out_shape=[
            jax.ShapeDtypeStruct((n, N_HEADS * HEAD_DIM), BF16),
            jax.ShapeDtypeStruct((N_KV, n, SLC_LANES), BF16),
        ],
        scratch_shapes=[
            pltpu.VMEM((rows, 2 * HEAD_DIM), BF16),
            pltpu.VMEM((nc, 2 * HEAD_DIM), BF16),
            pltpu.VMEM((nc, 2 * HEAD_DIM), BF16),
        ],
        compiler_params=_params(("parallel", "parallel", "arbitrary")),
        name="nsa_cmp_attn",
    )(proj, cmp_kv, cmp_kv, gates_r, msel, srow)


def _win_attn_kernel(q_ref, k_ref, v_ref, gate_ref, srow_ref, o_ref, qa_ref, ka_ref, va_ref):
    c = pl.program_id(2)
    t0 = pl.multiple_of(c * Q_BLOCK, Q_BLOCK)
    rows = GQA_REP * Q_BLOCK
    span = Q_BLOCK + WINDOW
    t_len = k_ref.shape[0]

    @pl.when(c == 0)
    def _():
        pos = lax.broadcasted_iota(jnp.int32, (t_len, LANES), 0)
        lane = lax.broadcasted_iota(jnp.int32, (t_len, LANES), 1)
        aug = jnp.where(lane < 3, ((pos >> 6) << 6).astype(F32), jnp.where(lane < 6, (pos & 63).astype(F32), 0.0))
        pad_lane = lax.broadcasted_iota(jnp.int32, (WINDOW, LANES), 1)
        ka_ref[:WINDOW, :HEAD_DIM] = jnp.zeros((WINDOW, HEAD_DIM), BF16)
        ka_ref[:WINDOW, HEAD_DIM:] = (pad_lane == AUG_PAD_LANE).astype(BF16)
        ka_ref[WINDOW:, :HEAD_DIM] = k_ref[...]
        ka_ref[WINDOW:, HEAD_DIM:] = aug.astype(BF16)
        va_ref[:WINDOW, :] = jnp.zeros((WINDOW, 2 * HEAD_DIM), BF16)
        va_ref[WINDOW:, :HEAD_DIM] = v_ref[...]
        va_ref[WINDOW:, HEAD_DIM:] = (lane == 0).astype(BF16)

    _stack_heads(qa_ref, q_ref, srow_ref, Q_BLOCK)
    s = _dot_nt(qa_ref[...], ka_ref[pl.ds(t0, span), :])
    i_col = lax.broadcasted_iota(jnp.int32, (rows, 1), 0) & (Q_BLOCK - 1)
    kk = lax.broadcasted_iota(jnp.int32, (1, Q_BLOCK), 1)
    first = jnp.where(kk > i_col, s[:, :Q_BLOCK], NEG)
    mid = s[:, Q_BLOCK:WINDOW]
    last = jnp.where(kk <= i_col, s[:, WINDOW:], NEG)
    m = jnp.maximum(jnp.maximum(jnp.max(first, axis=-1, keepdims=True), jnp.max(mid, axis=-1, keepdims=True)),
                    jnp.max(last, axis=-1, keepdims=True))
    acc = (_dot(jnp.exp2(first - m).astype(BF16), va_ref[pl.ds(t0, Q_BLOCK), :])
           + _dot(jnp.exp2(mid - m).astype(BF16), va_ref[pl.ds(t0 + Q_BLOCK, WINDOW - Q_BLOCK), :])
           + _dot(jnp.exp2(last - m).astype(BF16), va_ref[pl.ds(t0 + WINDOW, Q_BLOCK), :]))
    gates = jax.nn.sigmoid(gate_ref[0])
    o = acc[:, :HEAD_DIM] / acc[:, HEAD_DIM:HEAD_DIM + 1]
    for r in range(GQA_REP):
        gate = gates[:, 2 * GQA_REP + r:2 * GQA_REP + r + 1]
        o_ref[:, r * HEAD_DIM:(r + 1) * HEAD_DIM] = (o[r * Q_BLOCK:(r + 1) * Q_BLOCK] * gate).astype(o_ref.dtype)


def _win_attention(proj, gates_r, srow, *, b, t, k_col, v_col):
    n = b * t
    nqb = t // Q_BLOCK
    rows = GQA_REP * Q_BLOCK
    rowblk = lambda bi, gi, c: bi * nqb + c
    return pl.pallas_call(
        _win_attn_kernel,
        grid=(b, N_KV, nqb),
        in_specs=[
            pl.BlockSpec((Q_BLOCK, GQA_REP * HEAD_DIM), lambda bi, gi, c: (rowblk(bi, gi, c), gi)),
            pl.BlockSpec((t, HEAD_DIM), lambda bi, gi, c: (bi, k_col + gi)),
            pl.BlockSpec((t, HEAD_DIM), lambda bi, gi, c: (bi, v_col + gi)),
            pl.BlockSpec((1, Q_BLOCK, LANES), lambda bi, gi, c: (gi, rowblk(bi, gi, c), 0)),
            pl.BlockSpec((1, 8, LANES), lambda bi, gi, c: (gi, 0, 0)),
        ],
        out_specs=pl.BlockSpec((Q_BLOCK, GQA_REP * HEAD_DIM), lambda bi, gi, c: (rowblk(bi, gi, c), gi)),
        out_shape=jax.ShapeDtypeStruct((n, N_HEADS * HEAD_DIM), BF16),
        scratch_shapes=[
            pltpu.VMEM((rows, 2 * HEAD_DIM), BF16),
            pltpu.VMEM((t + WINDOW, 2 * HEAD_DIM), BF16),
            pltpu.VMEM((t + WINDOW, 2 * HEAD_DIM), BF16),
        ],
        compiler_params=_params(("parallel", "parallel", "arbitrary")),
        name="nsa_win_attn",
    )(proj, proj, proj, gates_r, srow)


def _slc_attn_kernel(reach_ref, q_ref, k_ref, v_ref, sb_ref, gate_ref, srow_ref, scol_ref, o_ref, qa_ref, ka_ref, va_ref, m_ref, acc_ref, s_ref):
    c = pl.program_id(2)
    qb = SLC_Q_BLOCK
    t0 = c * qb
    rows = GQA_REP * qb
    tk = SLC_KEY_TILE
    nb = tk // SLC_BLOCK
    t_len = k_ref.shape[0]

    @pl.when(c == 0)
    def _():
        pos = lax.broadcasted_iota(jnp.int32, (t_len, LANES), 0)
        lane = lax.broadcasted_iota(jnp.int32, (t_len, LANES), 1)
        off = pos & (tk - 1)
        blk = (pos >> 6) & (nb - 1)
        aug = jnp.where(lane < nb, (blk == lane).astype(F32),
                        jnp.where(lane < nb + 3, ((off >> 3) << 3).astype(F32),
                                  jnp.where(lane < nb + 6, (off & 7).astype(F32), 0.0)))
        ka_ref[:, :HEAD_DIM] = k_ref[...]
        ka_ref[:, HEAD_DIM:] = aug.astype(BF16)
        va_ref[:, :HEAD_DIM] = v_ref[...]
        va_ref[:, HEAD_DIM:] = (lane == 0).astype(BF16)

    for r in range(GQA_REP):
        qa_ref[r * qb:(r + 1) * qb, :HEAD_DIM] = q_ref[:, r * HEAD_DIM:(r + 1) * HEAD_DIM]
    m_ref[...] = jnp.full(m_ref.shape, NEG, F32)
    acc_ref[...] = jnp.zeros(acc_ref.shape, F32)
    sb = sb_ref[0]
    scol = scol_ref[0]
    t_col = t0 + (lax.broadcasted_iota(jnp.int32, (rows, 1), 0) & (qb - 1))
    k_iota = lax.broadcasted_iota(jnp.int32, (1, tk), 1)
    p_row = lax.broadcasted_iota(jnp.int32, (SLC_LANES, LANES), 0)
    p_col = lax.broadcasted_iota(jnp.int32, (SLC_LANES, LANES), 1)

    def scores(kt):
        k0 = pl.multiple_of(kt * tk, tk)
        pick = ((p_row == kt * nb + p_col) & (p_col < nb)).astype(BF16)
        sbt = _dot(sb, pick).astype(BF16)
        for r in range(GQA_REP):
            qa_ref[r * qb:(r + 1) * qb, HEAD_DIM:] = sbt + srow_ref[0, r:r + 1, :]
        return _dot_nt(qa_ref[...], ka_ref[pl.ds(k0, tk), :])

    def softmax_pv(kt, s, causal):
        k0 = pl.multiple_of(kt * tk, tk)
        if causal:
            s = jnp.where(k0 + k_iota <= t_col, s, NEG)
        shift = scol * (k0 - t0).astype(F32)
        m_old = m_ref[...]
        m_new = jnp.maximum(m_old, jnp.max(s, axis=-1, keepdims=True) + shift)
        alpha = jnp.exp2(m_old - m_new)
        p = jnp.exp2(s - (m_new - shift))
        acc_ref[...] = alpha * acc_ref[...] + _dot(p.astype(BF16), va_ref[pl.ds(k0, tk), :])
        m_ref[...] = m_new

    n_below = lax.div(t0, tk)
    n_first = jnp.minimum(jnp.maximum(lax.div(t0 - reach_ref[pl.program_id(1)], tk), 0), n_below)
    s_ref[...] = scores(n_first)

    def body(kt, carry):
        s_cur = s_ref[...]
        s_next = scores(kt + 1)
        softmax_pv(kt, s_cur, False)
        s_ref[...] = s_next
        return carry

    lax.fori_loop(n_first, n_below, body, 0)
    softmax_pv(n_below, s_ref[...], True)
    gates = jax.nn.sigmoid(gate_ref[0])
    acc = acc_ref[...]
    o = acc[:, :HEAD_DIM] / acc[:, HEAD_DIM:HEAD_DIM + 1]
    for r in range(GQA_REP):
        gate = gates[:, GQA_REP + r:GQA_REP + r + 1]
        o_ref[:, r * HEAD_DIM:(r + 1) * HEAD_DIM] = (o[r * qb:(r + 1) * qb] * gate).astype(o_ref.dtype)


def _slope_pieces(slopes):
    s_hi = slopes.astype(BF16)
    s_mid = (slopes - s_hi.astype(F32)).astype(BF16)
    s_lo = (slopes - s_hi.astype(F32) - s_mid.astype(F32)).astype(BF16)
    return jnp.stack([s_hi, s_mid, s_lo, s_hi, s_mid, s_lo], axis=1)


def _slope_rows(pieces, first_lane):
    rows = jnp.pad(pieces, ((0, 0), (first_lane, LANES - first_lane - pieces.shape[1]))).reshape(N_KV, GQA_REP, LANES)
    return jnp.pad(rows, ((0, 0), (0, 8 - GQA_REP), (0, 0)))


def _alibi_reach(slopes, gain_q, gain_k):
    bound = jnp.max(jnp.abs(gain_q)) * jnp.max(jnp.abs(gain_k)) * (HEAD_DIM ** 0.5 * LOG2E * 1.01)
    min_slope = jnp.min(slopes.reshape(N_KV, GQA_REP), axis=1)
    reach = jnp.ceil((2.0 * bound + EXP2_UNDERFLOW) / min_slope)
    return jnp.minimum(reach, 2.0 ** 30).astype(jnp.int32)


def _slc_attention(slopes, reach, proj, selbias, gates_r, *, b, t, k_col, v_col):
    n = b * t
    qb = SLC_Q_BLOCK
    nqb = t // qb
    rows = GQA_REP * qb
    nb = SLC_KEY_TILE // SLC_BLOCK
    rowblk = lambda bi, gi, c: bi * nqb + c
    srow = _slope_rows(_slope_pieces(slopes), nb)
    scol = jnp.repeat(slopes.reshape(N_KV, GQA_REP), qb, axis=1).reshape(N_KV, rows, 1)
    return pl.pallas_call(
        _slc_attn_kernel,
        grid=(b, N_KV, nqb),
        in_specs=[
            pl.BlockSpec(memory_space=pltpu.SMEM),
            pl.BlockSpec((qb, GQA_REP * HEAD_DIM), lambda bi, gi, c: (rowblk(bi, gi, c), gi)),
            pl.BlockSpec((t, HEAD_DIM), lambda bi, gi, c: (bi, k_col + gi)),
            pl.BlockSpec((t, HEAD_DIM), lambda bi, gi, c: (bi, v_col + gi)),
            pl.BlockSpec((1, qb, SLC_LANES), lambda bi, gi, c: (gi, rowblk(bi, gi, c), 0)),
            pl.BlockSpec((1, qb, LANES), lambda bi, gi, c: (gi, rowblk(bi, gi, c), 0)),
            pl.BlockSpec((1, 8, LANES), lambda bi, gi, c: (gi, 0, 0)),
            pl.BlockSpec((1, rows, 1), lambda bi, gi, c: (gi, 0, 0)),
        ],
        out_specs=pl.BlockSpec((qb, GQA_REP * HEAD_DIM), lambda bi, gi, c: (rowblk(bi, gi, c), gi)),
        out_shape=jax.ShapeDtypeStruct((n, N_HEADS * HEAD_DIM), BF16),
        scratch_shapes=[
            pltpu.VMEM((rows, 2 * HEAD_DIM), BF16),
            pltpu.VMEM((t, 2 * HEAD_DIM), BF16),
            pltpu.VMEM((t, 2 * HEAD_DIM), BF16),
            pltpu.VMEM((rows, 1), F32),
            pltpu.VMEM((rows, 2 * HEAD_DIM), F32),
            pltpu.VMEM((rows, SLC_KEY_TILE), F32),
        ],
        compiler_params=_params(("parallel", "parallel", "arbitrary")),
        name="nsa_slc_attn",
    )(reach, proj, proj, proj, selbias, gates_r, srow, scol)


def _sb_attn_kernel(q_ref, k_ref, v_ref, o_ref, acc_ref, carry_ref):
    ts = SB_TILE
    c = pl.program_id(2)
    acc_ref[...] = jnp.zeros(acc_ref.shape, F32)
    carry_ref[...] = jnp.zeros(carry_ref.shape, F32)
    r_idx = lax.broadcasted_iota(jnp.int32, (ts, 1), 0)
    c_idx = lax.broadcasted_iota(jnp.int32, (1, ts), 1)
    tri = (r_idx > c_idx).astype(BF16)
    diag = c_idx < r_idx

    def tile(jt, valid):
        k0 = pl.multiple_of(jt * ts, ts)
        heads = range(SB_HEADS_PER_STEP)
        cols = [slice(e * HEAD_DIM, (e + 1) * HEAD_DIM) for e in heads]
        carries = [carry_ref[e] for e in heads]
        zs = [_dot_nt(q_ref[:, cols[e]], k_ref[pl.ds(k0, ts), cols[e]]) for e in heads]
        lfs = []
        for e in heads:
            lf = -(jnp.maximum(zs[e], 0.0) + jnp.log2(1.0 + jnp.exp2(-jnp.abs(zs[e]))))
            lfs.append(lf if valid is None else jnp.where(valid, lf, 0.0))
        afters = []
        for e in heads:
            hi = lfs[e].astype(BF16)
            lo = (lfs[e] - hi.astype(F32)).astype(BF16)
            afters.append(_dot(hi, tri) + _dot(lo, tri))
        avs = []
        for e in heads:
            a = jnp.exp2(zs[e] + lfs[e] + afters[e] + carries[e])
            if valid is not None:
                a = jnp.where(valid, a, 0.0)
            avs.append(_dot(a.astype(BF16), v_ref[pl.ds(k0, ts), cols[e]]))
        for e in heads:
            acc_ref[e] += avs[e]
            carry_ref[e] = carries[e] + jnp.sum(lfs[e], axis=-1, keepdims=True)

    def live():
        return (jnp.max(carry_ref[...]) > SB_UNDERFLOW * LOG2E).astype(jnp.int32)

    tile(c, diag)

    def cond(state):
        jt, go = state
        return (jt >= 0) & (go > 0)

    def body(state):
        jt, _ = state
        tile(jt, None)
        return jt - 1, live()

    lax.while_loop(cond, body, (c - 1, live()))
    for e in range(SB_HEADS_PER_STEP):
        o_ref[:, e * HEAD_DIM:(e + 1) * HEAD_DIM] = acc_ref[e].astype(o_ref.dtype)


def _sb_attention(proj, *, b, t):
    n = b * t
    ts = SB_TILE
    nq = t // ts
    hs = SB_HEADS_PER_STEP
    width = hs * HEAD_DIM
    nhp = N_HEADS // hs
    return pl.pallas_call(
        _sb_attn_kernel,
        grid=(b, nhp, nq),
        in_specs=[
            pl.BlockSpec((ts, width), lambda bi, h, c: (bi * nq + c, h)),
            pl.BlockSpec((t, width), lambda bi, h, c: (bi, nhp + h)),
            pl.BlockSpec((t, width), lambda bi, h, c: (bi, 2 * nhp + h)),
        ],
        out_specs=pl.BlockSpec((ts, width), lambda bi, h, c: (bi * nq + c, h)),
        out_shape=jax.ShapeDtypeStruct((n, N_HEADS * HEAD_DIM), BF16),
        scratch_shapes=[pltpu.VMEM((hs, ts, HEAD_DIM), F32), pltpu.VMEM((hs, ts, 1), F32)],
        compiler_params=_params(("parallel", "parallel", "arbitrary")),
        name="sb_attn",
    )(proj, proj, proj)


def _first_max(vals, lane):
    m = jnp.max(vals, axis=-1, keepdims=True)
    idx = jnp.min(jnp.where(vals == m, lane, LANES), axis=-1, keepdims=True)
    return m, idx


def _router_kernel(x_ref, g_ref, w_ref, b_ref, o_ref):
    x = x_ref[...]
    ms = jnp.mean(x * x, axis=-1, keepdims=True)
    h = x * lax.rsqrt(ms + EPS) * g_ref[...]
    h_hi = h.astype(BF16)
    h_lo = (h - h_hi.astype(F32)).astype(BF16)
    logits = _dot(h_hi, w_ref[0]) + _dot(h_hi, w_ref[1]) + _dot(h_lo, w_ref[0]) + b_ref[...]
    lane = lax.broadcasted_iota(jnp.int32, (1, LANES), 1)
    gl = jnp.where(lane < N_GROUPS, logits, NEG)
    ge = jnp.exp(gl - jnp.max(gl, axis=-1, keepdims=True))
    pg_all = ge / jnp.sum(ge, axis=-1, keepdims=True)
    pg, gsel = _first_max(jnp.where(lane < N_GROUPS, pg_all, -1.0), lane)
    lo = N_GROUPS + gsel * EXPERTS_PER_GROUP
    in_group = (lane >= lo) & (lane < lo + EXPERTS_PER_GROUP)
    el = jnp.where(in_group, logits, NEG)
    ee = jnp.exp(el - jnp.max(el, axis=-1, keepdims=True))
    pe_all = jnp.where(in_group, ee / jnp.sum(ee, axis=-1, keepdims=True), -1.0)
    p1, i1 = _first_max(pe_all, lane)
    p2, i2 = _first_max(jnp.where(lane == i1, -1.0, pe_all), lane)
    denom = p1 + p2
    w1 = pg * p1 / denom
    w2 = pg * p2 / denom
    e1 = (i1 - N_GROUPS).astype(F32)
    e2 = (i2 - N_GROUPS).astype(F32)
    o_ref[...] = jnp.where(lane == 0, e1, jnp.where(lane == 1, e2, jnp.where(lane == 2, w1, jnp.where(lane == 3, w2, 0.0))))


def _router(x, g, w, bias, *, tm):
    n, d = x.shape
    return pl.pallas_call(
        _router_kernel,
        grid=(n // tm,),
        in_specs=[
            pl.BlockSpec((tm, d), lambda i: (i, 0)),
            pl.BlockSpec((1, d), lambda i: (0, 0)),
            pl.BlockSpec((2, d, LANES), lambda i: (0, 0, 0)),
            pl.BlockSpec((1, LANES), lambda i: (0, 0)),
        ],
        out_specs=pl.BlockSpec((tm, LANES), lambda i: (i, 0)),
        out_shape=jax.ShapeDtypeStruct((n, LANES), F32),
        compiler_params=_params(("parallel",)),
        name="moe_router",
    )(x, g.reshape(1, d), w, bias)


def _row_copy(src_hbm, dst_ref, sem, src_row, dst_row):
    return pltpu.make_async_copy(src_hbm.at[pl.ds(src_row, 1), :], dst_ref.at[pl.ds(dst_row, 1), :], sem)


def _row_gather_start(src_hbm, dst_ref, sem, idx_ref, base, count):
    def body(r, carry):
        _row_copy(src_hbm, dst_ref, sem, idx_ref[base + r], r).start()
        return carry

    lax.fori_loop(0, count, body, 0, unroll=8)


def _row_gather_wait(src_hbm, dst_ref, sem):
    pltpu.make_async_copy(src_hbm.at[pl.ds(0, dst_ref.shape[0]), :], dst_ref, sem).wait()


def _expert_kernel(src_ref, te_ref, nu_ref, x_hbm, g_ref, wg_ref, wu_ref, wd_ref, o_ref, xbuf, wgb, wub, wdb, sem):
    tm = MOE_TILE
    i = pl.program_id(0)
    nt = pl.num_programs(0)
    n_used = nu_ref[0]
    slot = i % MOE_SLOTS
    nxt = (i + 2) % MOE_SLOTS
    next_base = jnp.where(i + 2 < nt, i + 2, i + 2 - nt) * tm

    @pl.when((i < n_used) & ((i == 0) | (te_ref[i] != te_ref[jnp.maximum(i - 1, 0)])))
    def _():
        wgb[...] = wg_ref[0].astype(BF16)
        wub[...] = wu_ref[0].astype(BF16)
        wdb[...] = wd_ref[0].astype(BF16)

    @pl.when(i == 0)
    def _():
        _row_gather_start(x_hbm, xbuf.at[0], sem.at[0], src_ref, 0, tm)
        _row_gather_start(x_hbm, xbuf.at[1], sem.at[1], src_ref, tm, tm)

    _row_gather_wait(x_hbm, xbuf.at[slot], sem.at[slot])

    @pl.when(i < n_used)
    def _():
        x = xbuf[slot]
        ms = jnp.mean(x * x, axis=-1, keepdims=True)
        h = (x * lax.rsqrt(ms + EPS) * g_ref[...]).astype(BF16)
        for r in range(tm):
            _row_copy(x_hbm, xbuf.at[nxt], sem.at[nxt], src_ref[next_base + r], r).start()
        gate = _dot(h, wgb[...])
        up = _dot(h, wub[...])
        act = (gate * jax.nn.sigmoid(gate) * up).astype(BF16)
        o_ref[...] = _dot(act, wdb[...])

    @pl.when(i >= n_used)
    def _():
        _row_gather_start(x_hbm, xbuf.at[nxt], sem.at[nxt], src_ref, next_base, tm)
        o_ref[...] = jnp.zeros(o_ref.shape, o_ref.dtype)

    @pl.when(i == nt - 1)
    def _():
        prev = (i + 1) % MOE_SLOTS
        _row_gather_wait(x_hbm, xbuf.at[prev], sem.at[prev])
        _row_gather_wait(x_hbm, xbuf.at[nxt], sem.at[nxt])


def _experts(src_tok, tile_expert, n_used, x, g, wg, wu, wd):
    n, d = x.shape
    tm = MOE_TILE
    p_rows = src_tok.shape[0]
    de = wg.shape[2]
    grid_spec = pltpu.PrefetchScalarGridSpec(
        num_scalar_prefetch=3,
        grid=(p_rows // tm,),
        in_specs=[
            pl.BlockSpec(memory_space=pl.ANY),
            pl.BlockSpec((1, d), lambda i, s, te, nu: (0, 0)),
            pl.BlockSpec((1, d, de), lambda i, s, te, nu: (te[i], 0, 0)),
            pl.BlockSpec((1, d, de), lambda i, s, te, nu: (te[i], 0, 0)),
            pl.BlockSpec((1, de, d), lambda i, s, te, nu: (te[i], 0, 0)),
        ],
        out_specs=pl.BlockSpec((tm, d), lambda i, s, te, nu: (i, 0)),
        scratch_shapes=[
            pltpu.VMEM((MOE_SLOTS, tm, d), F32),
            pltpu.VMEM((d, de), BF16),
            pltpu.VMEM((d, de), BF16),
            pltpu.VMEM((de, d), BF16),
            pltpu.SemaphoreType.DMA((MOE_SLOTS,)),
        ],
    )
    return pl.pallas_call(
        _expert_kernel,
        grid_spec=grid_spec,
        out_shape=jax.ShapeDtypeStruct((p_rows, d), F32),
        compiler_params=_params(("arbitrary",)),
        name="moe_experts",
    )(src_tok, tile_expert, n_used, x, g.reshape(1, d), wg, wu, wd)


def _combine_kernel(pos_ref, x_ref, rt_ref, y_hbm, o_ref, ybuf, sem):
    tm = x_ref.shape[0]
    i = pl.program_id(0)
    nt = pl.num_programs(0)

    def start(tile, slot):
        for s in range(2):
            def body(r, carry):
                _row_copy(y_hbm, ybuf.at[slot, s], sem.at[slot], pos_ref[2 * (tile * tm + r) + s], r).start()
                return carry

            lax.fori_loop(0, tm, body, 0, unroll=8)

    @pl.when(i == 0)
    def _():
        start(0, 0)

    @pl.when(i + 1 < nt)
    def _():
        start(i + 1, (i + 1) % 2)

    slot = i % 2
    for s in range(2):
        _row_gather_wait(y_hbm, ybuf.at[slot, s], sem.at[slot])
    rt = rt_ref[...]
    o_ref[...] = x_ref[...] + rt[:, 2:3] * ybuf[slot, 0] + rt[:, 3:4] * ybuf[slot, 1]


def _combine(pos, x, routing, ys, *, tm):
    n, d = x.shape
    grid_spec = pltpu.PrefetchScalarGridSpec(
        num_scalar_prefetch=1,
        grid=(n // tm,),
        in_specs=[
            pl.BlockSpec((tm, d), lambda i, p: (i, 0)),
            pl.BlockSpec((tm, LANES), lambda i, p: (i, 0)),
            pl.BlockSpec(memory_space=pl.ANY),
        ],
        out_specs=pl.BlockSpec((tm, d), lambda i, p: (i, 0)),
        scratch_shapes=[pltpu.VMEM((2, 2, tm, d), F32), pltpu.SemaphoreType.DMA((2,))],
    )
    return pl.pallas_call(
        _combine_kernel,
        grid_spec=grid_spec,
        out_shape=jax.ShapeDtypeStruct((n, d), F32),
        compiler_params=_params(("arbitrary",)),
        name="moe_combine",
    )(pos, x, routing, ys)


def _hier_moe(x, g, w_group, b_group, w_router, b_router, w_gate, w_up, w_down):
    n, d = x.shape
    tm = MOE_TILE
    n_logit = N_GROUPS + N_EXPERTS
    w_r = jnp.pad(jnp.concatenate([w_group, w_router], axis=1), ((0, 0), (0, LANES - n_logit)))
    b_r = jnp.pad(jnp.concatenate([b_group, b_router]), (0, LANES - n_logit)).reshape(1, LANES)
    w_hi = w_r.astype(BF16)
    w_split = jnp.stack([w_hi, (w_r - w_hi.astype(F32)).astype(BF16)])
    routing = _router(x, g, w_split, b_r, tm=512 if n % 512 == 0 else n)

    e_flat = routing[:, :2].astype(jnp.int32).reshape(-1)
    onehot = (e_flat[:, None] == jnp.arange(N_EXPERTS, dtype=jnp.int32)[None, :]).astype(jnp.int32)
    counts = jnp.sum(onehot, axis=0)
    rank = jnp.sum((jnp.cumsum(onehot, axis=0) - onehot) * onehot, axis=1)
    padded = ((counts + tm - 1) // tm) * tm
    ends = jnp.cumsum(padded)
    pos = (ends - padded)[e_flat] + rank
    p_rows = 2 * n + N_EXPERTS * tm
    src_tok = jnp.zeros((p_rows,), jnp.int32).at[pos].set(jnp.arange(2 * n, dtype=jnp.int32) // 2)
    tile_start = jnp.arange(p_rows // tm, dtype=jnp.int32) * tm
    tile_expert = jnp.minimum(jnp.sum((ends[None, :] <= tile_start[:, None]).astype(jnp.int32), axis=1), N_EXPERTS - 1)
    n_used = (ends[-1] // tm).astype(jnp.int32).reshape(1)

    ys = _experts(src_tok, tile_expert, n_used, x, g, w_gate, w_up, w_down)
    return _combine(pos.astype(jnp.int32), x, routing, ys, tm=256 if n % 256 == 0 else n)


def _alibi_slopes():
    return jnp.exp2(-8.0 * jnp.arange(1, N_HEADS + 1, dtype=F32) / N_HEADS)


def _cmp_to_slc(n_cmp_pad, n_slc):
    cs = np.arange(n_cmp_pad) * CMP_STRIDE
    ss = np.arange(SLC_LANES) * SLC_BLOCK
    lo = np.maximum(cs[:, None], ss[None, :])
    hi = np.minimum(cs[:, None] + CMP_BLOCK, ss[None, :] + SLC_BLOCK)
    m = np.maximum(hi - lo, 0).astype(np.float32) / CMP_BLOCK
    m[:, n_slc:] = 0.0
    return m


def _nsa_mixer(x, g_norm, w_in, qk_gain, cmp_pe, cmp_w1, cmp_w2, w_out, *, b, t):
    n, d = x.shape
    q_dim = N_HEADS * HEAD_DIM
    kv_dim = N_KV * HEAD_DIM
    scale = HEAD_DIM ** -0.5 * LOG2E
    main = q_dim + 6 * kv_dim
    ones = jnp.ones((kv_dim,), F32)
    tile4 = lambda v: jnp.tile(v, N_KV)
    cgain = jnp.concatenate([jnp.tile(qk_gain[0], N_HEADS) * scale, ones, ones, tile4(qk_gain[2]), ones, tile4(qk_gain[3]), ones])
    tiles_per = lambda cols: cols // PROJ_COLS
    q_tiles = tiles_per(q_dim)
    kv_tiles = tiles_per(kv_dim)
    norm_tiles = list(range(q_tiles)) + [q_tiles + 2 * kv_tiles + j for j in range(kv_tiles)] + [q_tiles + 4 * kv_tiles + j for j in range(kv_tiles)]
    n_gate = 3 * N_HEADS
    w_gate = jnp.pad(w_in[:, main:], ((0, 0), (0, LANES - n_gate))).astype(BF16)
    cmp_tiles = [q_tiles + j for j in range(2 * kv_tiles)]
    proj, gate_logits, kv_cmp = _norm_matmul(x, g_norm, w_in[:, :main].astype(BF16), cgain, norm_tiles, extra_w=w_gate, f32_tiles=cmp_tiles)
    gr = gate_logits[:, :n_gate].reshape(n, 3, N_KV, GQA_REP).transpose(2, 0, 1, 3).reshape(N_KV, n, 3 * GQA_REP)
    gates_r = jnp.pad(gr, ((0, 0), (0, 0), (0, LANES - 3 * GQA_REP)))

    nc = t // CMP_STRIDE
    half = CMP_BLOCK // 2
    w1cat = jnp.concatenate([cmp_w1[:, :half], cmp_w1[:, half:]], axis=-1)
    pe_rows = jnp.pad(jnp.stack([cmp_pe[:, :half], cmp_pe[:, half:]], axis=2), ((0, 0), (0, 0), (0, 6), (0, 0)))
    cmp_gain = jnp.stack([qk_gain[1], jnp.ones((HEAD_DIM,), F32)]).reshape(2, 1, HEAD_DIM)
    cmp_kv = _compress(kv_cmp, w1cat.astype(BF16), pe_rows.astype(BF16), cmp_w2.astype(BF16), cmp_gain, b=b, t=t)

    slopes = _alibi_slopes() * LOG2E
    pieces = _slope_pieces(slopes)
    pad_flag = jnp.full((N_HEADS, 1), NEG, F32).astype(BF16)
    srow = _slope_rows(jnp.concatenate([pieces, pad_flag], axis=1), 0)
    n_slc = t // SLC_BLOCK
    msel = jnp.asarray(_cmp_to_slc(nc, n_slc), BF16)
    q_blocks = q_dim // HEAD_DIM
    o_c, selbias = _cmp_attention(proj, cmp_kv, gates_r, msel, srow, b=b, t=t, top_k=min(SLC_TOPK, n_slc))
    reach = _alibi_reach(slopes, qk_gain[0], qk_gain[2])
    o_s = _slc_attention(slopes, reach, proj, selbias, gates_r, b=b, t=t, k_col=q_blocks + 2 * N_KV, v_col=q_blocks + 3 * N_KV)
    o_w = _win_attention(proj, gates_r, srow, b=b, t=t, k_col=q_blocks + 4 * N_KV, v_col=q_blocks + 5 * N_KV)
    return _proj_residual([o_c, o_s, o_w], w_out.astype(BF16), x)


def _sb_mixer(x, g_norm, w_in, w_out, *, b, t):
    q_dim = N_HEADS * HEAD_DIM
    scale = HEAD_DIM ** -0.5 * LOG2E
    cgain = jnp.concatenate([jnp.full((q_dim,), scale, F32), jnp.ones((2 * q_dim,), F32)])
    proj = _norm_matmul(x, g_norm, w_in.astype(BF16), cgain, ())
    o = _sb_attention(proj, b=b, t=t)
    return _proj_residual([o], w_out.astype(BF16), x)


def kernel(x, norm_mix, norm_ffn, nsa_w_in, nsa_qk_gain, nsa_cmp_pe, nsa_cmp_w1, nsa_cmp_w2, nsa_w_out, sb_w_in, sb_w_out, moe_w_group, moe_b_group, moe_w_router, moe_b_router, moe_w_gate, moe_w_up, moe_w_down):
    b, t, d = x.shape
    depth = norm_mix.shape[0]
    xf = x.reshape(b * t, d)
    for i in range(depth):
        j = i // 2
        if i % 2 == 0:
            xf = _nsa_mixer(xf, norm_mix[i], nsa_w_in[j], nsa_qk_gain[j], nsa_cmp_pe[j], nsa_cmp_w1[j], nsa_cmp_w2[j], nsa_w_out[j], b=b, t=t)
        else:
            xf = _sb_mixer(xf, norm_mix[i], sb_w_in[j], sb_w_out[j], b=b, t=t)
        xf = _hier_moe(xf, norm_ffn[i], moe_w_group[i], moe_b_group[i], moe_w_router[i], moe_b_router[i], moe_w_gate[i], moe_w_up[i], moe_w_down[i])
    return xf.reshape(b, t, d)
```

```python
import functools

import jax
import jax.numpy as jnp
import numpy as np
from jax import lax
from jax.experimental import pallas as pl
from jax.experimental.pallas import tpu as pltpu

F32 = jnp.float32
BF16 = jnp.bfloat16

HEAD_DIM = 128
N_HEADS = 16
N_KV = 4
GQA_REP = N_HEADS // N_KV
CMP_BLOCK = 32
CMP_STRIDE = 16
SLC_BLOCK = 64
SLC_TOPK = 16
WINDOW = 512
Q_BLOCK = 256
N_GROUPS = 4
EXPERTS_PER_GROUP = 8
N_EXPERTS = N_GROUPS * EXPERTS_PER_GROUP
EPS = 1e-6
NEG = -1e30
LANES = 128
VMEM_LIMIT = 56 * 1024 * 1024

PROJ_ROWS = 512
PROJ_COLS = 512
CMP_Q_BLOCK = 256
SLC_Q_BLOCK = 256
SLC_LANES = 128
SB_TILE = 256
SB_HEADS_PER_STEP = 4
SLC_KEY_TILE = 1024
LOG2E = 1.4426950408889634
SB_UNDERFLOW = -105.0
MOE_TILE = 256
MOE_SLOTS = 3
EXP2_UNDERFLOW = 152.0
AUG_PAD_LANE = 6


def _dot(a, b):
    return jnp.dot(a, b, preferred_element_type=F32)


def _dot_nt(a, b):
    return lax.dot_general(a, b, (((1,), (1,)), ((), ())), preferred_element_type=F32)


def _params(sem):
    return pltpu.CompilerParams(dimension_semantics=sem, vmem_limit_bytes=VMEM_LIMIT)


def _resident(shape):
    return pl.BlockSpec(shape, lambda *_: (0,) * len(shape), pipeline_mode=pl.Buffered(1))


def _norm_matmul_kernel(x_ref, g_ref, w_ref, cg_ref, *rest, norm_tiles, f32_tiles, has_extra):
    if has_extra:
        wx_ref, o_ref, ox_ref, of_ref = rest
    else:
        (o_ref,) = rest
    x = x_ref[...]
    ms = jnp.mean(x * x, axis=-1, keepdims=True)
    h = (x * lax.rsqrt(ms + EPS) * g_ref[...]).astype(BF16)
    tn = PROJ_COLS
    for j in range(w_ref.shape[1] // tn):
        y = _dot(h, w_ref[:, j * tn:(j + 1) * tn])
        if j in f32_tiles:
            k = f32_tiles.index(j)
            of_ref[:, k * tn:(k + 1) * tn] = y
        for hh in range(tn // LANES):
            yh = y[:, hh * LANES:(hh + 1) * LANES]
            if j in norm_tiles:
                yh = yh * lax.rsqrt(jnp.mean(yh * yh, axis=-1, keepdims=True) + EPS)
            sl = slice(j * tn + hh * LANES, j * tn + (hh + 1) * LANES)
            o_ref[:, sl] = (yh * cg_ref[:, sl]).astype(o_ref.dtype)
    if has_extra:
        ox_ref[...] = _dot(h, wx_ref[...])


def _norm_matmul(x, g, w, cgain, norm_tiles, extra_w=None, f32_tiles=()):
    n, d = x.shape
    m = w.shape[1]
    tm = PROJ_ROWS
    has_extra = extra_w is not None
    in_specs = [pl.BlockSpec((tm, d), lambda i: (i, 0)), _resident((1, d)), _resident((d, m)), _resident((1, m))]
    out_specs = [pl.BlockSpec((tm, m), lambda i: (i, 0))]
    out_shape = [jax.ShapeDtypeStruct((n, m), BF16)]
    args = [x, g.reshape(1, d), w, cgain.reshape(1, m)]
    if has_extra:
        mx = extra_w.shape[1]
        mf = len(f32_tiles) * PROJ_COLS
        in_specs.append(_resident((d, mx)))
        out_specs += [pl.BlockSpec((tm, mx), lambda i: (i, 0)), pl.BlockSpec((tm, mf), lambda i: (i, 0))]
        out_shape += [jax.ShapeDtypeStruct((n, mx), F32), jax.ShapeDtypeStruct((n, mf), F32)]
        args.append(extra_w)
    out = pl.pallas_call(
        functools.partial(_norm_matmul_kernel, norm_tiles=tuple(norm_tiles), f32_tiles=tuple(f32_tiles), has_extra=has_extra),
        grid=(n // tm,),
        in_specs=in_specs,
        out_specs=out_specs,
        out_shape=out_shape,
        compiler_params=_params(("parallel",)),
        name="norm_matmul",
    )(*args)
    return out if has_extra else out[0]


def _first_max(vals, lane):
    m = jnp.max(vals, axis=-1, keepdims=True)
    idx = jnp.min(jnp.where(vals == m, lane, LANES), axis=-1, keepdims=True)
    return m, idx


def _route(x, g_ref, w_ref, b_ref):
    ms = jnp.mean(x * x, axis=-1, keepdims=True)
    h = x * lax.rsqrt(ms + EPS) * g_ref[...]
    h_hi = h.astype(BF16)
    h_lo = (h - h_hi.astype(F32)).astype(BF16)
    logits = _dot(h_hi, w_ref[0]) + _dot(h_hi, w_ref[1]) + _dot(h_lo, w_ref[0]) + b_ref[...]
    lane = lax.broadcasted_iota(jnp.int32, (1, LANES), 1)
    gl = jnp.where(lane < N_GROUPS, logits, NEG)
    ge = jnp.exp(gl - jnp.max(gl, axis=-1, keepdims=True))
    pg_all = ge / jnp.sum(ge, axis=-1, keepdims=True)
    pg, gsel = _first_max(jnp.where(lane < N_GROUPS, pg_all, -1.0), lane)
    lo = N_GROUPS + gsel * EXPERTS_PER_GROUP
    in_group = (lane >= lo) & (lane < lo + EXPERTS_PER_GROUP)
    el = jnp.where(in_group, logits, NEG)
    ee = jnp.exp(el - jnp.max(el, axis=-1, keepdims=True))
    pe_all = jnp.where(in_group, ee / jnp.sum(ee, axis=-1, keepdims=True), -1.0)
    p1, i1 = _first_max(pe_all, lane)
    p2, i2 = _first_max(jnp.where(lane == i1, -1.0, pe_all), lane)
    denom = p1 + p2
    w1 = pg * p1 / denom
    w2 = pg * p2 / denom
    e1 = (i1 - N_GROUPS).astype(F32)
    e2 = (i2 - N_GROUPS).astype(F32)
    return jnp.where(lane == 0, e1, jnp.where(lane == 1, e2, jnp.where(lane == 2, w1, jnp.where(lane == 3, w2, 0.0))))


def _proj_residual_kernel(*refs, n_in):
    a_refs = refs[:n_in]
    w_ref, r_ref, g_ref, wr_ref, br_ref, o_ref, rt_ref = refs[n_in:]
    if n_in == 1:
        a = a_refs[0][...]
    else:
        acc = a_refs[0][...].astype(F32)
        for a_ref in a_refs[1:]:
            acc = acc + a_ref[...].astype(F32)
        a = acc.astype(BF16)
    tn = PROJ_COLS
    for j in range(w_ref.shape[1] // tn):
        sl = slice(j * tn, (j + 1) * tn)
        o_ref[:, sl] = r_ref[:, sl] + _dot(a, w_ref[:, sl])
    rt_ref[...] = _route(o_ref[...], g_ref, wr_ref, br_ref)


def _proj_residual(branches, w, res, router):
    n, k = branches[0].shape
    m = w.shape[1]
    tm = PROJ_ROWS
    n_in = len(branches)
    g_ffn, w_split, b_r = router
    return pl.pallas_call(
        functools.partial(_proj_residual_kernel, n_in=n_in),
        grid=(n // tm,),
        in_specs=[pl.BlockSpec((tm, k), lambda i: (i, 0)) for _ in range(n_in)]
        + [_resident((k, m)), pl.BlockSpec((tm, m), lambda i: (i, 0)), _resident((1, m)), _resident(w_split.shape), _resident((1, LANES))],
        out_specs=[pl.BlockSpec((tm, m), lambda i: (i, 0)), pl.BlockSpec((tm, LANES), lambda i: (i, 0))],
        out_shape=[jax.ShapeDtypeStruct((n, m), F32), jax.ShapeDtypeStruct((n, LANES), F32)],
        compiler_params=_params(("parallel",)),
        name="proj_residual",
    )(*branches, w, res, g_ffn.reshape(1, m), w_split, b_r)


def _router_params(g_ffn, w_group, b_group, w_router, b_router):
    n_logit = N_GROUPS + N_EXPERTS
    w_r = jnp.pad(jnp.concatenate([w_group, w_router], axis=1), ((0, 0), (0, LANES - n_logit)))
    b_r = jnp.pad(jnp.concatenate([b_group, b_router]), (0, LANES - n_logit)).reshape(1, LANES)
    w_hi = w_r.astype(BF16)
    return g_ffn, jnp.stack([w_hi, (w_r - w_hi.astype(F32)).astype(BF16)]), b_r


def _cmp_kernel(kv_ref, w1_ref, pe_ref, w2_ref, gain_ref, o_ref):
    nc = kv_ref.shape[0] // CMP_STRIDE
    p = jnp.zeros((nc, 2 * HEAD_DIM), F32)
    pb = jnp.zeros((8, 2 * HEAD_DIM), F32)
    for l in range(CMP_STRIDE):
        w1 = w1_ref[0, l]
        p = p + _dot(kv_ref[pl.ds(l, nc, stride=CMP_STRIDE), :].astype(BF16), w1)
        pb = pb + _dot(pe_ref[0, l], w1)
    bias = pb[0:1, :HEAD_DIM] + pb[1:2, HEAD_DIM:]
    second = pltpu.roll(p[:, HEAD_DIM:], shift=nc - 1, axis=0)
    hid = jax.nn.gelu(p[:, :HEAD_DIM] + second + bias)
    c = _dot(hid.astype(BF16), w2_ref[0])
    nf = (pl.program_id(1) == 0).astype(F32)
    ms = jnp.mean(c * c, axis=-1, keepdims=True)
    fac = nf * lax.rsqrt(ms + EPS) + (1.0 - nf)
    o_ref[0, 0, 0] = (c * fac * gain_ref[0]).astype(o_ref.dtype)


def _compress(kv, w1cat, pe_rows, w2, gains, *, b, t):
    g, cs, hd = N_KV, CMP_STRIDE, HEAD_DIM
    nc = t // cs
    return pl.pallas_call(
        _cmp_kernel,
        grid=(b, 2, g),
        in_specs=[
            pl.BlockSpec((t, hd), lambda bi, k, gi: (bi, k * g + gi)),
            pl.BlockSpec((1, cs, hd, 2 * HEAD_DIM), lambda bi, k, gi: (k, 0, 0, 0)),
            pl.BlockSpec((1, cs, 8, hd), lambda bi, k, gi: (k, 0, 0, 0)),
            pl.BlockSpec((1, HEAD_DIM, HEAD_DIM), lambda bi, k, gi: (k, 0, 0)),
            pl.BlockSpec((1, 1, HEAD_DIM), lambda bi, k, gi: (k, 0, 0)),
        ],
        out_specs=pl.BlockSpec((1, 1, 1, nc, HEAD_DIM), lambda bi, k, gi: (bi, k, gi, 0, 0)),
        out_shape=jax.ShapeDtypeStruct((b, 2, g, nc, HEAD_DIM), BF16),
        compiler_params=_params(("parallel", "parallel", "parallel")),
        name="nsa_compress",
    )(kv, w1cat, pe_rows, w2, gains)


def _stack_heads(qa_ref, q_ref, srow_ref, qb):
    for r in range(GQA_REP):
        qa_ref[r * qb:(r + 1) * qb, :HEAD_DIM] = q_ref[:, r * HEAD_DIM:(r + 1) * HEAD_DIM]
        qa_ref[r * qb:(r + 1) * qb, HEAD_DIM:] = jnp.broadcast_to(srow_ref[0, r:r + 1, :], (qb, LANES))


def _cmp_attn_kernel(q_ref, kc_ref, vc_ref, gate_ref, msel_ref, srow_ref, oc_ref, sb_ref, qa_ref, ka_ref, va_ref, *, top_k):
    c = pl.program_id(2)
    qb = CMP_Q_BLOCK
    t0 = c * qb
    rows = GQA_REP * qb
    nc = ka_ref.shape[0]

    @pl.when(c == 0)
    def _():
        n_idx = lax.broadcasted_iota(jnp.int32, (nc, LANES), 0)
        lane = lax.broadcasted_iota(jnp.int32, (nc, LANES), 1)
        aug = jnp.where(lane < 3, ((n_idx >> 7) << 11).astype(F32), jnp.where(lane < 6, ((n_idx & 127) << 4).astype(F32), 0.0))
        ka_ref[:, :HEAD_DIM] = kc_ref[0, 0, 0]
        ka_ref[:, HEAD_DIM:] = aug.astype(BF16)
        va_ref[:, :HEAD_DIM] = vc_ref[0, 0, 0]
        va_ref[:, HEAD_DIM:] = (lane == 0).astype(BF16)

    _stack_heads(qa_ref, q_ref, srow_ref, qb)
    s = _dot_nt(qa_ref[...], ka_ref[...])
    t_col = t0 + (lax.broadcasted_iota(jnp.int32, (rows, 1), 0) & (qb - 1))
    cend = lax.broadcasted_iota(jnp.int32, (1, nc), 1) * CMP_STRIDE + (CMP_BLOCK - 1)
    s = jnp.where(cend <= t_col, s, NEG)
    m = jnp.max(s, axis=-1, keepdims=True)
    e = jnp.exp2(s - m)
    acc = _dot(e.astype(BF16), va_ref[...])
    inv = jnp.where(m > 0.5 * NEG, 1.0 / acc[:, HEAD_DIM:HEAD_DIM + 1], 0.0)
    gates = jax.nn.sigmoid(gate_ref[0])
    psum = jnp.zeros((qb, nc), F32)
    for r in range(GQA_REP):
        rs = slice(r * qb, (r + 1) * qb)
        psum = psum + e[rs] * inv[rs]
        oc_ref[:, r * HEAD_DIM:(r + 1) * HEAD_DIM] = (acc[rs, :HEAD_DIM] * (inv[rs] * gates[:, r:r + 1])).astype(oc_ref.dtype)

    msel = msel_ref[...]
    p_hi = psum.astype(BF16)
    rem = psum - p_hi.astype(F32)
    p_mid = rem.astype(BF16)
    p_lo = (rem - p_mid.astype(F32)).astype(BF16)
    imp = _dot(p_hi, msel) + _dot(p_mid, msel) + _dot(p_lo, msel)

    imp_t = imp.T
    j_col = lax.broadcasted_iota(jnp.int32, (SLC_LANES, 1), 0)
    t_row = t0 + lax.broadcasted_iota(jnp.int32, (1, qb), 1)
    forced = (j_col == 0) | (j_col == (t_row >> 6))
    causal = (j_col << 6) <= t_row
    score = jnp.where(forced, -NEG, jnp.where(causal, imp_t, NEG))
    keep = jnp.zeros((SLC_LANES, qb), F32)
    for _ in range(top_k):
        mx = jnp.max(score, axis=0, keepdims=True)
        first = jnp.min(jnp.where(score == mx, j_col, SLC_LANES), axis=0, keepdims=True)
        hit = j_col == first
        keep = jnp.where(hit, (mx > 0.5 * NEG).astype(F32), keep)
        score = jnp.where(hit, -3e38, score)
    bias_t = jnp.where(keep > 0.5, 0.0, NEG)
    sb_ref[0] = bias_t.T.astype(sb_ref.dtype)


def _cmp_attention(proj, cmp_kv, gates_r, msel, srow, *, b, t, top_k):
    n = b * t
    qb = CMP_Q_BLOCK
    nqb = t // qb
    rows = GQA_REP * qb
    nc = cmp_kv.shape[3]
    rowblk = lambda bi, gi, c: bi * nqb + c
    return pl.pallas_call(
        functools.partial(_cmp_attn_kernel, top_k=top_k),
        grid=(b, N_KV, nqb),
        in_specs=[
            pl.BlockSpec((qb, GQA_REP * HEAD_DIM), lambda bi, gi, c: (rowblk(bi, gi, c), gi)),
            pl.BlockSpec((1, 1, 1, nc, HEAD_DIM), lambda bi, gi, c: (bi, 0, gi, 0, 0)),
            pl.BlockSpec((1, 1, 1, nc, HEAD_DIM), lambda bi, gi, c: (bi, 1, gi, 0, 0)),
            pl.BlockSpec((1, qb, LANES), lambda bi, gi, c: (gi, rowblk(bi, gi, c), 0)),
            pl.BlockSpec((nc, SLC_LANES), lambda bi, gi, c: (0, 0)),
            pl.BlockSpec((1, 8, LANES), lambda bi, gi, c: (gi, 0, 0)),
        ],
        out_specs=[
            pl.BlockSpec((qb, GQA_REP * HEAD_DIM), lambda bi, gi, c: (rowblk(bi, gi, c), gi)),
            pl.BlockSpec((1, qb, SLC_LANES), lambda bi, gi, c: (gi, rowblk(bi, gi, c), 0)),
        ],
        out_shape=[
            jax.ShapeDtypeStruct((n, N_HEADS * HEAD_DIM), BF16),
            jax.ShapeDtypeStruct((N_KV, n, SLC_LANES), BF16),
        ],
        scratch_shapes=[
            pltpu.VMEM((rows, 2 * HEAD_DIM), BF16),
            pltpu.VMEM((nc, 2 * HEAD_DIM), BF16),
            pltpu.VMEM((nc, 2 * HEAD_DIM), BF16),
        ],
        compiler_params=_params(("parallel", "parallel", "arbitrary")),
        name="nsa_cmp_attn",
    )(proj, cmp_kv, cmp_kv, gates_r, msel, srow)


def _win_attn_kernel(q_ref, k_ref, v_ref, gate_ref, srow_ref, o_ref, qa_ref, ka_ref, va_ref):
    c = pl.program_id(2)
    t0 = pl.multiple_of(c * Q_BLOCK, Q_BLOCK)
    rows = GQA_REP * Q_BLOCK
    span = Q_BLOCK + WINDOW
    t_len = k_ref.shape[0]

    @pl.when(c == 0)
    def _():
        pos = lax.broadcasted_iota(jnp.int32, (t_len, LANES), 0)
        lane = lax.broadcasted_iota(jnp.int32, (t_len, LANES), 1)
        aug = jnp.where(lane < 3, ((pos >> 6) << 6).astype(F32), jnp.where(lane < 6, (pos & 63).astype(F32), 0.0))
        pad_lane = lax.broadcasted_iota(jnp.int32, (WINDOW, LANES), 1)
        ka_ref[:WINDOW, :HEAD_DIM] = jnp.zeros((WINDOW, HEAD_DIM), BF16)
        ka_ref[:WINDOW, HEAD_DIM:] = (pad_lane == AUG_PAD_LANE).astype(BF16)
        ka_ref[WINDOW:, :HEAD_DIM] = k_ref[...]
        ka_ref[WINDOW:, HEAD_DIM:] = aug.astype(BF16)
        va_ref[:WINDOW, :] = jnp.zeros((WINDOW, 2 * HEAD_DIM), BF16)
        va_ref[WINDOW:, :HEAD_DIM] = v_ref[...]
        va_ref[WINDOW:, HEAD_DIM:] = (lane == 0).astype(BF16)

    _stack_heads(qa_ref, q_ref, srow_ref, Q_BLOCK)
    s = _dot_nt(qa_ref[...], ka_ref[pl.ds(t0, span), :])
    i_col = lax.broadcasted_iota(jnp.int32, (rows, 1), 0) & (Q_BLOCK - 1)
    kk = lax.broadcasted_iota(jnp.int32, (1, Q_BLOCK), 1)
    first = jnp.where(kk > i_col, s[:, :Q_BLOCK], NEG)
    mid = s[:, Q_BLOCK:WINDOW]
    last = jnp.where(kk <= i_col, s[:, WINDOW:], NEG)
    m = jnp.maximum(jnp.maximum(jnp.max(first, axis=-1, keepdims=True), jnp.max(mid, axis=-1, keepdims=True)),
                    jnp.max(last, axis=-1, keepdims=True))
    acc = (_dot(jnp.exp2(first - m).astype(BF16), va_ref[pl.ds(t0, Q_BLOCK), :])
           + _dot(jnp.exp2(mid - m).astype(BF16), va_ref[pl.ds(t0 + Q_BLOCK, WINDOW - Q_BLOCK), :])
           + _dot(jnp.exp2(last - m).astype(BF16), va_ref[pl.ds(t0 + WINDOW, Q_BLOCK), :]))
    gates = jax.nn.sigmoid(gate_ref[0])
    o = acc[:, :HEAD_DIM] / acc[:, HEAD_DIM:HEAD_DIM + 1]
    for r in range(GQA_REP):
        gate = gates[:, 2 * GQA_REP + r:2 * GQA_REP + r + 1]
        o_ref[:, r * HEAD_DIM:(r + 1) * HEAD_DIM] = (o[r * Q_BLOCK:(r + 1) * Q_BLOCK] * gate).astype(o_ref.dtype)


def _win_attention(proj, gates_r, srow, *, b, t, k_col, v_col):
    n = b * t
    nqb = t // Q_BLOCK
    rows = GQA_REP * Q_BLOCK
    rowblk = lambda bi, gi, c: bi * nqb + c
    return pl.pallas_call(
        _win_attn_kernel,
        grid=(b, N_KV, nqb),
        in_specs=[
            pl.BlockSpec((Q_BLOCK, GQA_REP * HEAD_DIM), lambda bi, gi, c: (rowblk(bi, gi, c), gi)),
            pl.BlockSpec((t, HEAD_DIM), lambda bi, gi, c: (bi, k_col + gi)),
            pl.BlockSpec((t, HEAD_DIM), lambda bi, gi, c: (bi, v_col + gi)),
            pl.BlockSpec((1, Q_BLOCK, LANES), lambda bi, gi, c: (gi, rowblk(bi, gi, c), 0)),
            pl.BlockSpec((1, 8, LANES), lambda bi, gi, c: (gi, 0, 0)),
        ],
        out_specs=pl.BlockSpec((Q_BLOCK, GQA_REP * HEAD_DIM), lambda bi, gi, c: (rowblk(bi, gi, c), gi)),
        out_shape=jax.ShapeDtypeStruct((n, N_HEADS * HEAD_DIM), BF16),
        scratch_shapes=[
            pltpu.VMEM((rows, 2 * HEAD_DIM), BF16),
            pltpu.VMEM((t + WINDOW, 2 * HEAD_DIM), BF16),
            pltpu.VMEM((t + WINDOW, 2 * HEAD_DIM), BF16),
        ],
        compiler_params=_params(("parallel", "parallel", "arbitrary")),
        name="nsa_win_attn",
    )(proj, proj, proj, gates_r, srow)


def _slc_attn_kernel(reach_ref, q_ref, k_ref, v_ref, sb_ref, gate_ref, srow_ref, scol_ref, o_ref, qa_ref, ka_ref, va_ref, m_ref, acc_ref, s_ref):
    c = pl.program_id(2)
    qb = SLC_Q_BLOCK
    t0 = c * qb
    rows = GQA_REP * qb
    tk = SLC_KEY_TILE
    nb = tk // SLC_BLOCK
    t_len = k_ref.shape[0]

    @pl.when(c == 0)
    def _():
        pos = lax.broadcasted_iota(jnp.int32, (t_len, LANES), 0)
        lane = lax.broadcasted_iota(jnp.int32, (t_len, LANES), 1)
        off = pos & (tk - 1)
        blk = (pos >> 6) & (nb - 1)
        aug = jnp.where(lane < nb, (blk == lane).astype(F32),
                        jnp.where(lane < nb + 3, ((off >> 3) << 3).astype(F32),
                                  jnp.where(lane < nb + 6, (off & 7).astype(F32), 0.0)))
        ka_ref[:, :HEAD_DIM] = k_ref[...]
        ka_ref[:, HEAD_DIM:] = aug.astype(BF16)
        va_ref[:, :HEAD_DIM] = v_ref[...]
        va_ref[:, HEAD_DIM:] = (lane == 0).astype(BF16)

    for r in range(GQA_REP):
        qa_ref[r * qb:(r + 1) * qb, :HEAD_DIM] = q_ref[:, r * HEAD_DIM:(r + 1) * HEAD_DIM]
    m_ref[...] = jnp.full(m_ref.shape, NEG, F32)
    acc_ref[...] = jnp.zeros(acc_ref.shape, F32)
    sb = sb_ref[0]
    scol = scol_ref[0]
    t_col = t0 + (lax.broadcasted_iota(jnp.int32, (rows, 1), 0) & (qb - 1))
    k_iota = lax.broadcasted_iota(jnp.int32, (1, tk), 1)
    p_row = lax.broadcasted_iota(jnp.int32, (SLC_LANES, LANES), 0)
    p_col = lax.broadcasted_iota(jnp.int32, (SLC_LANES, LANES), 1)

    def scores(kt):
        k0 = pl.multiple_of(kt * tk, tk)
        pick = ((p_row == kt * nb + p_col) & (p_col < nb)).astype(BF16)
        sbt = _dot(sb, pick).astype(BF16)
        for r in range(GQA_REP):
            qa_ref[r * qb:(r + 1) * qb, HEAD_DIM:] = sbt + srow_ref[0, r:r + 1, :]
        return _dot_nt(qa_ref[...], ka_ref[pl.ds(k0, tk), :])

    def softmax_pv(kt, s, causal):
        k0 = pl.multiple_of(kt * tk, tk)
        if causal:
            s = jnp.where(k0 + k_iota <= t_col, s, NEG)
        shift = scol * (k0 - t0).astype(F32)
        m_old = m_ref[...]
        m_new = jnp.maximum(m_old, jnp.max(s, axis=-1, keepdims=True) + shift)
        alpha = jnp.exp2(m_old - m_new)
        p = jnp.exp2(s - (m_new - shift))
        acc_ref[...] = alpha * acc_ref[...] + _dot(p.astype(BF16), va_ref[pl.ds(k0, tk), :])
        m_ref[...] = m_new

    n_below = lax.div(t0, tk)
    n_first = jnp.minimum(jnp.maximum(lax.div(t0 - reach_ref[pl.program_id(1)], tk), 0), n_below)
    s_ref[...] = scores(n_first)

    def body(kt, carry):
        s_cur = s_ref[...]
        s_next = scores(kt + 1)
        softmax_pv(kt, s_cur, False)
        s_ref[...] = s_next
        return carry

    lax.fori_loop(n_first, n_below, body, 0)
    softmax_pv(n_below, s_ref[...], True)
    gates = jax.nn.sigmoid(gate_ref[0])
    acc = acc_ref[...]
    o = acc[:, :HEAD_DIM] / acc[:, HEAD_DIM:HEAD_DIM + 1]
    for r in range(GQA_REP):
        gate = gates[:, GQA_REP + r:GQA_REP + r + 1]
        o_ref[:, r * HEAD_DIM:(r + 1) * HEAD_DIM] = (o[r * qb:(r + 1) * qb] * gate).astype(o_ref.dtype)


def _slope_pieces(slopes):
    s_hi = slopes.astype(BF16)
    s_mid = (slopes - s_hi.astype(F32)).astype(BF16)
    s_lo = (slopes - s_hi.astype(F32) - s_mid.astype(F32)).astype(BF16)
    return jnp.stack([s_hi, s_mid, s_lo, s_hi, s_mid, s_lo], axis=1)


def _slope_rows(pieces, first_lane):
    rows = jnp.pad(pieces, ((0, 0), (first_lane, LANES - first_lane - pieces.shape[1]))).reshape(N_KV, GQA_REP, LANES)
    return jnp.pad(rows, ((0, 0), (0, 8 - GQA_REP), (0, 0)))


def _alibi_reach(slopes, gain_q, gain_k):
    bound = jnp.max(jnp.abs(gain_q)) * jnp.max(jnp.abs(gain_k)) * (HEAD_DIM ** 0.5 * LOG2E * 1.01)
    min_slope = jnp.min(slopes.reshape(N_KV, GQA_REP), axis=1)
    reach = jnp.ceil((2.0 * bound + EXP2_UNDERFLOW) / min_slope)
    return jnp.minimum(reach, 2.0 ** 30).astype(jnp.int32)


def _slc_attention(slopes, reach, proj, selbias, gates_r, *, b, t, k_col, v_col):
    n = b * t
    qb = SLC_Q_BLOCK
    nqb = t // qb
    rows = GQA_REP * qb
    nb = SLC_KEY_TILE // SLC_BLOCK
    rowblk = lambda bi, gi, c: bi * nqb + c
    srow = _slope_rows(_slope_pieces(slopes), nb)
    scol = jnp.repeat(slopes.reshape(N_KV, GQA_REP), qb, axis=1).reshape(N_KV, rows, 1)
    return pl.pallas_call(
        _slc_attn_kernel,
        grid=(b, N_KV, nqb),
        in_specs=[
            pl.BlockSpec(memory_space=pltpu.SMEM),
            pl.BlockSpec((qb, GQA_REP * HEAD_DIM), lambda bi, gi, c: (rowblk(bi, gi, c), gi)),
            pl.BlockSpec((t, HEAD_DIM), lambda bi, gi, c: (bi, k_col + gi)),
            pl.BlockSpec((t, HEAD_DIM), lambda bi, gi, c: (bi, v_col + gi)),
            pl.BlockSpec((1, qb, SLC_LANES), lambda bi, gi, c: (gi, rowblk(bi, gi, c), 0)),
            pl.BlockSpec((1, qb, LANES), lambda bi, gi, c: (gi, rowblk(bi, gi, c), 0)),
            pl.BlockSpec((1, 8, LANES), lambda bi, gi, c: (gi, 0, 0)),
            pl.BlockSpec((1, rows, 1), lambda bi, gi, c: (gi, 0, 0)),
        ],
        out_specs=pl.BlockSpec((qb, GQA_REP * HEAD_DIM), lambda bi, gi, c: (rowblk(bi, gi, c), gi)),
        out_shape=jax.ShapeDtypeStruct((n, N_HEADS * HEAD_DIM), BF16),
        scratch_shapes=[
            pltpu.VMEM((rows, 2 * HEAD_DIM), BF16),
            pltpu.VMEM((t, 2 * HEAD_DIM), BF16),
            pltpu.VMEM((t, 2 * HEAD_DIM), BF16),
            pltpu.VMEM((rows, 1), F32),
            pltpu.VMEM((rows, 2 * HEAD_DIM), F32),
            pltpu.VMEM((rows, SLC_KEY_TILE), F32),
        ],
        compiler_params=_params(("parallel", "parallel", "arbitrary")),
        name="nsa_slc_attn",
    )(reach, proj, proj, proj, selbias, gates_r, srow, scol)


def _sb_attn_kernel(q_ref, k_ref, v_ref, o_ref, acc_ref, carry_ref):
    ts = SB_TILE
    c = pl.program_id(2)
    acc_ref[...] = jnp.zeros(acc_ref.shape, F32)
    carry_ref[...] = jnp.zeros(carry_ref.shape, F32)
    r_idx = lax.broadcasted_iota(jnp.int32, (ts, 1), 0)
    c_idx = lax.broadcasted_iota(jnp.int32, (1, ts), 1)
    tri = (r_idx > c_idx).astype(BF16)
    diag = c_idx < r_idx

    def tile(jt, valid):
        k0 = pl.multiple_of(jt * ts, ts)
        heads = range(SB_HEADS_PER_STEP)
        cols = [slice(e * HEAD_DIM, (e + 1) * HEAD_DIM) for e in heads]
        carries = [carry_ref[e] for e in heads]
        zs = [_dot_nt(q_ref[:, cols[e]], k_ref[pl.ds(k0, ts), cols[e]]) for e in heads]
        lfs = []
        for e in heads:
            lf = -(jnp.maximum(zs[e], 0.0) + jnp.log2(1.0 + jnp.exp2(-jnp.abs(zs[e]))))
            lfs.append(lf if valid is None else jnp.where(valid, lf, 0.0))
        afters = []
        for e in heads:
            hi = lfs[e].astype(BF16)
            lo = (lfs[e] - hi.astype(F32)).astype(BF16)
            afters.append(_dot(hi, tri) + _dot(lo, tri))
        avs = []
        for e in heads:
            a = jnp.exp2(zs[e] + lfs[e] + afters[e] + carries[e])
            if valid is not None:
                a = jnp.where(valid, a, 0.0)
            avs.append(_dot(a.astype(BF16), v_ref[pl.ds(k0, ts), cols[e]]))
        for e in heads:
            acc_ref[e] += avs[e]
            carry_ref[e] = carries[e] + jnp.sum(lfs[e], axis=-1, keepdims=True)

    def live():
        return (jnp.max(carry_ref[...]) > SB_UNDERFLOW * LOG2E).astype(jnp.int32)

    tile(c, diag)

    def cond(state):
        jt, go = state
        return (jt >= 0) & (go > 0)

    def body(state):
        jt, _ = state
        tile(jt, None)
        return jt - 1, live()

    lax.while_loop(cond, body, (c - 1, live()))
    for e in range(SB_HEADS_PER_STEP):
        o_ref[:, e * HEAD_DIM:(e + 1) * HEAD_DIM] = acc_ref[e].astype(o_ref.dtype)


def _sb_attention(proj, *, b, t):
    n = b * t
    ts = SB_TILE
    nq = t // ts
    hs = SB_HEADS_PER_STEP
    width = hs * HEAD_DIM
    nhp = N_HEADS // hs
    return pl.pallas_call(
        _sb_attn_kernel,
        grid=(b, nhp, nq),
        in_specs=[
            pl.BlockSpec((ts, width), lambda bi, h, c: (bi * nq + c, h)),
            pl.BlockSpec((t, width), lambda bi, h, c: (bi, nhp + h)),
            pl.BlockSpec((t, width), lambda bi, h, c: (bi, 2 * nhp + h)),
        ],
        out_specs=pl.BlockSpec((ts, width), lambda bi, h, c: (bi * nq + c, h)),
        out_shape=jax.ShapeDtypeStruct((n, N_HEADS * HEAD_DIM), BF16),
        scratch_shapes=[pltpu.VMEM((hs, ts, HEAD_DIM), F32), pltpu.VMEM((hs, ts, 1), F32)],
        compiler_params=_params(("parallel", "parallel", "arbitrary")),
        name="sb_attn",
    )(proj, proj, proj)


def _row_copy(src_hbm, dst_ref, sem, src_row, dst_row):
    return pltpu.make_async_copy(src_hbm.at[pl.ds(src_row, 1), :], dst_ref.at[pl.ds(dst_row, 1), :], sem)


def _row_gather_start(src_hbm, dst_ref, sem, idx_ref, base, count):
    def body(r, carry):
        _row_copy(src_hbm, dst_ref, sem, idx_ref[base + r], r).start()
        return carry

    lax.fori_loop(0, count, body, 0, unroll=8)


def _row_gather_wait(src_hbm, dst_ref, sem):
    pltpu.make_async_copy(src_hbm.at[pl.ds(0, dst_ref.shape[0]), :], dst_ref, sem).wait()


def _expert_kernel(src_ref, te_ref, nu_ref, x_hbm, g_ref, wg_ref, wu_ref, wd_ref, o_ref, xbuf, wgb, wub, wdb, sem):
    tm = MOE_TILE
    i = pl.program_id(0)
    nt = pl.num_programs(0)
    n_used = nu_ref[0]
    slot = i % MOE_SLOTS
    nxt = (i + 2) % MOE_SLOTS
    next_base = jnp.where(i + 2 < nt, i + 2, i + 2 - nt) * tm

    @pl.when((i < n_used) & ((i == 0) | (te_ref[i] != te_ref[jnp.maximum(i - 1, 0)])))
    def _():
        wgb[...] = wg_ref[0, 0].astype(BF16)
        wub[...] = wu_ref[0, 0].astype(BF16)
        wdb[...] = wd_ref[0, 0].astype(BF16)

    @pl.when(i == 0)
    def _():
        _row_gather_start(x_hbm, xbuf.at[0], sem.at[0], src_ref, 0, tm)
        _row_gather_start(x_hbm, xbuf.at[1], sem.at[1], src_ref, tm, tm)

    _row_gather_wait(x_hbm, xbuf.at[slot], sem.at[slot])

    @pl.when(i < n_used)
    def _():
        x = xbuf[slot]
        ms = jnp.mean(x * x, axis=-1, keepdims=True)
        h = (x * lax.rsqrt(ms + EPS) * g_ref[...]).astype(BF16)
        for r in range(tm):
            _row_copy(x_hbm, xbuf.at[nxt], sem.at[nxt], src_ref[next_base + r], r).start()
        gate = _dot(h, wgb[...])
        up = _dot(h, wub[...])
        act = (gate * jax.nn.sigmoid(gate) * up).astype(BF16)
        o_ref[...] = _dot(act, wdb[...])

    @pl.when(i >= n_used)
    def _():
        _row_gather_start(x_hbm, xbuf.at[nxt], sem.at[nxt], src_ref, next_base, tm)
        o_ref[...] = jnp.zeros(o_ref.shape, o_ref.dtype)

    @pl.when(i == nt - 1)
    def _():
        prev = (i + 1) % MOE_SLOTS
        _row_gather_wait(x_hbm, xbuf.at[prev], sem.at[prev])
        _row_gather_wait(x_hbm, xbuf.at[nxt], sem.at[nxt])


def _experts(src_tok, tile_expert, n_used, x, g, wg, wu, wd, layer):
    n, d = x.shape
    tm = MOE_TILE
    p_rows = src_tok.shape[0]
    de = wg.shape[3]
    grid_spec = pltpu.PrefetchScalarGridSpec(
        num_scalar_prefetch=3,
        grid=(p_rows // tm,),
        in_specs=[
            pl.BlockSpec(memory_space=pl.ANY),
            pl.BlockSpec((1, d), lambda i, s, te, nu: (0, 0)),
            pl.BlockSpec((1, 1, d, de), lambda i, s, te, nu: (layer, te[i], 0, 0)),
            pl.BlockSpec((1, 1, d, de), lambda i, s, te, nu: (layer, te[i], 0, 0)),
            pl.BlockSpec((1, 1, de, d), lambda i, s, te, nu: (layer, te[i], 0, 0)),
        ],
        out_specs=pl.BlockSpec((tm, d), lambda i, s, te, nu: (i, 0)),
        scratch_shapes=[
            pltpu.VMEM((MOE_SLOTS, tm, d), F32),
            pltpu.VMEM((d, de), BF16),
            pltpu.VMEM((d, de), BF16),
            pltpu.VMEM((de, d), BF16),
            pltpu.SemaphoreType.DMA((MOE_SLOTS,)),
        ],
    )
    return pl.pallas_call(
        _expert_kernel,
        grid_spec=grid_spec,
        out_shape=jax.ShapeDtypeStruct((p_rows, d), F32),
        compiler_params=_params(("arbitrary",)),
        name="moe_experts",
    )(src_tok, tile_expert, n_used, x, g.reshape(1, d), wg, wu, wd)


def _combine_kernel(pos_ref, x_ref, rt_ref, y_hbm, o_ref, ybuf, sem):
    tm = x_ref.shape[0]
    i = pl.program_id(0)
    nt = pl.num_programs(0)

    def start(tile, slot):
        for s in range(2):
            def body(r, carry):
                _row_copy(y_hbm, ybuf.at[slot, s], sem.at[slot], pos_ref[2 * (tile * tm + r) + s], r).start()
                return carry

            lax.fori_loop(0, tm, body, 0, unroll=8)

    @pl.when(i == 0)
    def _():
        start(0, 0)

    @pl.when(i + 1 < nt)
    def _():
        start(i + 1, (i + 1) % 2)

    slot = i % 2
    for s in range(2):
        _row_gather_wait(y_hbm, ybuf.at[slot, s], sem.at[slot])
    rt = rt_ref[...]
    o_ref[...] = x_ref[...] + rt[:, 2:3] * ybuf[slot, 0] + rt[:, 3:4] * ybuf[slot, 1]


def _combine(pos, x, routing, ys, *, tm):
    n, d = x.shape
    grid_spec = pltpu.PrefetchScalarGridSpec(
        num_scalar_prefetch=1,
        grid=(n // tm,),
        in_specs=[
            pl.BlockSpec((tm, d), lambda i, p: (i, 0)),
            pl.BlockSpec((tm, LANES), lambda i, p: (i, 0)),
            pl.BlockSpec(memory_space=pl.ANY),
        ],
        out_specs=pl.BlockSpec((tm, d), lambda i, p: (i, 0)),
        scratch_shapes=[pltpu.VMEM((2, 2, tm, d), F32), pltpu.SemaphoreType.DMA((2,))],
    )
    return pl.pallas_call(
        _combine_kernel,
        grid_spec=grid_spec,
        out_shape=jax.ShapeDtypeStruct((n, d), F32),
        compiler_params=_params(("arbitrary",)),
        name="moe_combine",
    )(pos, x, routing, ys)


def _hier_moe(x, routing, g, w_gate, w_up, w_down, layer):
    n, d = x.shape
    tm = MOE_TILE

    e_flat = routing[:, :2].astype(jnp.int32).reshape(-1)
    onehot = (e_flat[:, None] == jnp.arange(N_EXPERTS, dtype=jnp.int32)[None, :]).astype(jnp.int32)
    counts = jnp.sum(onehot, axis=0)
    rank = jnp.sum((jnp.cumsum(onehot, axis=0) - onehot) * onehot, axis=1)
    padded = ((counts + tm - 1) // tm) * tm
    ends = jnp.cumsum(padded)
    pos = (ends - padded)[e_flat] + rank
    p_rows = 2 * n + N_EXPERTS * tm
    src_tok = jnp.zeros((p_rows,), jnp.int32).at[pos].set(jnp.arange(2 * n, dtype=jnp.int32) // 2)
    tile_start = jnp.arange(p_rows // tm, dtype=jnp.int32) * tm
    tile_expert = jnp.minimum(jnp.sum((ends[None, :] <= tile_start[:, None]).astype(jnp.int32), axis=1), N_EXPERTS - 1)
    n_used = (ends[-1] // tm).astype(jnp.int32).reshape(1)

    ys = _experts(src_tok, tile_expert, n_used, x, g, w_gate, w_up, w_down, layer)
    return _combine(pos.astype(jnp.int32), x, routing, ys, tm=256 if n % 256 == 0 else n)


def _alibi_slopes():
    return jnp.exp2(-8.0 * jnp.arange(1, N_HEADS + 1, dtype=F32) / N_HEADS)


def _cmp_to_slc(n_cmp_pad, n_slc):
    cs = np.arange(n_cmp_pad) * CMP_STRIDE
    ss = np.arange(SLC_LANES) * SLC_BLOCK
    lo = np.maximum(cs[:, None], ss[None, :])
    hi = np.minimum(cs[:, None] + CMP_BLOCK, ss[None, :] + SLC_BLOCK)
    m = np.maximum(hi - lo, 0).astype(np.float32) / CMP_BLOCK
    m[:, n_slc:] = 0.0
    return m


def _nsa_mixer(x, g_norm, w_in, qk_gain, cmp_pe, cmp_w1, cmp_w2, w_out, router, *, b, t):
    n, d = x.shape
    q_dim = N_HEADS * HEAD_DIM
    kv_dim = N_KV * HEAD_DIM
    scale = HEAD_DIM ** -0.5 * LOG2E
    main = q_dim + 6 * kv_dim
    ones = jnp.ones((kv_dim,), F32)
    tile4 = lambda v: jnp.tile(v, N_KV)
    cgain = jnp.concatenate([jnp.tile(qk_gain[0], N_HEADS) * scale, ones, ones, tile4(qk_gain[2]), ones, tile4(qk_gain[3]), ones])
    tiles_per = lambda cols: cols // PROJ_COLS
    q_tiles = tiles_per(q_dim)
    kv_tiles = tiles_per(kv_dim)
    norm_tiles = list(range(q_tiles)) + [q_tiles + 2 * kv_tiles + j for j in range(kv_tiles)] + [q_tiles + 4 * kv_tiles + j for j in range(kv_tiles)]
    n_gate = 3 * N_HEADS
    w_gate = jnp.pad(w_in[:, main:], ((0, 0), (0, LANES - n_gate))).astype(BF16)
    cmp_tiles = [q_tiles + j for j in range(2 * kv_tiles)]
    proj, gate_logits, kv_cmp = _norm_matmul(x, g_norm, w_in[:, :main].astype(BF16), cgain, norm_tiles, extra_w=w_gate, f32_tiles=cmp_tiles)
    gr = gate_logits[:, :n_gate].reshape(n, 3, N_KV, GQA_REP).transpose(2, 0, 1, 3).reshape(N_KV, n, 3 * GQA_REP)
    gates_r = jnp.pad(gr, ((0, 0), (0, 0), (0, LANES - 3 * GQA_REP)))

    nc = t // CMP_STRIDE
    half = CMP_BLOCK // 2
    w1cat = jnp.concatenate([cmp_w1[:, :half], cmp_w1[:, half:]], axis=-1)
    pe_rows = jnp.pad(jnp.stack([cmp_pe[:, :half], cmp_pe[:, half:]], axis=2), ((0, 0), (0, 0), (0, 6), (0, 0)))
    cmp_gain = jnp.stack([qk_gain[1], jnp.ones((HEAD_DIM,), F32)]).reshape(2, 1, HEAD_DIM)
    cmp_kv = _compress(kv_cmp, w1cat.astype(BF16), pe_rows.astype(BF16), cmp_w2.astype(BF16), cmp_gain, b=b, t=t)

    slopes = _alibi_slopes() * LOG2E
    pieces = _slope_pieces(slopes)
    pad_flag = jnp.full((N_HEADS, 1), NEG, F32).astype(BF16)
    srow = _slope_rows(jnp.concatenate([pieces, pad_flag], axis=1), 0)
    n_slc = t // SLC_BLOCK
    msel = jnp.asarray(_cmp_to_slc(nc, n_slc), BF16)
    q_blocks = q_dim // HEAD_DIM
    o_c, selbias = _cmp_attention(proj, cmp_kv, gates_r, msel, srow, b=b, t=t, top_k=min(SLC_TOPK, n_slc))
    reach = _alibi_reach(slopes, qk_gain[0], qk_gain[2])
    o_s = _slc_attention(slopes, reach, proj, selbias, gates_r, b=b, t=t, k_col=q_blocks + 2 * N_KV, v_col=q_blocks + 3 * N_KV)
    o_w = _win_attention(proj, gates_r, srow, b=b, t=t, k_col=q_blocks + 4 * N_KV, v_col=q_blocks + 5 * N_KV)
    return _proj_residual([o_c, o_s, o_w], w_out.astype(BF16), x, router)


def _sb_mixer(x, g_norm, w_in, w_out, router, *, b, t):
    q_dim = N_HEADS * HEAD_DIM
    scale = HEAD_DIM ** -0.5 * LOG2E
    cgain = jnp.concatenate([jnp.full((q_dim,), scale, F32), jnp.ones((2 * q_dim,), F32)])
    proj = _norm_matmul(x, g_norm, w_in.astype(BF16), cgain, ())
    o = _sb_attention(proj, b=b, t=t)
    return _proj_residual([o], w_out.astype(BF16), x, router)


def kernel(x, norm_mix, norm_ffn, nsa_w_in, nsa_qk_gain, nsa_cmp_pe, nsa_cmp_w1, nsa_cmp_w2, nsa_w_out, sb_w_in, sb_w_out, moe_w_group, moe_b_group, moe_w_router, moe_b_router, moe_w_gate, moe_w_up, moe_w_down):
    b, t, d = x.shape
    depth = norm_mix.shape[0]
    xf = x.reshape(b * t, d)
    for i in range(depth):
        j = i // 2
        router = _router_params(norm_ffn[i], moe_w_group[i], moe_b_group[i], moe_w_router[i], moe_b_router[i])
        if i % 2 == 0:
            xf, routing = _nsa_mixer(xf, norm_mix[i], nsa_w_in[j], nsa_qk_gain[j], nsa_cmp_pe[j], nsa_cmp_w1[j], nsa_cmp_w2[j], nsa_w_out[j], router, b=b, t=t)
        else:
            xf, routing = _sb_mixer(xf, norm_mix[i], sb_w_in[j], sb_w_out[j], router, b=b, t=t)
        xf = _hier_moe(xf, routing, norm_ffn[i], moe_w_gate, moe_w_up, moe_w_down, i)
    return xf.reshape(b, t, d)
```

```python
import functools

import jax
import jax.numpy as jnp
import numpy as np
from jax import lax
from jax.experimental import pallas as pl
from jax.experimental.pallas import tpu as pltpu

F32 = jnp.float32
BF16 = jnp.bfloat16

HEAD_DIM = 128
N_HEADS = 16
N_KV = 4
GQA_REP = N_HEADS // N_KV
CMP_BLOCK = 32
CMP_STRIDE = 16
SLC_BLOCK = 64
SLC_TOPK = 16
WINDOW = 512
Q_BLOCK = 256
N_GROUPS = 4
EXPERTS_PER_GROUP = 8
N_EXPERTS = N_GROUPS * EXPERTS_PER_GROUP
EPS = 1e-6
NEG = -1e30
LANES = 128
VMEM_LIMIT = 56 * 1024 * 1024

PROJ_ROWS = 512
PROJ_COLS = 512
CMP_Q_BLOCK = 256
SLC_Q_BLOCK = 256
SLC_LANES = 128
SB_TILE = 256
SB_HEADS_PER_STEP = 8
SLC_KEY_TILE = 1024
LOG2E = 1.4426950408889634
SB_UNDERFLOW = -105.0
MOE_TILE = 256
MOE_SLOTS = 3
EXP2_UNDERFLOW = 152.0
AUG_PAD_LANE = 6


def _dot(a, b):
    return jnp.dot(a, b, preferred_element_type=F32)


def _dot_nt(a, b):
    return lax.dot_general(a, b, (((1,), (1,)), ((), ())), preferred_element_type=F32)


def _params(sem):
    return pltpu.CompilerParams(dimension_semantics=sem, vmem_limit_bytes=VMEM_LIMIT)


def _resident(shape):
    return pl.BlockSpec(shape, lambda *_: (0,) * len(shape), pipeline_mode=pl.Buffered(1))


def _norm_matmul_kernel(x_ref, g_ref, w_ref, cg_ref, *rest, norm_tiles, f32_tiles, has_extra):
    if has_extra:
        wx_ref, o_ref, ox_ref, of_ref = rest
    else:
        (o_ref,) = rest
    x = x_ref[...]
    ms = jnp.mean(x * x, axis=-1, keepdims=True)
    h = (x * lax.rsqrt(ms + EPS) * g_ref[...]).astype(BF16)
    tn = PROJ_COLS
    for j in range(w_ref.shape[1] // tn):
        y = _dot(h, w_ref[:, j * tn:(j + 1) * tn])
        if j in f32_tiles:
            k = f32_tiles.index(j)
            of_ref[:, k * tn:(k + 1) * tn] = y
        for hh in range(tn // LANES):
            yh = y[:, hh * LANES:(hh + 1) * LANES]
            if j in norm_tiles:
                yh = yh * lax.rsqrt(jnp.mean(yh * yh, axis=-1, keepdims=True) + EPS)
            sl = slice(j * tn + hh * LANES, j * tn + (hh + 1) * LANES)
            o_ref[:, sl] = (yh * cg_ref[:, sl]).astype(o_ref.dtype)
    if has_extra:
        ox_ref[...] = _dot(h, wx_ref[...])


def _norm_matmul(x, g, w, cgain, norm_tiles, extra_w=None, f32_tiles=()):
    n, d = x.shape
    m = w.shape[1]
    tm = PROJ_ROWS
    has_extra = extra_w is not None
    in_specs = [pl.BlockSpec((tm, d), lambda i: (i, 0)), _resident((1, d)), _resident((d, m)), _resident((1, m))]
    out_specs = [pl.BlockSpec((tm, m), lambda i: (i, 0))]
    out_shape = [jax.ShapeDtypeStruct((n, m), BF16)]
    args = [x, g.reshape(1, d), w, cgain.reshape(1, m)]
    if has_extra:
        mx = extra_w.shape[1]
        mf = len(f32_tiles) * PROJ_COLS
        in_specs.append(_resident((d, mx)))
        out_specs += [pl.BlockSpec((tm, mx), lambda i: (i, 0)), pl.BlockSpec((tm, mf), lambda i: (i, 0))]
        out_shape += [jax.ShapeDtypeStruct((n, mx), F32), jax.ShapeDtypeStruct((n, mf), F32)]
        args.append(extra_w)
    out = pl.pallas_call(
        functools.partial(_norm_matmul_kernel, norm_tiles=tuple(norm_tiles), f32_tiles=tuple(f32_tiles), has_extra=has_extra),
        grid=(n // tm,),
        in_specs=in_specs,
        out_specs=out_specs,
        out_shape=out_shape,
        compiler_params=_params(("parallel",)),
        name="norm_matmul",
    )(*args)
    return out if has_extra else out[0]


def _first_max(vals, lane):
    m = jnp.max(vals, axis=-1, keepdims=True)
    idx = jnp.min(jnp.where(vals == m, lane, LANES), axis=-1, keepdims=True)
    return m, idx


def _route(x, g_ref, w_ref, b_ref):
    ms = jnp.mean(x * x, axis=-1, keepdims=True)
    h = x * lax.rsqrt(ms + EPS) * g_ref[...]
    h_hi = h.astype(BF16)
    h_lo = (h - h_hi.astype(F32)).astype(BF16)
    logits = _dot(h_hi, w_ref[0]) + _dot(h_hi, w_ref[1]) + _dot(h_lo, w_ref[0]) + b_ref[...]
    lane = lax.broadcasted_iota(jnp.int32, (1, LANES), 1)
    gl = jnp.where(lane < N_GROUPS, logits, NEG)
    ge = jnp.exp(gl - jnp.max(gl, axis=-1, keepdims=True))
    pg_all = ge / jnp.sum(ge, axis=-1, keepdims=True)
    pg, gsel = _first_max(jnp.where(lane < N_GROUPS, pg_all, -1.0), lane)
    lo = N_GROUPS + gsel * EXPERTS_PER_GROUP
    in_group = (lane >= lo) & (lane < lo + EXPERTS_PER_GROUP)
    el = jnp.where(in_group, logits, NEG)
    ee = jnp.exp(el - jnp.max(el, axis=-1, keepdims=True))
    pe_all = jnp.where(in_group, ee / jnp.sum(ee, axis=-1, keepdims=True), -1.0)
    p1, i1 = _first_max(pe_all, lane)
    p2, i2 = _first_max(jnp.where(lane == i1, -1.0, pe_all), lane)
    denom = p1 + p2
    w1 = pg * p1 / denom
    w2 = pg * p2 / denom
    e1 = (i1 - N_GROUPS).astype(F32)
    e2 = (i2 - N_GROUPS).astype(F32)
    return jnp.where(lane == 0, e1, jnp.where(lane == 1, e2, jnp.where(lane == 2, w1, jnp.where(lane == 3, w2, 0.0))))


def _proj_residual_kernel(*refs, n_in):
    a_refs = refs[:n_in]
    w_ref, r_ref, g_ref, wr_ref, br_ref, o_ref, rt_ref = refs[n_in:]
    if n_in == 1:
        a = a_refs[0][...]
    else:
        acc = a_refs[0][...].astype(F32)
        for a_ref in a_refs[1:]:
            acc = acc + a_ref[...].astype(F32)
        a = acc.astype(BF16)
    tn = PROJ_COLS
    for j in range(w_ref.shape[1] // tn):
        sl = slice(j * tn, (j + 1) * tn)
        o_ref[:, sl] = r_ref[:, sl] + _dot(a, w_ref[:, sl])
    rt_ref[...] = _route(o_ref[...], g_ref, wr_ref, br_ref)


def _proj_residual(branches, w, res, router):
    n, k = branches[0].shape
    m = w.shape[1]
    tm = PROJ_ROWS
    n_in = len(branches)
    g_ffn, w_split, b_r = router
    return pl.pallas_call(
        functools.partial(_proj_residual_kernel, n_in=n_in),
        grid=(n // tm,),
        in_specs=[pl.BlockSpec((tm, k), lambda i: (i, 0)) for _ in range(n_in)]
        + [_resident((k, m)), pl.BlockSpec((tm, m), lambda i: (i, 0)), _resident((1, m)), _resident(w_split.shape), _resident((1, LANES))],
        out_specs=[pl.BlockSpec((tm, m), lambda i: (i, 0)), pl.BlockSpec((tm, LANES), lambda i: (i, 0))],
        out_shape=[jax.ShapeDtypeStruct((n, m), F32), jax.ShapeDtypeStruct((n, LANES), F32)],
        compiler_params=_params(("parallel",)),
        name="proj_residual",
    )(*branches, w, res, g_ffn.reshape(1, m), w_split, b_r)


def _router_params(g_ffn, w_group, b_group, w_router, b_router):
    n_logit = N_GROUPS + N_EXPERTS
    w_r = jnp.pad(jnp.concatenate([w_group, w_router], axis=1), ((0, 0), (0, LANES - n_logit)))
    b_r = jnp.pad(jnp.concatenate([b_group, b_router]), (0, LANES - n_logit)).reshape(1, LANES)
    w_hi = w_r.astype(BF16)
    return g_ffn, jnp.stack([w_hi, (w_r - w_hi.astype(F32)).astype(BF16)]), b_r


def _cmp_kernel(kv_ref, w1_ref, pe_ref, w2_ref, gain_ref, o_ref):
    nc = kv_ref.shape[0] // CMP_STRIDE
    p = jnp.zeros((nc, 2 * HEAD_DIM), F32)
    pb = jnp.zeros((8, 2 * HEAD_DIM), F32)
    for l in range(CMP_STRIDE):
        w1 = w1_ref[0, l]
        p = p + _dot(kv_ref[pl.ds(l, nc, stride=CMP_STRIDE), :].astype(BF16), w1)
        pb = pb + _dot(pe_ref[0, l], w1)
    bias = pb[0:1, :HEAD_DIM] + pb[1:2, HEAD_DIM:]
    second = pltpu.roll(p[:, HEAD_DIM:], shift=nc - 1, axis=0)
    hid = jax.nn.gelu(p[:, :HEAD_DIM] + second + bias)
    c = _dot(hid.astype(BF16), w2_ref[0])
    nf = (pl.program_id(1) == 0).astype(F32)
    ms = jnp.mean(c * c, axis=-1, keepdims=True)
    fac = nf * lax.rsqrt(ms + EPS) + (1.0 - nf)
    o_ref[0, 0, 0] = (c * fac * gain_ref[0]).astype(o_ref.dtype)


def _compress(kv, w1cat, pe_rows, w2, gains, *, b, t):
    g, cs, hd = N_KV, CMP_STRIDE, HEAD_DIM
    nc = t // cs
    return pl.pallas_call(
        _cmp_kernel,
        grid=(b, 2, g),
        in_specs=[
            pl.BlockSpec((t, hd), lambda bi, k, gi: (bi, k * g + gi)),
            pl.BlockSpec((1, cs, hd, 2 * HEAD_DIM), lambda bi, k, gi: (k, 0, 0, 0)),
            pl.BlockSpec((1, cs, 8, hd), lambda bi, k, gi: (k, 0, 0, 0)),
            pl.BlockSpec((1, HEAD_DIM, HEAD_DIM), lambda bi, k, gi: (k, 0, 0)),
            pl.BlockSpec((1, 1, HEAD_DIM), lambda bi, k, gi: (k, 0, 0)),
        ],
        out_specs=pl.BlockSpec((1, 1, 1, nc, HEAD_DIM), lambda bi, k, gi: (bi, k, gi, 0, 0)),
        out_shape=jax.ShapeDtypeStruct((b, 2, g, nc, HEAD_DIM), BF16),
        compiler_params=_params(("parallel", "parallel", "parallel")),
        name="nsa_compress",
    )(kv, w1cat, pe_rows, w2, gains)


def _stack_heads(qa_ref, q_ref, srow_ref, qb):
    for r in range(GQA_REP):
        qa_ref[r * qb:(r + 1) * qb, :HEAD_DIM] = q_ref[:, r * HEAD_DIM:(r + 1) * HEAD_DIM]
        qa_ref[r * qb:(r + 1) * qb, HEAD_DIM:] = jnp.broadcast_to(srow_ref[0, r:r + 1, :], (qb, LANES))


def _cmp_attn_kernel(q_ref, kc_ref, vc_ref, gate_ref, msel_ref, srow_ref, oc_ref, sb_ref, qa_ref, ka_ref, va_ref, *, top_k):
    c = pl.program_id(2)
    qb = CMP_Q_BLOCK
    t0 = c * qb
    rows = GQA_REP * qb
    nc = ka_ref.shape[0]

    @pl.when(c == 0)
    def _():
        n_idx = lax.broadcasted_iota(jnp.int32, (nc, LANES), 0)
        lane = lax.broadcasted_iota(jnp.int32, (nc, LANES), 1)
        aug = jnp.where(lane < 3, ((n_idx >> 7) << 11).astype(F32), jnp.where(lane < 6, ((n_idx & 127) << 4).astype(F32), 0.0))
        ka_ref[:, :HEAD_DIM] = kc_ref[0, 0, 0]
        ka_ref[:, HEAD_DIM:] = aug.astype(BF16)
        va_ref[:, :HEAD_DIM] = vc_ref[0, 0, 0]
        va_ref[:, HEAD_DIM:] = (lane == 0).astype(BF16)

    _stack_heads(qa_ref, q_ref, srow_ref, qb)
    t_col = t0 + (lax.broadcasted_iota(jnp.int32, (rows, 1), 0) & (qb - 1))
    t_row = t0 + lax.broadcasted_iota(jnp.int32, (1, qb), 1)
    gates = jax.nn.sigmoid(gate_ref[0])

    def attend(k):
        ncol = k * LANES
        nrow = ncol * CMP_STRIDE // SLC_BLOCK
        s = _dot_nt(qa_ref[...], ka_ref[:ncol, :])
        cend = lax.broadcasted_iota(jnp.int32, (1, ncol), 1) * CMP_STRIDE + (CMP_BLOCK - 1)
        s = jnp.where(cend <= t_col, s, NEG)
        m = jnp.max(s, axis=-1, keepdims=True)
        e = jnp.exp2(s - m)
        acc = _dot(e.astype(BF16), va_ref[:ncol, :])
        inv = jnp.where(m > 0.5 * NEG, 1.0 / acc[:, HEAD_DIM:HEAD_DIM + 1], 0.0)
        psum = jnp.zeros((qb, ncol), F32)
        for r in range(GQA_REP):
            rs = slice(r * qb, (r + 1) * qb)
            psum = psum + e[rs] * inv[rs]
            oc_ref[:, r * HEAD_DIM:(r + 1) * HEAD_DIM] = (acc[rs, :HEAD_DIM] * (inv[rs] * gates[:, r:r + 1])).astype(oc_ref.dtype)

        msel = msel_ref[:ncol, :]
        p_hi = psum.astype(BF16)
        rem = psum - p_hi.astype(F32)
        p_mid = rem.astype(BF16)
        p_lo = (rem - p_mid.astype(F32)).astype(BF16)
        imp = _dot(p_hi, msel) + _dot(p_mid, msel) + _dot(p_lo, msel)

        imp_t = imp.T[:nrow]
        j_col = lax.broadcasted_iota(jnp.int32, (nrow, 1), 0)
        forced = (j_col == 0) | (j_col == (t_row >> 6))
        causal = (j_col << 6) <= t_row
        score = jnp.where(forced, -NEG, jnp.where(causal, imp_t, NEG))
        keep = jnp.zeros((nrow, qb), F32)
        for _ in range(top_k):
            mx = jnp.max(score, axis=0, keepdims=True)
            first = jnp.min(jnp.where(score == mx, j_col, SLC_LANES), axis=0, keepdims=True)
            hit = j_col == first
            keep = jnp.where(hit, (mx > 0.5 * NEG).astype(F32), keep)
            score = jnp.where(hit, -3e38, score)
        bias_t = jnp.where(keep > 0.5, 0.0, NEG)
        if nrow < SLC_LANES:
            bias_t = jnp.concatenate([bias_t, jnp.full((SLC_LANES - nrow, qb), NEG, F32)], axis=0)
        sb_ref[0] = bias_t.T.astype(sb_ref.dtype)

    n_variants = nc // LANES
    needed = jnp.minimum(lax.div(lax.div(t0 + qb - CMP_BLOCK, CMP_STRIDE), LANES) + 1, n_variants)
    for k in range(1, n_variants + 1):
        pl.when(needed == k)(functools.partial(attend, k))


def _cmp_attention(proj, cmp_kv, gates_r, msel, srow, *, b, t, top_k):
    n = b * t
    qb = CMP_Q_BLOCK
    nqb = t // qb
    rows = GQA_REP * qb
    nc = cmp_kv.shape[3]
    rowblk = lambda bi, gi, c: bi * nqb + c
    return pl.pallas_call(
        functools.partial(_cmp_attn_kernel, top_k=top_k),
        grid=(b, N_KV, nqb),
        in_specs=[
            pl.BlockSpec((qb, GQA_REP * HEAD_DIM), lambda bi, gi, c: (rowblk(bi, gi, c), gi)),
            pl.BlockSpec((1, 1, 1, nc, HEAD_DIM), lambda bi, gi, c: (bi, 0, gi, 0, 0)),
            pl.BlockSpec((1, 1, 1, nc, HEAD_DIM), lambda bi, gi, c: (bi, 1, gi, 0, 0)),
            pl.BlockSpec((1, qb, LANES), lambda bi, gi, c: (gi, rowblk(bi, gi, c), 0)),
            pl.BlockSpec((nc, SLC_LANES), lambda bi, gi, c: (0, 0)),
            pl.BlockSpec((1, 8, LANES), lambda bi, gi, c: (gi, 0, 0)),
        ],
        out_specs=[
            pl.BlockSpec((qb, GQA_REP * HEAD_DIM), lambda bi, gi, c: (rowblk(bi, gi, c), gi)),
            pl.BlockSpec((1, qb, SLC_LANES), lambda bi, gi, c: (gi, rowblk(bi, gi, c), 0)),
        ],
        out_shape=[
            jax.ShapeDtypeStruct((n, N_HEADS * HEAD_DIM), BF16),
            jax.ShapeDtypeStruct((N_KV, n, SLC_LANES), BF16),
        ],
        scratch_shapes=[
            pltpu.VMEM((rows, 2 * HEAD_DIM), BF16),
            pltpu.VMEM((nc, 2 * HEAD_DIM), BF16),
            pltpu.VMEM((nc, 2 * HEAD_DIM), BF16),
        ],
        compiler_params=_params(("parallel", "parallel", "arbitrary")),
        name="nsa_cmp_attn",
    )(proj, cmp_kv, cmp_kv, gates_r, msel, srow)


def _win_attn_kernel(q_ref, k_ref, v_ref, gate_ref, srow_ref, o_ref, qa_ref, ka_ref, va_ref):
    c = pl.program_id(2)
    t0 = pl.multiple_of(c * Q_BLOCK, Q_BLOCK)
    rows = GQA_REP * Q_BLOCK
    span = Q_BLOCK + WINDOW
    t_len = k_ref.shape[0]

    @pl.when(c == 0)
    def _():
        pos = lax.broadcasted_iota(jnp.int32, (t_len, LANES), 0)
        lane = lax.broadcasted_iota(jnp.int32, (t_len, LANES), 1)
        aug = jnp.where(lane < 3, ((pos >> 6) << 6).astype(F32), jnp.where(lane < 6, (pos & 63).astype(F32), 0.0))
        pad_lane = lax.broadcasted_iota(jnp.int32, (WINDOW, LANES), 1)
        ka_ref[:WINDOW, :HEAD_DIM] = jnp.zeros((WINDOW, HEAD_DIM), BF16)
        ka_ref[:WINDOW, HEAD_DIM:] = (pad_lane == AUG_PAD_LANE).astype(BF16)
        ka_ref[WINDOW:, :HEAD_DIM] = k_ref[...]
        ka_ref[WINDOW:, HEAD_DIM:] = aug.astype(BF16)
        va_ref[:WINDOW, :] = jnp.zeros((WINDOW, 2 * HEAD_DIM), BF16)
        va_ref[WINDOW:, :HEAD_DIM] = v_ref[...]
        va_ref[WINDOW:, HEAD_DIM:] = (lane == 0).astype(BF16)

    _stack_heads(qa_ref, q_ref, srow_ref, Q_BLOCK)
    s = _dot_nt(qa_ref[...], ka_ref[pl.ds(t0, span), :])
    i_col = lax.broadcasted_iota(jnp.int32, (rows, 1), 0) & (Q_BLOCK - 1)
    kk = lax.broadcasted_iota(jnp.int32, (1, Q_BLOCK), 1)
    first = jnp.where(kk > i_col, s[:, :Q_BLOCK], NEG)
    last = jnp.where(kk <= i_col, s[:, WINDOW:], NEG)
    m = jnp.maximum(jnp.max(first, axis=-1, keepdims=True), jnp.max(last, axis=-1, keepdims=True))
    if Q_BLOCK < WINDOW:
        mid = s[:, Q_BLOCK:WINDOW]
        m = jnp.maximum(m, jnp.max(mid, axis=-1, keepdims=True))
    acc = (_dot(jnp.exp2(first - m).astype(BF16), va_ref[pl.ds(t0, Q_BLOCK), :])
           + _dot(jnp.exp2(last - m).astype(BF16), va_ref[pl.ds(t0 + WINDOW, Q_BLOCK), :]))
    if Q_BLOCK < WINDOW:
        acc = acc + _dot(jnp.exp2(mid - m).astype(BF16), va_ref[pl.ds(t0 + Q_BLOCK, WINDOW - Q_BLOCK), :])
    gates = jax.nn.sigmoid(gate_ref[0])
    o = acc[:, :HEAD_DIM] / acc[:, HEAD_DIM:HEAD_DIM + 1]
    for r in range(GQA_REP):
        gate = gates[:, 2 * GQA_REP + r:2 * GQA_REP + r + 1]
        o_ref[:, r * HEAD_DIM:(r + 1) * HEAD_DIM] = (o[r * Q_BLOCK:(r + 1) * Q_BLOCK] * gate).astype(o_ref.dtype)


def _win_attention(proj, gates_r, srow, *, b, t, k_col, v_col):
    n = b * t
    nqb = t // Q_BLOCK
    rows = GQA_REP * Q_BLOCK
    rowblk = lambda bi, gi, c: bi * nqb + c
    return pl.pallas_call(
        _win_attn_kernel,
        grid=(b, N_KV, nqb),
        in_specs=[
            pl.BlockSpec((Q_BLOCK, GQA_REP * HEAD_DIM), lambda bi, gi, c: (rowblk(bi, gi, c), gi)),
            pl.BlockSpec((t, HEAD_DIM), lambda bi, gi, c: (bi, k_col + gi)),
            pl.BlockSpec((t, HEAD_DIM), lambda bi, gi, c: (bi, v_col + gi)),
            pl.BlockSpec((1, Q_BLOCK, LANES), lambda bi, gi, c: (gi, rowblk(bi, gi, c), 0)),
            pl.BlockSpec((1, 8, LANES), lambda bi, gi, c: (gi, 0, 0)),
        ],
        out_specs=pl.BlockSpec((Q_BLOCK, GQA_REP * HEAD_DIM), lambda bi, gi, c: (rowblk(bi, gi, c), gi)),
        out_shape=jax.ShapeDtypeStruct((n, N_HEADS * HEAD_DIM), BF16),
        scratch_shapes=[
            pltpu.VMEM((rows, 2 * HEAD_DIM), BF16),
            pltpu.VMEM((t + WINDOW, 2 * HEAD_DIM), BF16),
            pltpu.VMEM((t + WINDOW, 2 * HEAD_DIM), BF16),
        ],
        compiler_params=_params(("parallel", "parallel", "arbitrary")),
        name="nsa_win_attn",
    )(proj, proj, proj, gates_r, srow)


def _slc_attn_kernel(reach_ref, q_ref, k_ref, v_ref, sb_ref, gate_ref, srow_ref, scol_ref, o_ref, qa_ref, ka_ref, va_ref, m_ref, acc_ref, s_ref):
    c = pl.program_id(2)
    qb = SLC_Q_BLOCK
    t0 = c * qb
    rows = GQA_REP * qb
    tk = SLC_KEY_TILE
    nb = tk // SLC_BLOCK
    t_len = k_ref.shape[0]

    @pl.when(c == 0)
    def _():
        pos = lax.broadcasted_iota(jnp.int32, (t_len, LANES), 0)
        lane = lax.broadcasted_iota(jnp.int32, (t_len, LANES), 1)
        off = pos & (tk - 1)
        blk = (pos >> 6) & (nb - 1)
        aug = jnp.where(lane < nb, (blk == lane).astype(F32),
                        jnp.where(lane < nb + 3, ((off >> 3) << 3).astype(F32),
                                  jnp.where(lane < nb + 6, (off & 7).astype(F32), 0.0)))
        ka_ref[:, :HEAD_DIM] = k_ref[...]
        ka_ref[:, HEAD_DIM:] = aug.astype(BF16)
        va_ref[:, :HEAD_DIM] = v_ref[...]
        va_ref[:, HEAD_DIM:] = (lane == 0).astype(BF16)

    for r in range(GQA_REP):
        qa_ref[r * qb:(r + 1) * qb, :HEAD_DIM] = q_ref[:, r * HEAD_DIM:(r + 1) * HEAD_DIM]
    m_ref[...] = jnp.full(m_ref.shape, NEG, F32)
    acc_ref[...] = jnp.zeros(acc_ref.shape, F32)
    sb = sb_ref[0]
    scol = scol_ref[0]
    t_col = t0 + (lax.broadcasted_iota(jnp.int32, (rows, 1), 0) & (qb - 1))
    k_iota = lax.broadcasted_iota(jnp.int32, (1, tk), 1)
    p_row = lax.broadcasted_iota(jnp.int32, (SLC_LANES, LANES), 0)
    p_col = lax.broadcasted_iota(jnp.int32, (SLC_LANES, LANES), 1)

    def scores(kt):
        k0 = pl.multiple_of(kt * tk, tk)
        pick = ((p_row == kt * nb + p_col) & (p_col < nb)).astype(BF16)
        sbt = _dot(sb, pick).astype(BF16)
        for r in range(GQA_REP):
            qa_ref[r * qb:(r + 1) * qb, HEAD_DIM:] = sbt + srow_ref[0, r:r + 1, :]
        return _dot_nt(qa_ref[...], ka_ref[pl.ds(k0, tk), :])

    def softmax_pv(kt, s, causal):
        k0 = pl.multiple_of(kt * tk, tk)
        width = s.shape[1]
        if causal:
            s = jnp.where(k0 + k_iota[:, :width] <= t_col, s, NEG)
        shift = scol * (k0 - t0).astype(F32)
        m_old = m_ref[...]
        m_new = jnp.maximum(m_old, jnp.max(s, axis=-1, keepdims=True) + shift)
        alpha = jnp.exp2(m_old - m_new)
        p = jnp.exp2(s - (m_new - shift))
        acc_ref[...] = alpha * acc_ref[...] + _dot(p.astype(BF16), va_ref[pl.ds(k0, width), :])
        m_ref[...] = m_new

    n_below = lax.div(t0, tk)
    n_first = jnp.minimum(jnp.maximum(lax.div(t0 - reach_ref[pl.program_id(1)], tk), 0), n_below)
    s_ref[...] = scores(n_first)

    def body(kt, carry):
        s_cur = s_ref[...]
        s_next = scores(kt + 1)
        softmax_pv(kt, s_cur, False)
        s_ref[...] = s_next
        return carry

    lax.fori_loop(n_first, n_below, body, 0)
    groups = lax.div(t0 - n_below * tk, qb) + 1
    for j in range(1, tk // qb + 1):
        @pl.when(groups == j)
        def _(j=j):
            softmax_pv(n_below, s_ref[:, :j * qb], True)

    gates = jax.nn.sigmoid(gate_ref[0])
    acc = acc_ref[...]
    o = acc[:, :HEAD_DIM] / acc[:, HEAD_DIM:HEAD_DIM + 1]
    for r in range(GQA_REP):
        gate = gates[:, GQA_REP + r:GQA_REP + r + 1]
        o_ref[:, r * HEAD_DIM:(r + 1) * HEAD_DIM] = (o[r * qb:(r + 1) * qb] * gate).astype(o_ref.dtype)


def _slope_pieces(slopes):
    s_hi = slopes.astype(BF16)
    s_mid = (slopes - s_hi.astype(F32)).astype(BF16)
    s_lo = (slopes - s_hi.astype(F32) - s_mid.astype(F32)).astype(BF16)
    return jnp.stack([s_hi, s_mid, s_lo, s_hi, s_mid, s_lo], axis=1)


def _slope_rows(pieces, first_lane):
    rows = jnp.pad(pieces, ((0, 0), (first_lane, LANES - first_lane - pieces.shape[1]))).reshape(N_KV, GQA_REP, LANES)
    return jnp.pad(rows, ((0, 0), (0, 8 - GQA_REP), (0, 0)))


def _alibi_reach(slopes, gain_q, gain_k):
    bound = jnp.max(jnp.abs(gain_q)) * jnp.max(jnp.abs(gain_k)) * (HEAD_DIM ** 0.5 * LOG2E * 1.01)
    min_slope = jnp.min(slopes.reshape(N_KV, GQA_REP), axis=1)
    reach = jnp.ceil((2.0 * bound + EXP2_UNDERFLOW) / min_slope)
    return jnp.minimum(reach, 2.0 ** 30).astype(jnp.int32)


def _slc_attention(slopes, reach, proj, selbias, gates_r, *, b, t, k_col, v_col):
    n = b * t
    qb = SLC_Q_BLOCK
    nqb = t // qb
    rows = GQA_REP * qb
    nb = SLC_KEY_TILE // SLC_BLOCK
    rowblk = lambda bi, gi, c: bi * nqb + c
    srow = _slope_rows(_slope_pieces(slopes), nb)
    scol = jnp.repeat(slopes.reshape(N_KV, GQA_REP), qb, axis=1).reshape(N_KV, rows, 1)
    return pl.pallas_call(
        _slc_attn_kernel,
        grid=(b, N_KV, nqb),
        in_specs=[
            pl.BlockSpec(memory_space=pltpu.SMEM),
            pl.BlockSpec((qb, GQA_REP * HEAD_DIM), lambda bi, gi, c: (rowblk(bi, gi, c), gi)),
            pl.BlockSpec((t, HEAD_DIM), lambda bi, gi, c: (bi, k_col + gi)),
            pl.BlockSpec((t, HEAD_DIM), lambda bi, gi, c: (bi, v_col + gi)),
            pl.BlockSpec((1, qb, SLC_LANES), lambda bi, gi, c: (gi, rowblk(bi, gi, c), 0)),
            pl.BlockSpec((1, qb, LANES), lambda bi, gi, c: (gi, rowblk(bi, gi, c), 0)),
            pl.BlockSpec((1, 8, LANES), lambda bi, gi, c: (gi, 0, 0)),
            pl.BlockSpec((1, rows, 1), lambda bi, gi, c: (gi, 0, 0)),
        ],
        out_specs=pl.BlockSpec((qb, GQA_REP * HEAD_DIM), lambda bi, gi, c: (rowblk(bi, gi, c), gi)),
        out_shape=jax.ShapeDtypeStruct((n, N_HEADS * HEAD_DIM), BF16),
        scratch_shapes=[
            pltpu.VMEM((rows, 2 * HEAD_DIM), BF16),
            pltpu.VMEM((t, 2 * HEAD_DIM), BF16),
            pltpu.VMEM((t, 2 * HEAD_DIM), BF16),
            pltpu.VMEM((rows, 1), F32),
            pltpu.VMEM((rows, 2 * HEAD_DIM), F32),
            pltpu.VMEM((rows, SLC_KEY_TILE), F32),
        ],
        compiler_params=_params(("parallel", "parallel", "arbitrary")),
        name="nsa_slc_attn",
    )(reach, proj, proj, proj, selbias, gates_r, srow, scol)


def _sb_attn_kernel(q_ref, k_ref, v_ref, o_ref, acc_ref, carry_ref):
    ts = SB_TILE
    c = pl.program_id(2)
    acc_ref[...] = jnp.zeros(acc_ref.shape, F32)
    carry_ref[...] = jnp.zeros(carry_ref.shape, F32)
    r_idx = lax.broadcasted_iota(jnp.int32, (ts, 1), 0)
    c_idx = lax.broadcasted_iota(jnp.int32, (1, ts), 1)
    tri = (r_idx > c_idx).astype(BF16)
    diag = c_idx < r_idx

    def tile(jt, valid):
        k0 = pl.multiple_of(jt * ts, ts)
        heads = range(SB_HEADS_PER_STEP)
        cols = [slice(e * HEAD_DIM, (e + 1) * HEAD_DIM) for e in heads]
        carries = [carry_ref[e] for e in heads]
        zs = [_dot_nt(q_ref[:, cols[e]], k_ref[pl.ds(k0, ts), cols[e]]) for e in heads]
        lfs = []
        for e in heads:
            lf = -(jnp.maximum(zs[e], 0.0) + jnp.log2(1.0 + jnp.exp2(-jnp.abs(zs[e]))))
            lfs.append(lf if valid is None else jnp.where(valid, lf, 0.0))
        afters = []
        for e in heads:
            hi = lfs[e].astype(BF16)
            lo = (lfs[e] - hi.astype(F32)).astype(BF16)
            afters.append(_dot(hi, tri) + _dot(lo, tri))
        avs = []
        for e in heads:
            a = jnp.exp2(zs[e] + lfs[e] + afters[e] + carries[e])
            if valid is not None:
                a = jnp.where(valid, a, 0.0)
            avs.append(_dot(a.astype(BF16), v_ref[pl.ds(k0, ts), cols[e]]))
        for e in heads:
            acc_ref[e] += avs[e]
            carry_ref[e] = carries[e] + jnp.sum(lfs[e], axis=-1, keepdims=True)

    def live():
        return (jnp.max(carry_ref[...]) > SB_UNDERFLOW * LOG2E).astype(jnp.int32)

    tile(c, diag)

    def cond(state):
        jt, go = state
        return (jt >= 0) & (go > 0)

    def body(state):
        jt, _ = state
        tile(jt, None)
        return jt - 1, live()

    lax.while_loop(cond, body, (c - 1, live()))
    for e in range(SB_HEADS_PER_STEP):
        o_ref[:, e * HEAD_DIM:(e + 1) * HEAD_DIM] = acc_ref[e].astype(o_ref.dtype)


def _sb_attention(proj, *, b, t):
    n = b * t
    ts = SB_TILE
    nq = t // ts
    hs = SB_HEADS_PER_STEP
    width = hs * HEAD_DIM
    nhp = N_HEADS // hs
    return pl.pallas_call(
        _sb_attn_kernel,
        grid=(b, nhp, nq),
        in_specs=[
            pl.BlockSpec((ts, width), lambda bi, h, c: (bi * nq + c, h)),
            pl.BlockSpec((t, width), lambda bi, h, c: (bi, nhp + h), pipeline_mode=pl.Buffered(1)),
            pl.BlockSpec((t, width), lambda bi, h, c: (bi, 2 * nhp + h), pipeline_mode=pl.Buffered(1)),
        ],
        out_specs=pl.BlockSpec((ts, width), lambda bi, h, c: (bi * nq + c, h)),
        out_shape=jax.ShapeDtypeStruct((n, N_HEADS * HEAD_DIM), BF16),
        scratch_shapes=[pltpu.VMEM((hs, ts, HEAD_DIM), F32), pltpu.VMEM((hs, ts, 1), F32)],
        compiler_params=_params(("parallel", "parallel", "arbitrary")),
        name="sb_attn",
    )(proj, proj, proj)


def _row_copy(src_hbm, dst_ref, sem, src_row, dst_row):
    return pltpu.make_async_copy(src_hbm.at[pl.ds(src_row, 1), :], dst_ref.at[pl.ds(dst_row, 1), :], sem)


def _row_gather_start(src_hbm, dst_ref, sem, idx_ref, base, count):
    def body(r, carry):
        _row_copy(src_hbm, dst_ref, sem, idx_ref[base + r], r).start()
        return carry

    lax.fori_loop(0, count, body, 0, unroll=8)


def _row_gather_wait(src_hbm, dst_ref, sem):
    pltpu.make_async_copy(src_hbm.at[pl.ds(0, dst_ref.shape[0]), :], dst_ref, sem).wait()


def _expert_kernel(src_ref, te_ref, nu_ref, x_hbm, g_ref, wg_ref, wu_ref, wd_ref, o_ref, xbuf, wgb, wub, wdb, sem):
    tm = MOE_TILE
    i = pl.program_id(0)
    nt = pl.num_programs(0)
    n_used = nu_ref[0]
    slot = i % MOE_SLOTS
    nxt = (i + 2) % MOE_SLOTS
    next_base = jnp.where(i + 2 < nt, i + 2, i + 2 - nt) * tm

    @pl.when((i < n_used) & ((i == 0) | (te_ref[i] != te_ref[jnp.maximum(i - 1, 0)])))
    def _():
        wgb[...] = wg_ref[0, 0].astype(BF16)
        wub[...] = wu_ref[0, 0].astype(BF16)
        wdb[...] = wd_ref[0, 0].astype(BF16)

    @pl.when(i == 0)
    def _():
        _row_gather_start(x_hbm, xbuf.at[0], sem.at[0], src_ref, 0, tm)
        _row_gather_start(x_hbm, xbuf.at[1], sem.at[1], src_ref, tm, tm)

    @pl.when(i < n_used)
    def _():
        _row_gather_wait(x_hbm, xbuf.at[slot], sem.at[slot])
        x = xbuf[slot]
        ms = jnp.mean(x * x, axis=-1, keepdims=True)
        h = (x * lax.rsqrt(ms + EPS) * g_ref[...]).astype(BF16)
        for r in range(tm):
            _row_copy(x_hbm, xbuf.at[nxt], sem.at[nxt], src_ref[next_base + r], r).start()
        gate = _dot(h, wgb[...])
        up = _dot(h, wub[...])
        act = (gate * jax.nn.sigmoid(gate) * up).astype(BF16)
        o_ref[...] = _dot(act, wdb[...])

    @pl.when(i >= n_used)
    def _():
        o_ref[...] = jnp.zeros(o_ref.shape, o_ref.dtype)

    @pl.when(i == n_used - 1)
    def _():
        prev = (i + 1) % MOE_SLOTS
        _row_gather_wait(x_hbm, xbuf.at[prev], sem.at[prev])
        _row_gather_wait(x_hbm, xbuf.at[nxt], sem.at[nxt])


def _experts(src_tok, tile_expert, n_used, x, g, wg, wu, wd, layer):
    n, d = x.shape
    tm = MOE_TILE
    p_rows = src_tok.shape[0]
    de = wg.shape[3]
    grid_spec = pltpu.PrefetchScalarGridSpec(
        num_scalar_prefetch=3,
        grid=(p_rows // tm,),
        in_specs=[
            pl.BlockSpec(memory_space=pl.ANY),
            pl.BlockSpec((1, d), lambda i, s, te, nu: (0, 0)),
            pl.BlockSpec((1, 1, d, de), lambda i, s, te, nu: (layer, te[i], 0, 0)),
            pl.BlockSpec((1, 1, d, de), lambda i, s, te, nu: (layer, te[i], 0, 0)),
            pl.BlockSpec((1, 1, de, d), lambda i, s, te, nu: (layer, te[i], 0, 0)),
        ],
        out_specs=pl.BlockSpec((tm, d), lambda i, s, te, nu: (i, 0)),
        scratch_shapes=[
            pltpu.VMEM((MOE_SLOTS, tm, d), F32),
            pltpu.VMEM((d, de), BF16),
            pltpu.VMEM((d, de), BF16),
            pltpu.VMEM((de, d), BF16),
            pltpu.SemaphoreType.DMA((MOE_SLOTS,)),
        ],
    )
    return pl.pallas_call(
        _expert_kernel,
        grid_spec=grid_spec,
        out_shape=jax.ShapeDtypeStruct((p_rows, d), F32),
        compiler_params=_params(("arbitrary",)),
        name="moe_experts",
    )(src_tok, tile_expert, n_used, x, g.reshape(1, d), wg, wu, wd)


def _combine_kernel(pos_ref, x_ref, rt_ref, y_hbm, o_ref, ybuf, sem):
    tm = x_ref.shape[0]
    i = pl.program_id(0)
    nt = pl.num_programs(0)

    def start(tile, slot):
        for s in range(2):
            def body(r, carry):
                _row_copy(y_hbm, ybuf.at[slot, s], sem.at[slot], pos_ref[2 * (tile * tm + r) + s], r).start()
                return carry

            lax.fori_loop(0, tm, body, 0, unroll=8)

    @pl.when(i == 0)
    def _():
        start(0, 0)

    @pl.when(i + 1 < nt)
    def _():
        start(i + 1, (i + 1) % 2)

    slot = i % 2
    for s in range(2):
        _row_gather_wait(y_hbm, ybuf.at[slot, s], sem.at[slot])
    rt = rt_ref[...]
    o_ref[...] = x_ref[...] + rt[:, 2:3] * ybuf[slot, 0] + rt[:, 3:4] * ybuf[slot, 1]


def _combine(pos, x, routing, ys, *, tm):
    n, d = x.shape
    grid_spec = pltpu.PrefetchScalarGridSpec(
        num_scalar_prefetch=1,
        grid=(n // tm,),
        in_specs=[
            pl.BlockSpec((tm, d), lambda i, p: (i, 0)),
            pl.BlockSpec((tm, LANES), lambda i, p: (i, 0)),
            pl.BlockSpec(memory_space=pl.ANY),
        ],
        out_specs=pl.BlockSpec((tm, d), lambda i, p: (i, 0)),
        scratch_shapes=[pltpu.VMEM((2, 2, tm, d), F32), pltpu.SemaphoreType.DMA((2,))],
    )
    return pl.pallas_call(
        _combine_kernel,
        grid_spec=grid_spec,
        out_shape=jax.ShapeDtypeStruct((n, d), F32),
        compiler_params=_params(("arbitrary",)),
        name="moe_combine",
    )(pos, x, routing, ys)


def _hier_moe(x, routing, g, w_gate, w_up, w_down, layer):
    n, d = x.shape
    tm = MOE_TILE

    e_flat = routing[:, :2].astype(jnp.int32).reshape(-1)
    onehot = (e_flat[:, None] == jnp.arange(N_EXPERTS, dtype=jnp.int32)[None, :]).astype(jnp.int32)
    counts = jnp.sum(onehot, axis=0)
    rank = jnp.sum((jnp.cumsum(onehot, axis=0) - onehot) * onehot, axis=1)
    padded = ((counts + tm - 1) // tm) * tm
    ends = jnp.cumsum(padded)
    pos = (ends - padded)[e_flat] + rank
    p_rows = 2 * n + N_EXPERTS * tm
    src_tok = jnp.zeros((p_rows,), jnp.int32).at[pos].set(jnp.arange(2 * n, dtype=jnp.int32) // 2)
    tile_start = jnp.arange(p_rows // tm, dtype=jnp.int32) * tm
    tile_expert = jnp.minimum(jnp.sum((ends[None, :] <= tile_start[:, None]).astype(jnp.int32), axis=1), N_EXPERTS - 1)
    n_used = (ends[-1] // tm).astype(jnp.int32).reshape(1)

    ys = _experts(src_tok, tile_expert, n_used, x, g, w_gate, w_up, w_down, layer)
    return _combine(pos.astype(jnp.int32), x, routing, ys, tm=256 if n % 256 == 0 else n)


def _alibi_slopes():
    return jnp.exp2(-8.0 * jnp.arange(1, N_HEADS + 1, dtype=F32) / N_HEADS)


def _cmp_to_slc(n_cmp_pad, n_slc):
    cs = np.arange(n_cmp_pad) * CMP_STRIDE
    ss = np.arange(SLC_LANES) * SLC_BLOCK
    lo = np.maximum(cs[:, None], ss[None, :])
    hi = np.minimum(cs[:, None] + CMP_BLOCK, ss[None, :] + SLC_BLOCK)
    m = np.maximum(hi - lo, 0).astype(np.float32) / CMP_BLOCK
    m[:, n_slc:] = 0.0
    return m


def _nsa_mixer(x, g_norm, w_in, qk_gain, cmp_pe, cmp_w1, cmp_w2, w_out, router, *, b, t):
    n, d = x.shape
    q_dim = N_HEADS * HEAD_DIM
    kv_dim = N_KV * HEAD_DIM
    scale = HEAD_DIM ** -0.5 * LOG2E
    main = q_dim + 6 * kv_dim
    ones = jnp.ones((kv_dim,), F32)
    tile4 = lambda v: jnp.tile(v, N_KV)
    cgain = jnp.concatenate([jnp.tile(qk_gain[0], N_HEADS) * scale, ones, ones, tile4(qk_gain[2]), ones, tile4(qk_gain[3]), ones])
    tiles_per = lambda cols: cols // PROJ_COLS
    q_tiles = tiles_per(q_dim)
    kv_tiles = tiles_per(kv_dim)
    norm_tiles = list(range(q_tiles)) + [q_tiles + 2 * kv_tiles + j for j in range(kv_tiles)] + [q_tiles + 4 * kv_tiles + j for j in range(kv_tiles)]
    n_gate = 3 * N_HEADS
    w_gate = jnp.pad(w_in[:, main:], ((0, 0), (0, LANES - n_gate))).astype(BF16)
    cmp_tiles = [q_tiles + j for j in range(2 * kv_tiles)]
    proj, gate_logits, kv_cmp = _norm_matmul(x, g_norm, w_in[:, :main].astype(BF16), cgain, norm_tiles, extra_w=w_gate, f32_tiles=cmp_tiles)
    gr = gate_logits[:, :n_gate].reshape(n, 3, N_KV, GQA_REP).transpose(2, 0, 1, 3).reshape(N_KV, n, 3 * GQA_REP)
    gates_r = jnp.pad(gr, ((0, 0), (0, 0), (0, LANES - 3 * GQA_REP)))

    nc = t // CMP_STRIDE
    half = CMP_BLOCK // 2
    w1cat = jnp.concatenate([cmp_w1[:, :half], cmp_w1[:, half:]], axis=-1)
    pe_rows = jnp.pad(jnp.stack([cmp_pe[:, :half], cmp_pe[:, half:]], axis=2), ((0, 0), (0, 0), (0, 6), (0, 0)))
    cmp_gain = jnp.stack([qk_gain[1], jnp.ones((HEAD_DIM,), F32)]).reshape(2, 1, HEAD_DIM)
    cmp_kv = _compress(kv_cmp, w1cat.astype(BF16), pe_rows.astype(BF16), cmp_w2.astype(BF16), cmp_gain, b=b, t=t)

    slopes = _alibi_slopes() * LOG2E
    pieces = _slope_pieces(slopes)
    pad_flag = jnp.full((N_HEADS, 1), NEG, F32).astype(BF16)
    srow = _slope_rows(jnp.concatenate([pieces, pad_flag], axis=1), 0)
    n_slc = t // SLC_BLOCK
    msel = jnp.asarray(_cmp_to_slc(nc, n_slc), BF16)
    q_blocks = q_dim // HEAD_DIM
    o_c, selbias = _cmp_attention(proj, cmp_kv, gates_r, msel, srow, b=b, t=t, top_k=min(SLC_TOPK, n_slc))
    reach = _alibi_reach(slopes, qk_gain[0], qk_gain[2])
    o_s = _slc_attention(slopes, reach, proj, selbias, gates_r, b=b, t=t, k_col=q_blocks + 2 * N_KV, v_col=q_blocks + 3 * N_KV)
    o_w = _win_attention(proj, gates_r, srow, b=b, t=t, k_col=q_blocks + 4 * N_KV, v_col=q_blocks + 5 * N_KV)
    return _proj_residual([o_c, o_s, o_w], w_out.astype(BF16), x, router)


def _sb_mixer(x, g_norm, w_in, w_out, router, *, b, t):
    q_dim = N_HEADS * HEAD_DIM
    scale = HEAD_DIM ** -0.5 * LOG2E
    cgain = jnp.concatenate([jnp.full((q_dim,), scale, F32), jnp.ones((2 * q_dim,), F32)])
    proj = _norm_matmul(x, g_norm, w_in.astype(BF16), cgain, ())
    o = _sb_attention(proj, b=b, t=t)
    return _proj_residual([o], w_out.astype(BF16), x, router)


def kernel(x, norm_mix, norm_ffn, nsa_w_in, nsa_qk_gain, nsa_cmp_pe, nsa_cmp_w1, nsa_cmp_w2, nsa_w_out, sb_w_in, sb_w_out, moe_w_group, moe_b_group, moe_w_router, moe_b_router, moe_w_gate, moe_w_up, moe_w_down):
    b, t, d = x.shape
    depth = norm_mix.shape[0]
    xf = x.reshape(b * t, d)
    for i in range(depth):
        j = i // 2
        router = _router_params(norm_ffn[i], moe_w_group[i], moe_b_group[i], moe_w_router[i], moe_b_router[i])
        if i % 2 == 0:
            xf, routing = _nsa_mixer(xf, norm_mix[i], nsa_w_in[j], nsa_qk_gain[j], nsa_cmp_pe[j], nsa_cmp_w1[j], nsa_cmp_w2[j], nsa_w_out[j], router, b=b, t=t)
        else:
            xf, routing = _sb_mixer(xf, norm_mix[i], sb_w_in[j], sb_w_out[j], router, b=b, t=t)
        xf = _hier_moe(xf, routing, norm_ffn[i], moe_w_gate, moe_w_up, moe_w_down, i)
    return xf.reshape(b, t, d)
```

```python
import functools

import jax
import jax.numpy as jnp
import numpy as np
from jax import lax
from jax.experimental import pallas as pl
from jax.experimental.pallas import tpu as pltpu

F32 = jnp.float32
BF16 = jnp.bfloat16

HEAD_DIM = 128
N_HEADS = 16
N_KV = 4
GQA_REP = N_HEADS // N_KV
CMP_BLOCK = 32
CMP_STRIDE = 16
SLC_BLOCK = 64
SLC_TOPK = 16
WINDOW = 512
Q_BLOCK = 256
N_GROUPS = 4
EXPERTS_PER_GROUP = 8
N_EXPERTS = N_GROUPS * EXPERTS_PER_GROUP
EPS = 1e-6
NEG = -1e30
LANES = 128
VMEM_LIMIT = 56 * 1024 * 1024

PROJ_ROWS = 512
PROJ_COLS = 512
CMP_Q_BLOCK = 256
SLC_Q_BLOCK = 256
SLC_LANES = 128
SB_TILE = 256
SB_HEADS_PER_STEP = 8
SLC_KEY_TILE = 1024
LOG2E = 1.4426950408889634
SB_UNDERFLOW = -105.0
MOE_TILE = 256
MOE_SLOTS = 3
EXP2_UNDERFLOW = 152.0
AUG_PAD_LANE = 6


def _dot(a, b):
    return jnp.dot(a, b, preferred_element_type=F32)


def _dot_nt(a, b):
    return lax.dot_general(a, b, (((1,), (1,)), ((), ())), preferred_element_type=F32)


def _params(sem):
    return pltpu.CompilerParams(dimension_semantics=sem, vmem_limit_bytes=VMEM_LIMIT)


def _resident(shape):
    return pl.BlockSpec(shape, lambda *_: (0,) * len(shape), pipeline_mode=pl.Buffered(1))


def _norm_matmul_kernel(x_ref, g_ref, w_ref, cg_ref, *rest, norm_tiles, f32_tiles, has_extra):
    if has_extra:
        wx_ref, o_ref, ox_ref, of_ref = rest
    else:
        (o_ref,) = rest
    x = x_ref[...]
    ms = jnp.mean(x * x, axis=-1, keepdims=True)
    h = (x * lax.rsqrt(ms + EPS) * g_ref[...]).astype(BF16)
    tn = PROJ_COLS
    for j in range(w_ref.shape[1] // tn):
        y = _dot(h, w_ref[:, j * tn:(j + 1) * tn])
        if j in f32_tiles:
            k = f32_tiles.index(j)
            of_ref[:, k * tn:(k + 1) * tn] = y
        for hh in range(tn // LANES):
            yh = y[:, hh * LANES:(hh + 1) * LANES]
            if j in norm_tiles:
                yh = yh * lax.rsqrt(jnp.mean(yh * yh, axis=-1, keepdims=True) + EPS)
            sl = slice(j * tn + hh * LANES, j * tn + (hh + 1) * LANES)
            o_ref[:, sl] = (yh * cg_ref[:, sl]).astype(o_ref.dtype)
    if has_extra:
        ox_ref[...] = _dot(h, wx_ref[...])


def _norm_matmul(x, g, w, cgain, norm_tiles, extra_w=None, f32_tiles=()):
    n, d = x.shape
    m = w.shape[1]
    tm = PROJ_ROWS
    has_extra = extra_w is not None
    in_specs = [pl.BlockSpec((tm, d), lambda i: (i, 0)), _resident((1, d)), _resident((d, m)), _resident((1, m))]
    out_specs = [pl.BlockSpec((tm, m), lambda i: (i, 0))]
    out_shape = [jax.ShapeDtypeStruct((n, m), BF16)]
    args = [x, g.reshape(1, d), w, cgain.reshape(1, m)]
    if has_extra:
        mx = extra_w.shape[1]
        mf = len(f32_tiles) * PROJ_COLS
        in_specs.append(_resident((d, mx)))
        out_specs += [pl.BlockSpec((tm, mx), lambda i: (i, 0)), pl.BlockSpec((tm, mf), lambda i: (i, 0))]
        out_shape += [jax.ShapeDtypeStruct((n, mx), F32), jax.ShapeDtypeStruct((n, mf), F32)]
        args.append(extra_w)
    out = pl.pallas_call(
        functools.partial(_norm_matmul_kernel, norm_tiles=tuple(norm_tiles), f32_tiles=tuple(f32_tiles), has_extra=has_extra),
        grid=(n // tm,),
        in_specs=in_specs,
        out_specs=out_specs,
        out_shape=out_shape,
        compiler_params=_params(("parallel",)),
        name="norm_matmul",
    )(*args)
    return out if has_extra else out[0]


def _first_max(vals, lane):
    m = jnp.max(vals, axis=-1, keepdims=True)
    idx = jnp.min(jnp.where(vals == m, lane, LANES), axis=-1, keepdims=True)
    return m, idx


def _route(x, g_ref, w_ref, b_ref):
    ms = jnp.mean(x * x, axis=-1, keepdims=True)
    h = x * lax.rsqrt(ms + EPS) * g_ref[...]
    h_hi = h.astype(BF16)
    h_lo = (h - h_hi.astype(F32)).astype(BF16)
    logits = _dot(h_hi, w_ref[0]) + _dot(h_hi, w_ref[1]) + _dot(h_lo, w_ref[0]) + b_ref[...]
    lane = lax.broadcasted_iota(jnp.int32, (1, LANES), 1)
    gl = jnp.where(lane < N_GROUPS, logits, NEG)
    ge = jnp.exp(gl - jnp.max(gl, axis=-1, keepdims=True))
    pg_all = ge / jnp.sum(ge, axis=-1, keepdims=True)
    pg, gsel = _first_max(jnp.where(lane < N_GROUPS, pg_all, -1.0), lane)
    lo = N_GROUPS + gsel * EXPERTS_PER_GROUP
    in_group = (lane >= lo) & (lane < lo + EXPERTS_PER_GROUP)
    el = jnp.where(in_group, logits, NEG)
    ee = jnp.exp(el - jnp.max(el, axis=-1, keepdims=True))
    pe_all = jnp.where(in_group, ee / jnp.sum(ee, axis=-1, keepdims=True), -1.0)
    p1, i1 = _first_max(pe_all, lane)
    p2, i2 = _first_max(jnp.where(lane == i1, -1.0, pe_all), lane)
    denom = p1 + p2
    w1 = pg * p1 / denom
    w2 = pg * p2 / denom
    e1 = (i1 - N_GROUPS).astype(F32)
    e2 = (i2 - N_GROUPS).astype(F32)
    return jnp.where(lane == 0, e1, jnp.where(lane == 1, e2, jnp.where(lane == 2, w1, jnp.where(lane == 3, w2, 0.0))))


def _proj_residual_kernel(*refs, n_in):
    a_refs = refs[:n_in]
    w_ref, r_ref, g_ref, wr_ref, br_ref, o_ref, rt_ref = refs[n_in:]
    if n_in == 1:
        a = a_refs[0][...]
    else:
        acc = a_refs[0][...].astype(F32)
        for a_ref in a_refs[1:]:
            acc = acc + a_ref[...].astype(F32)
        a = acc.astype(BF16)
    tn = PROJ_COLS
    for j in range(w_ref.shape[1] // tn):
        sl = slice(j * tn, (j + 1) * tn)
        o_ref[:, sl] = r_ref[:, sl] + _dot(a, w_ref[:, sl])
    rt_ref[...] = _route(o_ref[...], g_ref, wr_ref, br_ref)


def _proj_residual(branches, w, res, router):
    n, k = branches[0].shape
    m = w.shape[1]
    tm = PROJ_ROWS
    n_in = len(branches)
    g_ffn, w_split, b_r = router
    return pl.pallas_call(
        functools.partial(_proj_residual_kernel, n_in=n_in),
        grid=(n // tm,),
        in_specs=[pl.BlockSpec((tm, k), lambda i: (i, 0)) for _ in range(n_in)]
        + [_resident((k, m)), pl.BlockSpec((tm, m), lambda i: (i, 0)), _resident((1, m)), _resident(w_split.shape), _resident((1, LANES))],
        out_specs=[pl.BlockSpec((tm, m), lambda i: (i, 0)), pl.BlockSpec((tm, LANES), lambda i: (i, 0))],
        out_shape=[jax.ShapeDtypeStruct((n, m), F32), jax.ShapeDtypeStruct((n, LANES), F32)],
        compiler_params=_params(("parallel",)),
        name="proj_residual",
    )(*branches, w, res, g_ffn.reshape(1, m), w_split, b_r)


def _router_params(g_ffn, w_group, b_group, w_router, b_router):
    n_logit = N_GROUPS + N_EXPERTS
    w_r = jnp.pad(jnp.concatenate([w_group, w_router], axis=1), ((0, 0), (0, LANES - n_logit)))
    b_r = jnp.pad(jnp.concatenate([b_group, b_router]), (0, LANES - n_logit)).reshape(1, LANES)
    w_hi = w_r.astype(BF16)
    return g_ffn, jnp.stack([w_hi, (w_r - w_hi.astype(F32)).astype(BF16)]), b_r


def _cmp_kernel(kv_ref, w1_ref, pe_ref, w2_ref, gain_ref, o_ref):
    nc = kv_ref.shape[0] // CMP_STRIDE
    p = jnp.zeros((nc, 2 * HEAD_DIM), F32)
    pb = jnp.zeros((8, 2 * HEAD_DIM), F32)
    for l in range(CMP_STRIDE):
        w1 = w1_ref[0, l]
        p = p + _dot(kv_ref[pl.ds(l, nc, stride=CMP_STRIDE), :].astype(BF16), w1)
        pb = pb + _dot(pe_ref[0, l], w1)
    bias = pb[0:1, :HEAD_DIM] + pb[1:2, HEAD_DIM:]
    second = pltpu.roll(p[:, HEAD_DIM:], shift=nc - 1, axis=0)
    hid = jax.nn.gelu(p[:, :HEAD_DIM] + second + bias)
    c = _dot(hid.astype(BF16), w2_ref[0])
    nf = (pl.program_id(1) == 0).astype(F32)
    ms = jnp.mean(c * c, axis=-1, keepdims=True)
    fac = nf * lax.rsqrt(ms + EPS) + (1.0 - nf)
    o_ref[0, 0, 0] = (c * fac * gain_ref[0]).astype(o_ref.dtype)


def _compress(kv, w1cat, pe_rows, w2, gains, *, b, t):
    g, cs, hd = N_KV, CMP_STRIDE, HEAD_DIM
    nc = t // cs
    return pl.pallas_call(
        _cmp_kernel,
        grid=(b, 2, g),
        in_specs=[
            pl.BlockSpec((t, hd), lambda bi, k, gi: (bi, k * g + gi)),
            pl.BlockSpec((1, cs, hd, 2 * HEAD_DIM), lambda bi, k, gi: (k, 0, 0, 0)),
            pl.BlockSpec((1, cs, 8, hd), lambda bi, k, gi: (k, 0, 0, 0)),
            pl.BlockSpec((1, HEAD_DIM, HEAD_DIM), lambda bi, k, gi: (k, 0, 0)),
            pl.BlockSpec((1, 1, HEAD_DIM), lambda bi, k, gi: (k, 0, 0)),
        ],
        out_specs=pl.BlockSpec((1, 1, 1, nc, HEAD_DIM), lambda bi, k, gi: (bi, k, gi, 0, 0)),
        out_shape=jax.ShapeDtypeStruct((b, 2, g, nc, HEAD_DIM), BF16),
        compiler_params=_params(("parallel", "parallel", "parallel")),
        name="nsa_compress",
    )(kv, w1cat, pe_rows, w2, gains)


def _stack_heads(qa_ref, q_ref, srow_ref, qb):
    for r in range(GQA_REP):
        qa_ref[r * qb:(r + 1) * qb, :HEAD_DIM] = q_ref[:, r * HEAD_DIM:(r + 1) * HEAD_DIM]
        qa_ref[r * qb:(r + 1) * qb, HEAD_DIM:] = jnp.broadcast_to(srow_ref[0, r:r + 1, :], (qb, LANES))


def _cmp_attn_kernel(q_ref, kc_ref, vc_ref, gate_ref, msel_ref, srow_ref, oc_ref, sb_ref, qa_ref, ka_ref, va_ref, *, top_k):
    c = pl.program_id(2)
    qb = CMP_Q_BLOCK
    t0 = c * qb
    rows = GQA_REP * qb
    nc = ka_ref.shape[0]

    @pl.when(c == 0)
    def _():
        n_idx = lax.broadcasted_iota(jnp.int32, (nc, LANES), 0)
        lane = lax.broadcasted_iota(jnp.int32, (nc, LANES), 1)
        aug = jnp.where(lane < 3, ((n_idx >> 7) << 11).astype(F32), jnp.where(lane < 6, ((n_idx & 127) << 4).astype(F32), 0.0))
        ka_ref[:, :HEAD_DIM] = kc_ref[0, 0, 0]
        ka_ref[:, HEAD_DIM:] = aug.astype(BF16)
        va_ref[:, :HEAD_DIM] = vc_ref[0, 0, 0]
        va_ref[:, HEAD_DIM:] = (lane == 0).astype(BF16)

    _stack_heads(qa_ref, q_ref, srow_ref, qb)
    t_col = t0 + (lax.broadcasted_iota(jnp.int32, (rows, 1), 0) & (qb - 1))
    t_row = t0 + lax.broadcasted_iota(jnp.int32, (1, qb), 1)
    gates = jax.nn.sigmoid(gate_ref[0])

    def attend(k):
        ncol = k * LANES
        nrow = ncol * CMP_STRIDE // SLC_BLOCK
        s = _dot_nt(qa_ref[...], ka_ref[:ncol, :])
        cend = lax.broadcasted_iota(jnp.int32, (1, ncol), 1) * CMP_STRIDE + (CMP_BLOCK - 1)
        s = jnp.where(cend <= t_col, s, NEG)
        m = jnp.max(s, axis=-1, keepdims=True)
        e = jnp.exp2(s - m)
        acc = _dot(e.astype(BF16), va_ref[:ncol, :])
        inv = jnp.where(m > 0.5 * NEG, 1.0 / acc[:, HEAD_DIM:HEAD_DIM + 1], 0.0)
        psum = jnp.zeros((qb, ncol), F32)
        for r in range(GQA_REP):
            rs = slice(r * qb, (r + 1) * qb)
            psum = psum + e[rs] * inv[rs]
            oc_ref[:, r * HEAD_DIM:(r + 1) * HEAD_DIM] = (acc[rs, :HEAD_DIM] * (inv[rs] * gates[:, r:r + 1])).astype(oc_ref.dtype)

        msel = msel_ref[:ncol, :]
        p_hi = psum.astype(BF16)
        rem = psum - p_hi.astype(F32)
        p_mid = rem.astype(BF16)
        p_lo = (rem - p_mid.astype(F32)).astype(BF16)
        imp = _dot(p_hi, msel) + _dot(p_mid, msel) + _dot(p_lo, msel)

        imp_t = imp.T[:nrow]
        j_col = lax.broadcasted_iota(jnp.int32, (nrow, 1), 0)
        forced = (j_col == 0) | (j_col == (t_row >> 6))
        causal = (j_col << 6) <= t_row
        score = jnp.where(forced, -NEG, jnp.where(causal, imp_t, NEG))
        keep = jnp.zeros((nrow, qb), F32)
        for _ in range(top_k):
            mx = jnp.max(score, axis=0, keepdims=True)
            first = jnp.min(jnp.where(score == mx, j_col, SLC_LANES), axis=0, keepdims=True)
            hit = j_col == first
            keep = jnp.where(hit, (mx > 0.5 * NEG).astype(F32), keep)
            score = jnp.where(hit, -3e38, score)
        bias_t = jnp.where(keep > 0.5, 0.0, NEG)
        if nrow < SLC_LANES:
            bias_t = jnp.concatenate([bias_t, jnp.full((SLC_LANES - nrow, qb), NEG, F32)], axis=0)
        sb_ref[0] = bias_t.T.astype(sb_ref.dtype)

    n_variants = nc // LANES
    needed = jnp.minimum(lax.div(lax.div(t0 + qb - CMP_BLOCK, CMP_STRIDE), LANES) + 1, n_variants)
    for k in range(1, n_variants + 1):
        pl.when(needed == k)(functools.partial(attend, k))


def _cmp_attention(proj, cmp_kv, gates_r, msel, srow, *, b, t, top_k):
    n = b * t
    qb = CMP_Q_BLOCK
    nqb = t // qb
    rows = GQA_REP * qb
    nc = cmp_kv.shape[3]
    rowblk = lambda bi, gi, c: bi * nqb + c
    return pl.pallas_call(
        functools.partial(_cmp_attn_kernel, top_k=top_k),
        grid=(b, N_KV, nqb),
        in_specs=[
            pl.BlockSpec((qb, GQA_REP * HEAD_DIM), lambda bi, gi, c: (rowblk(bi, gi, c), gi)),
            pl.BlockSpec((1, 1, 1, nc, HEAD_DIM), lambda bi, gi, c: (bi, 0, gi, 0, 0)),
            pl.BlockSpec((1, 1, 1, nc, HEAD_DIM), lambda bi, gi, c: (bi, 1, gi, 0, 0)),
            pl.BlockSpec((1, qb, LANES), lambda bi, gi, c: (gi, rowblk(bi, gi, c), 0)),
            pl.BlockSpec((nc, SLC_LANES), lambda bi, gi, c: (0, 0)),
            pl.BlockSpec((1, 8, LANES), lambda bi, gi, c: (gi, 0, 0)),
        ],
        out_specs=[
            pl.BlockSpec((qb, GQA_REP * HEAD_DIM), lambda bi, gi, c: (rowblk(bi, gi, c), gi)),
            pl.BlockSpec((1, qb, SLC_LANES), lambda bi, gi, c: (gi, rowblk(bi, gi, c), 0)),
        ],
        out_shape=[
            jax.ShapeDtypeStruct((n, N_HEADS * HEAD_DIM), BF16),
            jax.ShapeDtypeStruct((N_KV, n, SLC_LANES), BF16),
        ],
        scratch_shapes=[
            pltpu.VMEM((rows, 2 * HEAD_DIM), BF16),
            pltpu.VMEM((nc, 2 * HEAD_DIM), BF16),
            pltpu.VMEM((nc, 2 * HEAD_DIM), BF16),
        ],
        compiler_params=_params(("parallel", "parallel", "arbitrary")),
        name="nsa_cmp_attn",
    )(proj, cmp_kv, cmp_kv, gates_r, msel, srow)


def _win_attn_kernel(q_ref, k_ref, v_ref, gate_ref, srow_ref, o_ref, qa_ref, ka_ref, va_ref):
    c = pl.program_id(2)
    t0 = pl.multiple_of(c * Q_BLOCK, Q_BLOCK)
    rows = GQA_REP * Q_BLOCK
    span = Q_BLOCK + WINDOW
    t_len = k_ref.shape[0]

    @pl.when(c == 0)
    def _():
        pos = lax.broadcasted_iota(jnp.int32, (t_len, LANES), 0)
        lane = lax.broadcasted_iota(jnp.int32, (t_len, LANES), 1)
        aug = jnp.where(lane < 3, ((pos >> 6) << 6).astype(F32), jnp.where(lane < 6, (pos & 63).astype(F32), 0.0))
        pad_lane = lax.broadcasted_iota(jnp.int32, (WINDOW, LANES), 1)
        ka_ref[:WINDOW, :HEAD_DIM] = jnp.zeros((WINDOW, HEAD_DIM), BF16)
        ka_ref[:WINDOW, HEAD_DIM:] = (pad_lane == AUG_PAD_LANE).astype(BF16)
        ka_ref[WINDOW:, :HEAD_DIM] = k_ref[...]
        ka_ref[WINDOW:, HEAD_DIM:] = aug.astype(BF16)
        va_ref[:WINDOW, :] = jnp.zeros((WINDOW, 2 * HEAD_DIM), BF16)
        va_ref[WINDOW:, :HEAD_DIM] = v_ref[...]
        va_ref[WINDOW:, HEAD_DIM:] = (lane == 0).astype(BF16)

    _stack_heads(qa_ref, q_ref, srow_ref, Q_BLOCK)
    s = _dot_nt(qa_ref[...], ka_ref[pl.ds(t0, span), :])
    i_col = lax.broadcasted_iota(jnp.int32, (rows, 1), 0) & (Q_BLOCK - 1)
    kk = lax.broadcasted_iota(jnp.int32, (1, Q_BLOCK), 1)
    first = jnp.where(kk > i_col, s[:, :Q_BLOCK], NEG)
    last = jnp.where(kk <= i_col, s[:, WINDOW:], NEG)
    m = jnp.maximum(jnp.max(first, axis=-1, keepdims=True), jnp.max(last, axis=-1, keepdims=True))
    if Q_BLOCK < WINDOW:
        mid = s[:, Q_BLOCK:WINDOW]
        m = jnp.maximum(m, jnp.max(mid, axis=-1, keepdims=True))
    acc = (_dot(jnp.exp2(first - m).astype(BF16), va_ref[pl.ds(t0, Q_BLOCK), :])
           + _dot(jnp.exp2(last - m).astype(BF16), va_ref[pl.ds(t0 + WINDOW, Q_BLOCK), :]))
    if Q_BLOCK < WINDOW:
        acc = acc + _dot(jnp.exp2(mid - m).astype(BF16), va_ref[pl.ds(t0 + Q_BLOCK, WINDOW - Q_BLOCK), :])
    gates = jax.nn.sigmoid(gate_ref[0])
    o = acc[:, :HEAD_DIM] / acc[:, HEAD_DIM:HEAD_DIM + 1]
    for r in range(GQA_REP):
        gate = gates[:, 2 * GQA_REP + r:2 * GQA_REP + r + 1]
        o_ref[:, r * HEAD_DIM:(r + 1) * HEAD_DIM] = (o[r * Q_BLOCK:(r + 1) * Q_BLOCK] * gate).astype(o_ref.dtype)


def _win_attention(proj, gates_r, srow, *, b, t, k_col, v_col):
    n = b * t
    nqb = t // Q_BLOCK
    rows = GQA_REP * Q_BLOCK
    rowblk = lambda bi, gi, c: bi * nqb + c
    return pl.pallas_call(
        _win_attn_kernel,
        grid=(b, N_KV, nqb),
        in_specs=[
            pl.BlockSpec((Q_BLOCK, GQA_REP * HEAD_DIM), lambda bi, gi, c: (rowblk(bi, gi, c), gi)),
            pl.BlockSpec((t, HEAD_DIM), lambda bi, gi, c: (bi, k_col + gi)),
            pl.BlockSpec((t, HEAD_DIM), lambda bi, gi, c: (bi, v_col + gi)),
            pl.BlockSpec((1, Q_BLOCK, LANES), lambda bi, gi, c: (gi, rowblk(bi, gi, c), 0)),
            pl.BlockSpec((1, 8, LANES), lambda bi, gi, c: (gi, 0, 0)),
        ],
        out_specs=pl.BlockSpec((Q_BLOCK, GQA_REP * HEAD_DIM), lambda bi, gi, c: (rowblk(bi, gi, c), gi)),
        out_shape=jax.ShapeDtypeStruct((n, N_HEADS * HEAD_DIM), BF16),
        scratch_shapes=[
            pltpu.VMEM((rows, 2 * HEAD_DIM), BF16),
            pltpu.VMEM((t + WINDOW, 2 * HEAD_DIM), BF16),
            pltpu.VMEM((t + WINDOW, 2 * HEAD_DIM), BF16),
        ],
        compiler_params=_params(("parallel", "parallel", "arbitrary")),
        name="nsa_win_attn",
    )(proj, proj, proj, gates_r, srow)


def _slc_attn_kernel(reach_ref, q_ref, k_ref, v_ref, sb_ref, gate_ref, srow_ref, scol_ref, o_ref, qa_ref, ka_ref, va_ref, m_ref, acc_ref, s_ref):
    c = pl.program_id(2)
    qb = SLC_Q_BLOCK
    t0 = c * qb
    rows = GQA_REP * qb
    tk = SLC_KEY_TILE
    nb = tk // SLC_BLOCK
    t_len = k_ref.shape[0]

    @pl.when(c == 0)
    def _():
        pos = lax.broadcasted_iota(jnp.int32, (t_len, LANES), 0)
        lane = lax.broadcasted_iota(jnp.int32, (t_len, LANES), 1)
        off = pos & (tk - 1)
        blk = (pos >> 6) & (nb - 1)
        aug = jnp.where(lane < nb, (blk == lane).astype(F32),
                        jnp.where(lane < nb + 3, ((off >> 3) << 3).astype(F32),
                                  jnp.where(lane < nb + 6, (off & 7).astype(F32), 0.0)))
        ka_ref[:, :HEAD_DIM] = k_ref[...]
        ka_ref[:, HEAD_DIM:] = aug.astype(BF16)
        va_ref[:, :HEAD_DIM] = v_ref[...]
        va_ref[:, HEAD_DIM:] = (lane == 0).astype(BF16)

    for r in range(GQA_REP):
        qa_ref[r * qb:(r + 1) * qb, :HEAD_DIM] = q_ref[:, r * HEAD_DIM:(r + 1) * HEAD_DIM]
    m_ref[...] = jnp.full(m_ref.shape, NEG, F32)
    acc_ref[...] = jnp.zeros(acc_ref.shape, F32)
    sb = sb_ref[0]
    scol = scol_ref[0]
    t_col = t0 + (lax.broadcasted_iota(jnp.int32, (rows, 1), 0) & (qb - 1))
    k_iota = lax.broadcasted_iota(jnp.int32, (1, tk), 1)
    p_row = lax.broadcasted_iota(jnp.int32, (SLC_LANES, LANES), 0)
    p_col = lax.broadcasted_iota(jnp.int32, (SLC_LANES, LANES), 1)

    def scores(kt):
        k0 = pl.multiple_of(kt * tk, tk)
        pick = ((p_row == kt * nb + p_col) & (p_col < nb)).astype(BF16)
        sbt = _dot(sb, pick).astype(BF16)
        for r in range(GQA_REP):
            qa_ref[r * qb:(r + 1) * qb, HEAD_DIM:] = sbt + srow_ref[0, r:r + 1, :]
        return _dot_nt(qa_ref[...], ka_ref[pl.ds(k0, tk), :])

    def softmax_pv(kt, s, causal):
        k0 = pl.multiple_of(kt * tk, tk)
        width = s.shape[1]
        if causal:
            s = jnp.where(k0 + k_iota[:, :width] <= t_col, s, NEG)
        shift = scol * (k0 - t0).astype(F32)
        m_old = m_ref[...]
        m_new = jnp.maximum(m_old, jnp.max(s, axis=-1, keepdims=True) + shift)
        alpha = jnp.exp2(m_old - m_new)
        p = jnp.exp2(s - (m_new - shift))
        acc_ref[...] = alpha * acc_ref[...] + _dot(p.astype(BF16), va_ref[pl.ds(k0, width), :])
        m_ref[...] = m_new

    n_below = lax.div(t0, tk)
    n_first = jnp.minimum(jnp.maximum(lax.div(t0 - reach_ref[pl.program_id(1)], tk), 0), n_below)
    s_ref[...] = scores(n_first)

    def body(kt, carry):
        s_cur = s_ref[...]
        s_next = scores(kt + 1)
        softmax_pv(kt, s_cur, False)
        s_ref[...] = s_next
        return carry

    lax.fori_loop(n_first, n_below, body, 0)
    groups = lax.div(t0 - n_below * tk, qb) + 1
    for j in range(1, tk // qb + 1):
        @pl.when(groups == j)
        def _(j=j):
            softmax_pv(n_below, s_ref[:, :j * qb], True)

    gates = jax.nn.sigmoid(gate_ref[0])
    acc = acc_ref[...]
    o = acc[:, :HEAD_DIM] / acc[:, HEAD_DIM:HEAD_DIM + 1]
    for r in range(GQA_REP):
        gate = gates[:, GQA_REP + r:GQA_REP + r + 1]
        o_ref[:, r * HEAD_DIM:(r + 1) * HEAD_DIM] = (o[r * qb:(r + 1) * qb] * gate).astype(o_ref.dtype)


def _slope_pieces(slopes):
    s_hi = slopes.astype(BF16)
    s_mid = (slopes - s_hi.astype(F32)).astype(BF16)
    s_lo = (slopes - s_hi.astype(F32) - s_mid.astype(F32)).astype(BF16)
    return jnp.stack([s_hi, s_mid, s_lo, s_hi, s_mid, s_lo], axis=1)


def _slope_rows(pieces, first_lane):
    rows = jnp.pad(pieces, ((0, 0), (first_lane, LANES - first_lane - pieces.shape[1]))).reshape(N_KV, GQA_REP, LANES)
    return jnp.pad(rows, ((0, 0), (0, 8 - GQA_REP), (0, 0)))


def _alibi_reach(slopes, gain_q, gain_k):
    bound = jnp.max(jnp.abs(gain_q)) * jnp.max(jnp.abs(gain_k)) * (HEAD_DIM ** 0.5 * LOG2E * 1.01)
    min_slope = jnp.min(slopes.reshape(N_KV, GQA_REP), axis=1)
    reach = jnp.ceil((2.0 * bound + EXP2_UNDERFLOW) / min_slope)
    return jnp.minimum(reach, 2.0 ** 30).astype(jnp.int32)


def _slc_attention(slopes, reach, proj, selbias, gates_r, *, b, t, k_col, v_col):
    n = b * t
    qb = SLC_Q_BLOCK
    nqb = t // qb
    rows = GQA_REP * qb
    nb = SLC_KEY_TILE // SLC_BLOCK
    rowblk = lambda bi, gi, c: bi * nqb + c
    srow = _slope_rows(_slope_pieces(slopes), nb)
    scol = jnp.repeat(slopes.reshape(N_KV, GQA_REP), qb, axis=1).reshape(N_KV, rows, 1)
    return pl.pallas_call(
        _slc_attn_kernel,
        grid=(b, N_KV, nqb),
        in_specs=[
            pl.BlockSpec(memory_space=pltpu.SMEM),
            pl.BlockSpec((qb, GQA_REP * HEAD_DIM), lambda bi, gi, c: (rowblk(bi, gi, c), gi)),
            pl.BlockSpec((t, HEAD_DIM), lambda bi, gi, c: (bi, k_col + gi)),
            pl.BlockSpec((t, HEAD_DIM), lambda bi, gi, c: (bi, v_col + gi)),
            pl.BlockSpec((1, qb, SLC_LANES), lambda bi, gi, c: (gi, rowblk(bi, gi, c), 0)),
            pl.BlockSpec((1, qb, LANES), lambda bi, gi, c: (gi, rowblk(bi, gi, c), 0)),
            pl.BlockSpec((1, 8, LANES), lambda bi, gi, c: (gi, 0, 0)),
            pl.BlockSpec((1, rows, 1), lambda bi, gi, c: (gi, 0, 0)),
        ],
        out_specs=pl.BlockSpec((qb, GQA_REP * HEAD_DIM), lambda bi, gi, c: (rowblk(bi, gi, c), gi)),
        out_shape=jax.ShapeDtypeStruct((n, N_HEADS * HEAD_DIM), BF16),
        scratch_shapes=[
            pltpu.VMEM((rows, 2 * HEAD_DIM), BF16),
            pltpu.VMEM((t, 2 * HEAD_DIM), BF16),
            pltpu.VMEM((t, 2 * HEAD_DIM), BF16),
            pltpu.VMEM((rows, 1), F32),
            pltpu.VMEM((rows, 2 * HEAD_DIM), F32),
            pltpu.VMEM((rows, SLC_KEY_TILE), F32),
        ],
        compiler_params=_params(("parallel", "parallel", "arbitrary")),
        name="nsa_slc_attn",
    )(reach, proj, proj, proj, selbias, gates_r, srow, scol)


def _sb_attn_kernel(q_ref, k_ref, v_ref, o_ref, acc_ref, carry_ref):
    ts = SB_TILE
    c = pl.program_id(2)
    acc_ref[...] = jnp.zeros(acc_ref.shape, F32)
    carry_ref[...] = jnp.zeros(carry_ref.shape, F32)
    r_idx = lax.broadcasted_iota(jnp.int32, (ts, 1), 0)
    c_idx = lax.broadcasted_iota(jnp.int32, (1, ts), 1)
    tri = (r_idx > c_idx).astype(BF16)
    diag = c_idx < r_idx

    def tile(jt, valid):
        k0 = pl.multiple_of(jt * ts, ts)
        heads = range(SB_HEADS_PER_STEP)
        cols = [slice(e * HEAD_DIM, (e + 1) * HEAD_DIM) for e in heads]
        carries = [carry_ref[e] for e in heads]
        zs = [_dot_nt(q_ref[:, cols[e]], k_ref[pl.ds(k0, ts), cols[e]]) for e in heads]
        lfs = []
        for e in heads:
            lf = -(jnp.maximum(zs[e], 0.0) + jnp.log2(1.0 + jnp.exp2(-jnp.abs(zs[e]))))
            lfs.append(lf if valid is None else jnp.where(valid, lf, 0.0))
        afters = []
        for e in heads:
            hi = lfs[e].astype(BF16)
            lo = (lfs[e] - hi.astype(F32)).astype(BF16)
            afters.append(_dot(hi, tri) + _dot(lo, tri))
        avs = []
        for e in heads:
            a = jnp.exp2(zs[e] + lfs[e] + afters[e] + carries[e])
            if valid is not None:
                a = jnp.where(valid, a, 0.0)
            avs.append(_dot(a.astype(BF16), v_ref[pl.ds(k0, ts), cols[e]]))
        for e in heads:
            acc_ref[e] += avs[e]
            carry_ref[e] = carries[e] + jnp.sum(lfs[e], axis=-1, keepdims=True)

    def live():
        return (jnp.max(carry_ref[...]) > SB_UNDERFLOW * LOG2E).astype(jnp.int32)

    tile(c, diag)

    def cond(state):
        jt, go = state
        return (jt >= 0) & (go > 0)

    def body(state):
        jt, _ = state
        tile(jt, None)
        return jt - 1, live()

    lax.while_loop(cond, body, (c - 1, live()))
    for e in range(SB_HEADS_PER_STEP):
        o_ref[:, e * HEAD_DIM:(e + 1) * HEAD_DIM] = acc_ref[e].astype(o_ref.dtype)


def _sb_attention(proj, *, b, t):
    n = b * t
    ts = SB_TILE
    nq = t // ts
    hs = SB_HEADS_PER_STEP
    width = hs * HEAD_DIM
    nhp = N_HEADS // hs
    return pl.pallas_call(
        _sb_attn_kernel,
        grid=(b, nhp, nq),
        in_specs=[
            pl.BlockSpec((ts, width), lambda bi, h, c: (bi * nq + c, h)),
            pl.BlockSpec((t, width), lambda bi, h, c: (bi, nhp + h), pipeline_mode=pl.Buffered(1)),
            pl.BlockSpec((t, width), lambda bi, h, c: (bi, 2 * nhp + h), pipeline_mode=pl.Buffered(1)),
        ],
        out_specs=pl.BlockSpec((ts, width), lambda bi, h, c: (bi * nq + c, h)),
        out_shape=jax.ShapeDtypeStruct((n, N_HEADS * HEAD_DIM), BF16),
        scratch_shapes=[pltpu.VMEM((hs, ts, HEAD_DIM), F32), pltpu.VMEM((hs, ts, 1), F32)],
        compiler_params=_params(("parallel", "parallel", "arbitrary")),
        name="sb_attn",
    )(proj, proj, proj)


def _row_copy(src_hbm, dst_ref, sem, src_row, dst_row):
    return pltpu.make_async_copy(src_hbm.at[pl.ds(src_row, 1), :], dst_ref.at[pl.ds(dst_row, 1), :], sem)


def _row_gather_start(src_hbm, dst_ref, sem, idx_ref, base, count):
    def body(r, carry):
        _row_copy(src_hbm, dst_ref, sem, idx_ref[base + r], r).start()
        return carry

    lax.fori_loop(0, count, body, 0, unroll=8)


def _row_gather_wait(src_hbm, dst_ref, sem):
    pltpu.make_async_copy(src_hbm.at[pl.ds(0, dst_ref.shape[0]), :], dst_ref, sem).wait()


def _expert_kernel(src_ref, te_ref, nu_ref, x_hbm, g_ref, wg_ref, wu_ref, wd_ref, o_ref, xbuf, wgb, wub, wdb, sem):
    tm = MOE_TILE
    i = pl.program_id(0)
    nt = pl.num_programs(0)
    n_used = nu_ref[0]
    slot = i % MOE_SLOTS
    nxt = (i + 2) % MOE_SLOTS
    next_base = jnp.where(i + 2 < nt, i + 2, i + 2 - nt) * tm

    @pl.when((i < n_used) & ((i == 0) | (te_ref[i] != te_ref[jnp.maximum(i - 1, 0)])))
    def _():
        wgb[...] = wg_ref[0, 0].astype(BF16)
        wub[...] = wu_ref[0, 0].astype(BF16)
        wdb[...] = wd_ref[0, 0].astype(BF16)

    @pl.when(i == 0)
    def _():
        _row_gather_start(x_hbm, xbuf.at[0], sem.at[0], src_ref, 0, tm)
        _row_gather_start(x_hbm, xbuf.at[1], sem.at[1], src_ref, tm, tm)

    @pl.when(i < n_used)
    def _():
        _row_gather_wait(x_hbm, xbuf.at[slot], sem.at[slot])
        x = xbuf[slot]
        ms = jnp.mean(x * x, axis=-1, keepdims=True)
        h = (x * lax.rsqrt(ms + EPS) * g_ref[...]).astype(BF16)
        for r in range(tm):
            _row_copy(x_hbm, xbuf.at[nxt], sem.at[nxt], src_ref[next_base + r], r).start()
        gate = _dot(h, wgb[...])
        up = _dot(h, wub[...])
        act = (gate * jax.nn.sigmoid(gate) * up).astype(BF16)
        o_ref[...] = _dot(act, wdb[...])

    @pl.when(i >= n_used)
    def _():
        o_ref[...] = jnp.zeros(o_ref.shape, o_ref.dtype)

    @pl.when(i == n_used - 1)
    def _():
        prev = (i + 1) % MOE_SLOTS
        _row_gather_wait(x_hbm, xbuf.at[prev], sem.at[prev])
        _row_gather_wait(x_hbm, xbuf.at[nxt], sem.at[nxt])


def _experts(src_tok, tile_expert, n_used, x, g, wg, wu, wd, layer):
    n, d = x.shape
    tm = MOE_TILE
    p_rows = src_tok.shape[0]
    de = wg.shape[3]
    grid_spec = pltpu.PrefetchScalarGridSpec(
        num_scalar_prefetch=3,
        grid=(p_rows // tm,),
        in_specs=[
            pl.BlockSpec(memory_space=pl.ANY),
            pl.BlockSpec((1, d), lambda i, s, te, nu: (0, 0)),
            pl.BlockSpec((1, 1, d, de), lambda i, s, te, nu: (layer, te[i], 0, 0)),
            pl.BlockSpec((1, 1, d, de), lambda i, s, te, nu: (layer, te[i], 0, 0)),
            pl.BlockSpec((1, 1, de, d), lambda i, s, te, nu: (layer, te[i], 0, 0)),
        ],
        out_specs=pl.BlockSpec((tm, d), lambda i, s, te, nu: (i, 0)),
        scratch_shapes=[
            pltpu.VMEM((MOE_SLOTS, tm, d), F32),
            pltpu.VMEM((d, de), BF16),
            pltpu.VMEM((d, de), BF16),
            pltpu.VMEM((de, d), BF16),
            pltpu.SemaphoreType.DMA((MOE_SLOTS,)),
        ],
    )
    return pl.pallas_call(
        _expert_kernel,
        grid_spec=grid_spec,
        out_shape=jax.ShapeDtypeStruct((p_rows, d), F32),
        compiler_params=_params(("arbitrary",)),
        name="moe_experts",
    )(src_tok, tile_expert, n_used, x, g.reshape(1, d), wg, wu, wd)


def _combine_kernel(pos_ref, x_ref, rt_ref, y_hbm, o_ref, ybuf, sem):
    tm = x_ref.shape[0]
    i = pl.program_id(0)
    nt = pl.num_programs(0)

    def start(tile, slot, unrolled):
        base = 2 * tile * tm
        for s in range(2):
            if unrolled:
                for r in range(tm):
                    _row_copy(y_hbm, ybuf.at[slot, s], sem.at[slot], pos_ref[base + 2 * r + s], r).start()
            else:
                def body(r, carry):
                    _row_copy(y_hbm, ybuf.at[slot, s], sem.at[slot], pos_ref[base + 2 * r + s], r).start()
                    return carry

                lax.fori_loop(0, tm, body, 0, unroll=8)

    @pl.when(i == 0)
    def _():
        start(0, 0, False)

    @pl.when(i + 1 < nt)
    def _():
        start(i + 1, (i + 1) % 2, True)

    slot = i % 2
    for s in range(2):
        _row_gather_wait(y_hbm, ybuf.at[slot, s], sem.at[slot])
    rt = rt_ref[...]
    o_ref[...] = x_ref[...] + rt[:, 2:3] * ybuf[slot, 0] + rt[:, 3:4] * ybuf[slot, 1]


def _combine(pos, x, routing, ys, *, tm):
    n, d = x.shape
    grid_spec = pltpu.PrefetchScalarGridSpec(
        num_scalar_prefetch=1,
        grid=(n // tm,),
        in_specs=[
            pl.BlockSpec((tm, d), lambda i, p: (i, 0)),
            pl.BlockSpec((tm, LANES), lambda i, p: (i, 0)),
            pl.BlockSpec(memory_space=pl.ANY),
        ],
        out_specs=pl.BlockSpec((tm, d), lambda i, p: (i, 0)),
        scratch_shapes=[pltpu.VMEM((2, 2, tm, d), F32), pltpu.SemaphoreType.DMA((2,))],
    )
    return pl.pallas_call(
        _combine_kernel,
        grid_spec=grid_spec,
        out_shape=jax.ShapeDtypeStruct((n, d), F32),
        compiler_params=_params(("arbitrary",)),
        name="moe_combine",
    )(pos, x, routing, ys)


def _source_tokens_kernel(pos_ref, src_ref):
    def zero(q, carry):
        src_ref[q] = 0
        return carry

    def put(p, carry):
        src_ref[pos_ref[p]] = p >> 1
        return carry

    lax.fori_loop(0, src_ref.shape[0], zero, 0, unroll=16)
    lax.fori_loop(0, pos_ref.shape[0], put, 0, unroll=16)


def _source_tokens(pos, p_rows):
    return pl.pallas_call(
        _source_tokens_kernel,
        in_specs=[pl.BlockSpec(memory_space=pltpu.SMEM)],
        out_specs=pl.BlockSpec(memory_space=pltpu.SMEM),
        out_shape=jax.ShapeDtypeStruct((p_rows,), jnp.int32),
        name="moe_source_tokens",
    )(pos)


def _hier_moe(x, routing, g, w_gate, w_up, w_down, layer):
    n, d = x.shape
    tm = MOE_TILE

    e_flat = routing[:, :2].astype(jnp.int32).reshape(-1)
    onehot = (e_flat[:, None] == jnp.arange(N_EXPERTS, dtype=jnp.int32)[None, :]).astype(jnp.int32)
    counts = jnp.sum(onehot, axis=0)
    rank = jnp.sum((jnp.cumsum(onehot, axis=0) - onehot) * onehot, axis=1)
    padded = ((counts + tm - 1) // tm) * tm
    ends = jnp.cumsum(padded)
    pos = (ends - padded)[e_flat] + rank
    p_rows = 2 * n + N_EXPERTS * tm
    pos = pos.astype(jnp.int32)
    src_tok = _source_tokens(pos, p_rows)
    tile_start = jnp.arange(p_rows // tm, dtype=jnp.int32) * tm
    tile_expert = jnp.minimum(jnp.sum((ends[None, :] <= tile_start[:, None]).astype(jnp.int32), axis=1), N_EXPERTS - 1)
    n_used = (ends[-1] // tm).astype(jnp.int32).reshape(1)

    ys = _experts(src_tok, tile_expert, n_used, x, g, w_gate, w_up, w_down, layer)
    return _combine(pos, x, routing, ys, tm=256 if n % 256 == 0 else n)


def _alibi_slopes():
    return jnp.exp2(-8.0 * jnp.arange(1, N_HEADS + 1, dtype=F32) / N_HEADS)


def _cmp_to_slc(n_cmp_pad, n_slc):
    cs = np.arange(n_cmp_pad) * CMP_STRIDE
    ss = np.arange(SLC_LANES) * SLC_BLOCK
    lo = np.maximum(cs[:, None], ss[None, :])
    hi = np.minimum(cs[:, None] + CMP_BLOCK, ss[None, :] + SLC_BLOCK)
    m = np.maximum(hi - lo, 0).astype(np.float32) / CMP_BLOCK
    m[:, n_slc:] = 0.0
    return m


def _nsa_mixer(x, g_norm, w_in, qk_gain, cmp_pe, cmp_w1, cmp_w2, w_out, router, *, b, t):
    n, d = x.shape
    q_dim = N_HEADS * HEAD_DIM
    kv_dim = N_KV * HEAD_DIM
    scale = HEAD_DIM ** -0.5 * LOG2E
    main = q_dim + 6 * kv_dim
    ones = jnp.ones((kv_dim,), F32)
    tile4 = lambda v: jnp.tile(v, N_KV)
    cgain = jnp.concatenate([jnp.tile(qk_gain[0], N_HEADS) * scale, ones, ones, tile4(qk_gain[2]), ones, tile4(qk_gain[3]), ones])
    tiles_per = lambda cols: cols // PROJ_COLS
    q_tiles = tiles_per(q_dim)
    kv_tiles = tiles_per(kv_dim)
    norm_tiles = list(range(q_tiles)) + [q_tiles + 2 * kv_tiles + j for j in range(kv_tiles)] + [q_tiles + 4 * kv_tiles + j for j in range(kv_tiles)]
    n_gate = 3 * N_HEADS
    w_gate = jnp.pad(w_in[:, main:], ((0, 0), (0, LANES - n_gate))).astype(BF16)
    cmp_tiles = [q_tiles + j for j in range(2 * kv_tiles)]
    proj, gate_logits, kv_cmp = _norm_matmul(x, g_norm, w_in[:, :main].astype(BF16), cgain, norm_tiles, extra_w=w_gate, f32_tiles=cmp_tiles)
    gr = gate_logits[:, :n_gate].reshape(n, 3, N_KV, GQA_REP).transpose(2, 0, 1, 3).reshape(N_KV, n, 3 * GQA_REP)
    gates_r = jnp.pad(gr, ((0, 0), (0, 0), (0, LANES - 3 * GQA_REP)))

    nc = t // CMP_STRIDE
    half = CMP_BLOCK // 2
    w1cat = jnp.concatenate([cmp_w1[:, :half], cmp_w1[:, half:]], axis=-1)
    pe_rows = jnp.pad(jnp.stack([cmp_pe[:, :half], cmp_pe[:, half:]], axis=2), ((0, 0), (0, 0), (0, 6), (0, 0)))
    cmp_gain = jnp.stack([qk_gain[1], jnp.ones((HEAD_DIM,), F32)]).reshape(2, 1, HEAD_DIM)
    cmp_kv = _compress(kv_cmp, w1cat.astype(BF16), pe_rows.astype(BF16), cmp_w2.astype(BF16), cmp_gain, b=b, t=t)

    slopes = _alibi_slopes() * LOG2E
    pieces = _slope_pieces(slopes)
    pad_flag = jnp.full((N_HEADS, 1), NEG, F32).astype(BF16)
    srow = _slope_rows(jnp.concatenate([pieces, pad_flag], axis=1), 0)
    n_slc = t // SLC_BLOCK
    msel = jnp.asarray(_cmp_to_slc(nc, n_slc), BF16)
    q_blocks = q_dim // HEAD_DIM
    o_c, selbias = _cmp_attention(proj, cmp_kv, gates_r, msel, srow, b=b, t=t, top_k=min(SLC_TOPK, n_slc))
    reach = _alibi_reach(slopes, qk_gain[0], qk_gain[2])
    o_s = _slc_attention(slopes, reach, proj, selbias, gates_r, b=b, t=t, k_col=q_blocks + 2 * N_KV, v_col=q_blocks + 3 * N_KV)
    o_w = _win_attention(proj, gates_r, srow, b=b, t=t, k_col=q_blocks + 4 * N_KV, v_col=q_blocks + 5 * N_KV)
    return _proj_residual([o_c, o_s, o_w], w_out.astype(BF16), x, router)


def _sb_mixer(x, g_norm, w_in, w_out, router, *, b, t):
    q_dim = N_HEADS * HEAD_DIM
    scale = HEAD_DIM ** -0.5 * LOG2E
    cgain = jnp.concatenate([jnp.full((q_dim,), scale, F32), jnp.ones((2 * q_dim,), F32)])
    proj = _norm_matmul(x, g_norm, w_in.astype(BF16), cgain, ())
    o = _sb_attention(proj, b=b, t=t)
    return _proj_residual([o], w_out.astype(BF16), x, router)


def kernel(x, norm_mix, norm_ffn, nsa_w_in, nsa_qk_gain, nsa_cmp_pe, nsa_cmp_w1, nsa_cmp_w2, nsa_w_out, sb_w_in, sb_w_out, moe_w_group, moe_b_group, moe_w_router, moe_b_router, moe_w_gate, moe_w_up, moe_w_down):
    b, t, d = x.shape
    depth = norm_mix.shape[0]
    xf = x.reshape(b * t, d)
    for i in range(depth):
        j = i // 2
        router = _router_params(norm_ffn[i], moe_w_group[i], moe_b_group[i], moe_w_router[i], moe_b_router[i])
        if i % 2 == 0:
            xf, routing = _nsa_mixer(xf, norm_mix[i], nsa_w_in[j], nsa_qk_gain[j], nsa_cmp_pe[j], nsa_cmp_w1[j], nsa_cmp_w2[j], nsa_w_out[j], router, b=b, t=t)
        else:
            xf, routing = _sb_mixer(xf, norm_mix[i], sb_w_in[j], sb_w_out[j], router, b=b, t=t)
        xf = _hier_moe(xf, routing, norm_ffn[i], moe_w_gate, moe_w_up, moe_w_down, i)
    return xf.reshape(b, t, d)
```

```python
import functools

import jax
import jax.numpy as jnp
import numpy as np
from jax import lax
from jax.experimental import pallas as pl
from jax.experimental.pallas import tpu as pltpu

F32 = jnp.float32
BF16 = jnp.bfloat16

HEAD_DIM = 128
N_HEADS = 16
N_KV = 4
GQA_REP = N_HEADS // N_KV
CMP_BLOCK = 32
CMP_STRIDE = 16
SLC_BLOCK = 64
SLC_TOPK = 16
WINDOW = 512
Q_BLOCK = 256
WIN_SUB = 2
N_GROUPS = 4
EXPERTS_PER_GROUP = 8
N_EXPERTS = N_GROUPS * EXPERTS_PER_GROUP
EPS = 1e-6
NEG = -1e30
LANES = 128
VMEM_LIMIT = 56 * 1024 * 1024

PROJ_ROWS = 512
PROJ_COLS = 512
CMP_Q_BLOCK = 256
CMP_SUB = 2
SLC_Q_BLOCK = 256
SLC_LANES = 128
SB_TILE = 256
SB_HEADS_PER_STEP = 8
SLC_KEY_TILE = 1024
LOG2E = 1.4426950408889634
SB_UNDERFLOW = -105.0
MOE_TILE = 256
MOE_SLOTS = 3
EXP2_UNDERFLOW = 152.0
AUG_PAD_LANE = 6


def _dot(a, b):
    return jnp.dot(a, b, preferred_element_type=F32)


def _dot_nt(a, b):
    return lax.dot_general(a, b, (((1,), (1,)), ((), ())), preferred_element_type=F32)


def _params(sem):
    return pltpu.CompilerParams(dimension_semantics=sem, vmem_limit_bytes=VMEM_LIMIT)


def _resident(shape):
    return pl.BlockSpec(shape, lambda *_: (0,) * len(shape), pipeline_mode=pl.Buffered(1))


def _norm_matmul_kernel(x_ref, g_ref, w_ref, cg_ref, *rest, norm_tiles, f32_tiles, has_extra):
    if has_extra:
        wx_ref, o_ref, ox_ref, of_ref = rest
    else:
        (o_ref,) = rest
    x = x_ref[...]
    ms = jnp.mean(x * x, axis=-1, keepdims=True)
    h = (x * lax.rsqrt(ms + EPS) * g_ref[...]).astype(BF16)
    tn = PROJ_COLS
    for j in range(w_ref.shape[1] // tn):
        y = _dot(h, w_ref[:, j * tn:(j + 1) * tn])
        if j in f32_tiles:
            k = f32_tiles.index(j)
            of_ref[:, k * tn:(k + 1) * tn] = y
        for hh in range(tn // LANES):
            yh = y[:, hh * LANES:(hh + 1) * LANES]
            if j in norm_tiles:
                yh = yh * lax.rsqrt(jnp.mean(yh * yh, axis=-1, keepdims=True) + EPS)
            sl = slice(j * tn + hh * LANES, j * tn + (hh + 1) * LANES)
            o_ref[:, sl] = (yh * cg_ref[:, sl]).astype(o_ref.dtype)
    if has_extra:
        ox_ref[...] = _dot(h, wx_ref[...])


def _norm_matmul(x, g, w, cgain, norm_tiles, extra_w=None, f32_tiles=()):
    n, d = x.shape
    m = w.shape[1]
    tm = PROJ_ROWS
    has_extra = extra_w is not None
    in_specs = [pl.BlockSpec((tm, d), lambda i: (i, 0)), _resident((1, d)), _resident((d, m)), _resident((1, m))]
    out_specs = [pl.BlockSpec((tm, m), lambda i: (i, 0))]
    out_shape = [jax.ShapeDtypeStruct((n, m), BF16)]
    args = [x, g.reshape(1, d), w, cgain.reshape(1, m)]
    if has_extra:
        mx = extra_w.shape[1]
        mf = len(f32_tiles) * PROJ_COLS
        in_specs.append(_resident((d, mx)))
        out_specs += [pl.BlockSpec((tm, mx), lambda i: (i, 0)), pl.BlockSpec((tm, mf), lambda i: (i, 0))]
        out_shape += [jax.ShapeDtypeStruct((n, mx), F32), jax.ShapeDtypeStruct((n, mf), F32)]
        args.append(extra_w)
    out = pl.pallas_call(
        functools.partial(_norm_matmul_kernel, norm_tiles=tuple(norm_tiles), f32_tiles=tuple(f32_tiles), has_extra=has_extra),
        grid=(n // tm,),
        in_specs=in_specs,
        out_specs=out_specs,
        out_shape=out_shape,
        compiler_params=_params(("parallel",)),
        name="norm_matmul",
    )(*args)
    return out if has_extra else out[0]


def _first_max(vals, lane):
    m = jnp.max(vals, axis=-1, keepdims=True)
    idx = jnp.min(jnp.where(vals == m, lane, LANES), axis=-1, keepdims=True)
    return m, idx


def _route(x, g_ref, w_ref, b_ref):
    ms = jnp.mean(x * x, axis=-1, keepdims=True)
    h = x * lax.rsqrt(ms + EPS) * g_ref[...]
    h_hi = h.astype(BF16)
    h_lo = (h - h_hi.astype(F32)).astype(BF16)
    logits = _dot(h_hi, w_ref[0]) + _dot(h_hi, w_ref[1]) + _dot(h_lo, w_ref[0]) + b_ref[...]
    lane = lax.broadcasted_iota(jnp.int32, (1, LANES), 1)
    gl = jnp.where(lane < N_GROUPS, logits, NEG)
    ge = jnp.exp(gl - jnp.max(gl, axis=-1, keepdims=True))
    pg_all = ge / jnp.sum(ge, axis=-1, keepdims=True)
    pg, gsel = _first_max(jnp.where(lane < N_GROUPS, pg_all, -1.0), lane)
    lo = N_GROUPS + gsel * EXPERTS_PER_GROUP
    in_group = (lane >= lo) & (lane < lo + EXPERTS_PER_GROUP)
    el = jnp.where(in_group, logits, NEG)
    ee = jnp.exp(el - jnp.max(el, axis=-1, keepdims=True))
    pe_all = jnp.where(in_group, ee / jnp.sum(ee, axis=-1, keepdims=True), -1.0)
    p1, i1 = _first_max(pe_all, lane)
    p2, i2 = _first_max(jnp.where(lane == i1, -1.0, pe_all), lane)
    denom = p1 + p2
    w1 = pg * p1 / denom
    w2 = pg * p2 / denom
    e1 = (i1 - N_GROUPS).astype(F32)
    e2 = (i2 - N_GROUPS).astype(F32)
    return jnp.where(lane == 0, e1, jnp.where(lane == 1, e2, jnp.where(lane == 2, w1, jnp.where(lane == 3, w2, 0.0))))


def _proj_residual_kernel(*refs, n_in):
    a_refs = refs[:n_in]
    w_ref, r_ref, g_ref, wr_ref, br_ref, o_ref, rt_ref = refs[n_in:]
    if n_in == 1:
        a = a_refs[0][...]
    else:
        acc = a_refs[0][...].astype(F32)
        for a_ref in a_refs[1:]:
            acc = acc + a_ref[...].astype(F32)
        a = acc.astype(BF16)
    tn = PROJ_COLS
    for j in range(w_ref.shape[1] // tn):
        sl = slice(j * tn, (j + 1) * tn)
        o_ref[:, sl] = r_ref[:, sl] + _dot(a, w_ref[:, sl])
    rt_ref[...] = _route(o_ref[...], g_ref, wr_ref, br_ref)


def _proj_residual(branches, w, res, router):
    n, k = branches[0].shape
    m = w.shape[1]
    tm = PROJ_ROWS
    n_in = len(branches)
    g_ffn, w_split, b_r = router
    return pl.pallas_call(
        functools.partial(_proj_residual_kernel, n_in=n_in),
        grid=(n // tm,),
        in_specs=[pl.BlockSpec((tm, k), lambda i: (i, 0)) for _ in range(n_in)]
        + [_resident((k, m)), pl.BlockSpec((tm, m), lambda i: (i, 0)), _resident((1, m)), _resident(w_split.shape), _resident((1, LANES))],
        out_specs=[pl.BlockSpec((tm, m), lambda i: (i, 0)), pl.BlockSpec((tm, LANES), lambda i: (i, 0))],
        out_shape=[jax.ShapeDtypeStruct((n, m), F32), jax.ShapeDtypeStruct((n, LANES), F32)],
        compiler_params=_params(("parallel",)),
        name="proj_residual",
    )(*branches, w, res, g_ffn.reshape(1, m), w_split, b_r)


def _router_params(g_ffn, w_group, b_group, w_router, b_router):
    n_logit = N_GROUPS + N_EXPERTS
    w_r = jnp.pad(jnp.concatenate([w_group, w_router], axis=1), ((0, 0), (0, LANES - n_logit)))
    b_r = jnp.pad(jnp.concatenate([b_group, b_router]), (0, LANES - n_logit)).reshape(1, LANES)
    w_hi = w_r.astype(BF16)
    return g_ffn, jnp.stack([w_hi, (w_r - w_hi.astype(F32)).astype(BF16)]), b_r


def _cmp_kernel(kv_ref, w1_ref, pe_ref, w2_ref, gain_ref, o_ref):
    nc = kv_ref.shape[0] // CMP_STRIDE
    p = jnp.zeros((nc, 2 * HEAD_DIM), F32)
    pb = jnp.zeros((8, 2 * HEAD_DIM), F32)
    for l in range(CMP_STRIDE):
        w1 = w1_ref[0, l]
        p = p + _dot(kv_ref[pl.ds(l, nc, stride=CMP_STRIDE), :].astype(BF16), w1)
        pb = pb + _dot(pe_ref[0, l], w1)
    bias = pb[0:1, :HEAD_DIM] + pb[1:2, HEAD_DIM:]
    second = pltpu.roll(p[:, HEAD_DIM:], shift=nc - 1, axis=0)
    hid = jax.nn.gelu(p[:, :HEAD_DIM] + second + bias)
    c = _dot(hid.astype(BF16), w2_ref[0])
    nf = (pl.program_id(1) == 0).astype(F32)
    ms = jnp.mean(c * c, axis=-1, keepdims=True)
    fac = nf * lax.rsqrt(ms + EPS) + (1.0 - nf)
    o_ref[0, 0, 0] = (c * fac * gain_ref[0]).astype(o_ref.dtype)


def _compress(kv, w1cat, pe_rows, w2, gains, *, b, t):
    g, cs, hd = N_KV, CMP_STRIDE, HEAD_DIM
    nc = t // cs
    return pl.pallas_call(
        _cmp_kernel,
        grid=(b, 2, g),
        in_specs=[
            pl.BlockSpec((t, hd), lambda bi, k, gi: (bi, k * g + gi)),
            pl.BlockSpec((1, cs, hd, 2 * HEAD_DIM), lambda bi, k, gi: (k, 0, 0, 0)),
            pl.BlockSpec((1, cs, 8, hd), lambda bi, k, gi: (k, 0, 0, 0)),
            pl.BlockSpec((1, HEAD_DIM, HEAD_DIM), lambda bi, k, gi: (k, 0, 0)),
            pl.BlockSpec((1, 1, HEAD_DIM), lambda bi, k, gi: (k, 0, 0)),
        ],
        out_specs=pl.BlockSpec((1, 1, 1, nc, HEAD_DIM), lambda bi, k, gi: (bi, k, gi, 0, 0)),
        out_shape=jax.ShapeDtypeStruct((b, 2, g, nc, HEAD_DIM), BF16),
        compiler_params=_params(("parallel", "parallel", "parallel")),
        name="nsa_compress",
    )(kv, w1cat, pe_rows, w2, gains)


def _cmp_attn_kernel(q_ref, kc_ref, vc_ref, gate_ref, msel_ref, srow_ref, oc_ref, sb_ref, qa_ref, ka_ref, va_ref, *, top_k):
    c = pl.program_id(2)
    qb = CMP_Q_BLOCK
    t_step = c * (CMP_SUB * qb)
    rows = GQA_REP * qb
    nc = ka_ref.shape[0]
    subs = range(CMP_SUB)

    @pl.when(c == 0)
    def _():
        n_idx = lax.broadcasted_iota(jnp.int32, (nc, LANES), 0)
        lane = lax.broadcasted_iota(jnp.int32, (nc, LANES), 1)
        aug = jnp.where(lane < 3, ((n_idx >> 7) << 11).astype(F32), jnp.where(lane < 6, ((n_idx & 127) << 4).astype(F32), 0.0))
        ka_ref[:, :HEAD_DIM] = kc_ref[0, 0, 0]
        ka_ref[:, HEAD_DIM:] = aug.astype(BF16)
        va_ref[:, :HEAD_DIM] = vc_ref[0, 0, 0]
        va_ref[:, HEAD_DIM:] = (lane == 0).astype(BF16)

    for u in subs:
        for r in range(GQA_REP):
            dst = slice((u * GQA_REP + r) * qb, (u * GQA_REP + r + 1) * qb)
            qa_ref[dst, :HEAD_DIM] = q_ref[u * qb:(u + 1) * qb, r * HEAD_DIM:(r + 1) * HEAD_DIM]
            qa_ref[dst, HEAD_DIM:] = jnp.broadcast_to(srow_ref[0, r:r + 1, :], (qb, LANES))
    row_q = lax.broadcasted_iota(jnp.int32, (rows, 1), 0) & (qb - 1)
    lane_q = lax.broadcasted_iota(jnp.int32, (1, qb), 1)
    gates = jax.nn.sigmoid(gate_ref[0])

    def attend(k):
        ncol = k * LANES
        nrow = ncol * CMP_STRIDE // SLC_BLOCK
        cend = lax.broadcasted_iota(jnp.int32, (1, ncol), 1) * CMP_STRIDE + (CMP_BLOCK - 1)
        j_col = lax.broadcasted_iota(jnp.int32, (nrow, 1), 0)
        msel = msel_ref[:ncol, :]
        scores = [_dot_nt(qa_ref[u * rows:(u + 1) * rows, :], ka_ref[:ncol, :]) for u in subs]
        es, ms = [], []
        for u in subs:
            s = jnp.where(cend <= t_step + u * qb + row_q, scores[u], NEG)
            m = jnp.max(s, axis=-1, keepdims=True)
            ms.append(m)
            es.append(jnp.exp2(s - m))
        accs = [_dot(es[u].astype(BF16), va_ref[:ncol, :]) for u in subs]
        imps = []
        for u in subs:
            inv = jnp.where(ms[u] > 0.5 * NEG, 1.0 / accs[u][:, HEAD_DIM:HEAD_DIM + 1], 0.0)
            psum = jnp.zeros((qb, ncol), F32)
            for r in range(GQA_REP):
                rs = slice(r * qb, (r + 1) * qb)
                psum = psum + es[u][rs] * inv[rs]
                gate = gates[u * qb:(u + 1) * qb, r:r + 1]
                oc_ref[u * qb:(u + 1) * qb, r * HEAD_DIM:(r + 1) * HEAD_DIM] = (accs[u][rs, :HEAD_DIM] * (inv[rs] * gate)).astype(oc_ref.dtype)
            p_hi = psum.astype(BF16)
            rem = psum - p_hi.astype(F32)
            p_mid = rem.astype(BF16)
            p_lo = (rem - p_mid.astype(F32)).astype(BF16)
            imps.append(_dot(p_hi, msel) + _dot(p_mid, msel) + _dot(p_lo, msel))

        score, keep = [], []
        for u in subs:
            t_row = t_step + u * qb + lane_q
            forced = (j_col == 0) | (j_col == (t_row >> 6))
            causal = (j_col << 6) <= t_row
            score.append(jnp.where(forced, -NEG, jnp.where(causal, imps[u].T[:nrow], NEG)))
            keep.append(jnp.zeros((nrow, qb), F32))
        for _ in range(top_k):
            for u in subs:
                mx = jnp.max(score[u], axis=0, keepdims=True)
                first = jnp.min(jnp.where(score[u] == mx, j_col, SLC_LANES), axis=0, keepdims=True)
                hit = j_col == first
                keep[u] = jnp.where(hit, (mx > 0.5 * NEG).astype(F32), keep[u])
                score[u] = jnp.where(hit, -3e38, score[u])
        for u in subs:
            bias_t = jnp.where(keep[u] > 0.5, 0.0, NEG)
            if nrow < SLC_LANES:
                bias_t = jnp.concatenate([bias_t, jnp.full((SLC_LANES - nrow, qb), NEG, F32)], axis=0)
            sb_ref[0, u * qb:(u + 1) * qb, :] = bias_t.T.astype(sb_ref.dtype)

    n_variants = nc // LANES
    needed = jnp.minimum(lax.div(lax.div(t_step + CMP_SUB * qb - CMP_BLOCK, CMP_STRIDE), LANES) + 1, n_variants)
    for k in range(1, n_variants + 1):
        pl.when(needed == k)(functools.partial(attend, k))


def _cmp_attention(proj, cmp_kv, gates_r, msel, srow, *, b, t, top_k):
    n = b * t
    qb = CMP_SUB * CMP_Q_BLOCK
    nqb = t // qb
    rows = GQA_REP * qb
    nc = cmp_kv.shape[3]
    rowblk = lambda bi, gi, c: bi * nqb + c
    return pl.pallas_call(
        functools.partial(_cmp_attn_kernel, top_k=top_k),
        grid=(b, N_KV, nqb),
        in_specs=[
            pl.BlockSpec((qb, GQA_REP * HEAD_DIM), lambda bi, gi, c: (rowblk(bi, gi, c), gi)),
            pl.BlockSpec((1, 1, 1, nc, HEAD_DIM), lambda bi, gi, c: (bi, 0, gi, 0, 0)),
            pl.BlockSpec((1, 1, 1, nc, HEAD_DIM), lambda bi, gi, c: (bi, 1, gi, 0, 0)),
            pl.BlockSpec((1, qb, LANES), lambda bi, gi, c: (gi, rowblk(bi, gi, c), 0)),
            pl.BlockSpec((nc, SLC_LANES), lambda bi, gi, c: (0, 0)),
            pl.BlockSpec((1, 8, LANES), lambda bi, gi, c: (gi, 0, 0)),
        ],
        out_specs=[
            pl.BlockSpec((qb, GQA_REP * HEAD_DIM), lambda bi, gi, c: (rowblk(bi, gi, c), gi)),
            pl.BlockSpec((1, qb, SLC_LANES), lambda bi, gi, c: (gi, rowblk(bi, gi, c), 0)),
        ],
        out_shape=[
            jax.ShapeDtypeStruct((n, N_HEADS * HEAD_DIM), BF16),
            jax.ShapeDtypeStruct((N_KV, n, SLC_LANES), BF16),
        ],
        scratch_shapes=[
            pltpu.VMEM((rows, 2 * HEAD_DIM), BF16),
            pltpu.VMEM((nc, 2 * HEAD_DIM), BF16),
            pltpu.VMEM((nc, 2 * HEAD_DIM), BF16),
        ],
        compiler_params=_params(("parallel", "parallel", "arbitrary")),
        name="nsa_cmp_attn",
    )(proj, cmp_kv, cmp_kv, gates_r, msel, srow)


def _win_attn_kernel(q_ref, k_ref, v_ref, gate_ref, srow_ref, o_ref, qa_ref, ka_ref, va_ref):
    c = pl.program_id(2)
    t0 = pl.multiple_of(c * (WIN_SUB * Q_BLOCK), WIN_SUB * Q_BLOCK)
    rows = GQA_REP * Q_BLOCK
    span = Q_BLOCK + WINDOW
    t_len = k_ref.shape[0]

    @pl.when(c == 0)
    def _():
        pos = lax.broadcasted_iota(jnp.int32, (t_len, LANES), 0)
        lane = lax.broadcasted_iota(jnp.int32, (t_len, LANES), 1)
        aug = jnp.where(lane < 3, ((pos >> 6) << 6).astype(F32), jnp.where(lane < 6, (pos & 63).astype(F32), 0.0))
        pad_lane = lax.broadcasted_iota(jnp.int32, (WINDOW, LANES), 1)
        ka_ref[:WINDOW, :HEAD_DIM] = jnp.zeros((WINDOW, HEAD_DIM), BF16)
        ka_ref[:WINDOW, HEAD_DIM:] = (pad_lane == AUG_PAD_LANE).astype(BF16)
        ka_ref[WINDOW:, :HEAD_DIM] = k_ref[...]
        ka_ref[WINDOW:, HEAD_DIM:] = aug.astype(BF16)
        va_ref[:WINDOW, :] = jnp.zeros((WINDOW, 2 * HEAD_DIM), BF16)
        va_ref[WINDOW:, :HEAD_DIM] = v_ref[...]
        va_ref[WINDOW:, HEAD_DIM:] = (lane == 0).astype(BF16)

    i_col = lax.broadcasted_iota(jnp.int32, (rows, 1), 0) & (Q_BLOCK - 1)
    kk = lax.broadcasted_iota(jnp.int32, (1, Q_BLOCK), 1)
    gates = jax.nn.sigmoid(gate_ref[0])
    subs = range(WIN_SUB)
    for u in subs:
        for r in range(GQA_REP):
            dst = slice((u * GQA_REP + r) * Q_BLOCK, (u * GQA_REP + r + 1) * Q_BLOCK)
            qa_ref[dst, :HEAD_DIM] = q_ref[u * Q_BLOCK:(u + 1) * Q_BLOCK, r * HEAD_DIM:(r + 1) * HEAD_DIM]
            qa_ref[dst, HEAD_DIM:] = jnp.broadcast_to(srow_ref[0, r:r + 1, :], (Q_BLOCK, LANES))
    starts = [t0 + u * Q_BLOCK for u in subs]
    scores = [_dot_nt(qa_ref[u * rows:(u + 1) * rows, :], ka_ref[pl.ds(starts[u], span), :]) for u in subs]
    weights = []
    for u in subs:
        s = scores[u]
        first = jnp.where(kk > i_col, s[:, :Q_BLOCK], NEG)
        mid = s[:, Q_BLOCK:WINDOW]
        last = jnp.where(kk <= i_col, s[:, WINDOW:], NEG)
        m = jnp.maximum(jnp.maximum(jnp.max(first, axis=-1, keepdims=True), jnp.max(mid, axis=-1, keepdims=True)),
                        jnp.max(last, axis=-1, keepdims=True))
        weights.append([jnp.exp2(part - m).astype(BF16) for part in (first, mid, last)])
    accs = []
    for u in subs:
        p_first, p_mid, p_last = weights[u]
        accs.append(_dot(p_first, va_ref[pl.ds(starts[u], Q_BLOCK), :])
                    + _dot(p_mid, va_ref[pl.ds(starts[u] + Q_BLOCK, WINDOW - Q_BLOCK), :])
                    + _dot(p_last, va_ref[pl.ds(starts[u] + WINDOW, Q_BLOCK), :]))
    for u in subs:
        o = accs[u][:, :HEAD_DIM] / accs[u][:, HEAD_DIM:HEAD_DIM + 1]
        for r in range(GQA_REP):
            gate = gates[u * Q_BLOCK:(u + 1) * Q_BLOCK, 2 * GQA_REP + r:2 * GQA_REP + r + 1]
            o_ref[u * Q_BLOCK:(u + 1) * Q_BLOCK, r * HEAD_DIM:(r + 1) * HEAD_DIM] = (o[r * Q_BLOCK:(r + 1) * Q_BLOCK] * gate).astype(o_ref.dtype)


def _win_attention(proj, gates_r, srow, *, b, t, k_col, v_col):
    n = b * t
    step_q = WIN_SUB * Q_BLOCK
    nqb = t // step_q
    rows = GQA_REP * Q_BLOCK
    rowblk = lambda bi, gi, c: bi * nqb + c
    return pl.pallas_call(
        _win_attn_kernel,
        grid=(b, N_KV, nqb),
        in_specs=[
            pl.BlockSpec((step_q, GQA_REP * HEAD_DIM), lambda bi, gi, c: (rowblk(bi, gi, c), gi)),
            pl.BlockSpec((t, HEAD_DIM), lambda bi, gi, c: (bi, k_col + gi)),
            pl.BlockSpec((t, HEAD_DIM), lambda bi, gi, c: (bi, v_col + gi)),
            pl.BlockSpec((1, step_q, LANES), lambda bi, gi, c: (gi, rowblk(bi, gi, c), 0)),
            pl.BlockSpec((1, 8, LANES), lambda bi, gi, c: (gi, 0, 0)),
        ],
        out_specs=pl.BlockSpec((step_q, GQA_REP * HEAD_DIM), lambda bi, gi, c: (rowblk(bi, gi, c), gi)),
        out_shape=jax.ShapeDtypeStruct((n, N_HEADS * HEAD_DIM), BF16),
        scratch_shapes=[
            pltpu.VMEM((WIN_SUB * rows, 2 * HEAD_DIM), BF16),
            pltpu.VMEM((t + WINDOW, 2 * HEAD_DIM), BF16),
            pltpu.VMEM((t + WINDOW, 2 * HEAD_DIM), BF16),
        ],
        compiler_params=_params(("parallel", "parallel", "arbitrary")),
        name="nsa_win_attn",
    )(proj, proj, proj, gates_r, srow)


def _slc_attn_kernel(reach_ref, q_ref, k_ref, v_ref, sb_ref, gate_ref, srow_ref, scol_ref, o_ref, qa_ref, ka_ref, va_ref, m_ref, acc_ref, s_ref):
    c = pl.program_id(2)
    qb = SLC_Q_BLOCK
    t0 = c * qb
    rows = GQA_REP * qb
    tk = SLC_KEY_TILE
    nb = tk // SLC_BLOCK
    t_len = k_ref.shape[0]

    @pl.when(c == 0)
    def _():
        pos = lax.broadcasted_iota(jnp.int32, (t_len, LANES), 0)
        lane = lax.broadcasted_iota(jnp.int32, (t_len, LANES), 1)
        off = pos & (tk - 1)
        blk = (pos >> 6) & (nb - 1)
        aug = jnp.where(lane < nb, (blk == lane).astype(F32),
                        jnp.where(lane < nb + 3, ((off >> 3) << 3).astype(F32),
                                  jnp.where(lane < nb + 6, (off & 7).astype(F32), 0.0)))
        ka_ref[:, :HEAD_DIM] = k_ref[...]
        ka_ref[:, HEAD_DIM:] = aug.astype(BF16)
        va_ref[:, :HEAD_DIM] = v_ref[...]
        va_ref[:, HEAD_DIM:] = (lane == 0).astype(BF16)

    for r in range(GQA_REP):
        qa_ref[r * qb:(r + 1) * qb, :HEAD_DIM] = q_ref[:, r * HEAD_DIM:(r + 1) * HEAD_DIM]
    m_ref[...] = jnp.full(m_ref.shape, NEG, F32)
    acc_ref[...] = jnp.zeros(acc_ref.shape, F32)
    sb = sb_ref[0]
    scol = scol_ref[0]
    t_col = t0 + (lax.broadcasted_iota(jnp.int32, (rows, 1), 0) & (qb - 1))
    k_iota = lax.broadcasted_iota(jnp.int32, (1, tk), 1)
    p_row = lax.broadcasted_iota(jnp.int32, (SLC_LANES, LANES), 0)
    p_col = lax.broadcasted_iota(jnp.int32, (SLC_LANES, LANES), 1)

    def scores(kt):
        k0 = pl.multiple_of(kt * tk, tk)
        pick = ((p_row == kt * nb + p_col) & (p_col < nb)).astype(BF16)
        sbt = _dot(sb, pick).astype(BF16)
        for r in range(GQA_REP):
            qa_ref[r * qb:(r + 1) * qb, HEAD_DIM:] = sbt + srow_ref[0, r:r + 1, :]
        return _dot_nt(qa_ref[...], ka_ref[pl.ds(k0, tk), :])

    def softmax_pv(kt, s, causal):
        k0 = pl.multiple_of(kt * tk, tk)
        width = s.shape[1]
        if causal:
            s = jnp.where(k0 + k_iota[:, :width] <= t_col, s, NEG)
        shift = scol * (k0 - t0).astype(F32)
        m_old = m_ref[...]
        m_new = jnp.maximum(m_old, jnp.max(s, axis=-1, keepdims=True) + shift)
        alpha = jnp.exp2(m_old - m_new)
        p = jnp.exp2(s - (m_new - shift))
        acc_ref[...] = alpha * acc_ref[...] + _dot(p.astype(BF16), va_ref[pl.ds(k0, width), :])
        m_ref[...] = m_new

    n_below = lax.div(t0, tk)
    n_first = jnp.minimum(jnp.maximum(lax.div(t0 - reach_ref[pl.program_id(1)], tk), 0), n_below)
    s_ref[...] = scores(n_first)

    def body(kt, carry):
        s_cur = s_ref[...]
        s_next = scores(kt + 1)
        softmax_pv(kt, s_cur, False)
        s_ref[...] = s_next
        return carry

    lax.fori_loop(n_first, n_below, body, 0)
    groups = lax.div(t0 - n_below * tk, qb) + 1
    for j in range(1, tk // qb + 1):
        @pl.when(groups == j)
        def _(j=j):
            softmax_pv(n_below, s_ref[:, :j * qb], True)

    gates = jax.nn.sigmoid(gate_ref[0])
    acc = acc_ref[...]
    o = acc[:, :HEAD_DIM] / acc[:, HEAD_DIM:HEAD_DIM + 1]
    for r in range(GQA_REP):
        gate = gates[:, GQA_REP + r:GQA_REP + r + 1]
        o_ref[:, r * HEAD_DIM:(r + 1) * HEAD_DIM] = (o[r * qb:(r + 1) * qb] * gate).astype(o_ref.dtype)


def _slope_pieces(slopes):
    s_hi = slopes.astype(BF16)
    s_mid = (slopes - s_hi.astype(F32)).astype(BF16)
    s_lo = (slopes - s_hi.astype(F32) - s_mid.astype(F32)).astype(BF16)
    return jnp.stack([s_hi, s_mid, s_lo, s_hi, s_mid, s_lo], axis=1)


def _slope_rows(pieces, first_lane):
    rows = jnp.pad(pieces, ((0, 0), (first_lane, LANES - first_lane - pieces.shape[1]))).reshape(N_KV, GQA_REP, LANES)
    return jnp.pad(rows, ((0, 0), (0, 8 - GQA_REP), (0, 0)))


def _alibi_reach(slopes, gain_q, gain_k):
    bound = jnp.max(jnp.abs(gain_q)) * jnp.max(jnp.abs(gain_k)) * (HEAD_DIM ** 0.5 * LOG2E * 1.01)
    min_slope = jnp.min(slopes.reshape(N_KV, GQA_REP), axis=1)
    reach = jnp.ceil((2.0 * bound + EXP2_UNDERFLOW) / min_slope)
    return jnp.minimum(reach, 2.0 ** 30).astype(jnp.int32)


def _slc_attention(slopes, reach, proj, selbias, gates_r, *, b, t, k_col, v_col):
    n = b * t
    qb = SLC_Q_BLOCK
    nqb = t // qb
    rows = GQA_REP * qb
    nb = SLC_KEY_TILE // SLC_BLOCK
    rowblk = lambda bi, gi, c: bi * nqb + c
    srow = _slope_rows(_slope_pieces(slopes), nb)
    scol = jnp.repeat(slopes.reshape(N_KV, GQA_REP), qb, axis=1).reshape(N_KV, rows, 1)
    return pl.pallas_call(
        _slc_attn_kernel,
        grid=(b, N_KV, nqb),
        in_specs=[
            pl.BlockSpec(memory_space=pltpu.SMEM),
            pl.BlockSpec((qb, GQA_REP * HEAD_DIM), lambda bi, gi, c: (rowblk(bi, gi, c), gi)),
            pl.BlockSpec((t, HEAD_DIM), lambda bi, gi, c: (bi, k_col + gi)),
            pl.BlockSpec((t, HEAD_DIM), lambda bi, gi, c: (bi, v_col + gi)),
            pl.BlockSpec((1, qb, SLC_LANES), lambda bi, gi, c: (gi, rowblk(bi, gi, c), 0)),
            pl.BlockSpec((1, qb, LANES), lambda bi, gi, c: (gi, rowblk(bi, gi, c), 0)),
            pl.BlockSpec((1, 8, LANES), lambda bi, gi, c: (gi, 0, 0)),
            pl.BlockSpec((1, rows, 1), lambda bi, gi, c: (gi, 0, 0)),
        ],
        out_specs=pl.BlockSpec((qb, GQA_REP * HEAD_DIM), lambda bi, gi, c: (rowblk(bi, gi, c), gi)),
        out_shape=jax.ShapeDtypeStruct((n, N_HEADS * HEAD_DIM), BF16),
        scratch_shapes=[
            pltpu.VMEM((rows, 2 * HEAD_DIM), BF16),
            pltpu.VMEM((t, 2 * HEAD_DIM), BF16),
            pltpu.VMEM((t, 2 * HEAD_DIM), BF16),
            pltpu.VMEM((rows, 1), F32),
            pltpu.VMEM((rows, 2 * HEAD_DIM), F32),
            pltpu.VMEM((rows, SLC_KEY_TILE), F32),
        ],
        compiler_params=_params(("parallel", "parallel", "arbitrary")),
        name="nsa_slc_attn",
    )(reach, proj, proj, proj, selbias, gates_r, srow, scol)


def _sb_attn_kernel(q_ref, k_ref, v_ref, o_ref, acc_ref, carry_ref):
    ts = SB_TILE
    c = pl.program_id(2)
    acc_ref[...] = jnp.zeros(acc_ref.shape, F32)
    carry_ref[...] = jnp.zeros(carry_ref.shape, F32)
    r_idx = lax.broadcasted_iota(jnp.int32, (ts, 1), 0)
    c_idx = lax.broadcasted_iota(jnp.int32, (1, ts), 1)
    tri = (r_idx > c_idx).astype(BF16)
    diag = c_idx < r_idx

    def tile(jt, valid):
        k0 = pl.multiple_of(jt * ts, ts)
        heads = range(SB_HEADS_PER_STEP)
        cols = [slice(e * HEAD_DIM, (e + 1) * HEAD_DIM) for e in heads]
        carries = [carry_ref[e] for e in heads]
        zs = [_dot_nt(q_ref[:, cols[e]], k_ref[pl.ds(k0, ts), cols[e]]) for e in heads]
        lfs = []
        for e in heads:
            lf = -(jnp.maximum(zs[e], 0.0) + jnp.log2(1.0 + jnp.exp2(-jnp.abs(zs[e]))))
            lfs.append(lf if valid is None else jnp.where(valid, lf, 0.0))
        afters = []
        for e in heads:
            hi = lfs[e].astype(BF16)
            lo = (lfs[e] - hi.astype(F32)).astype(BF16)
            afters.append(_dot(hi, tri) + _dot(lo, tri))
        avs = []
        for e in heads:
            a = jnp.exp2(zs[e] + lfs[e] + afters[e] + carries[e])
            if valid is not None:
                a = jnp.where(valid, a, 0.0)
            avs.append(_dot(a.astype(BF16), v_ref[pl.ds(k0, ts), cols[e]]))
        for e in heads:
            acc_ref[e] += avs[e]
            carry_ref[e] = carries[e] + jnp.sum(lfs[e], axis=-1, keepdims=True)

    def live():
        return (jnp.max(carry_ref[...]) > SB_UNDERFLOW * LOG2E).astype(jnp.int32)

    tile(c, diag)

    def cond(state):
        jt, go = state
        return (jt >= 0) & (go > 0)

    def body(state):
        jt, _ = state
        tile(jt, None)
        return jt - 1, live()

    lax.while_loop(cond, body, (c - 1, live()))
    for e in range(SB_HEADS_PER_STEP):
        o_ref[:, e * HEAD_DIM:(e + 1) * HEAD_DIM] = acc_ref[e].astype(o_ref.dtype)


def _sb_attention(proj, *, b, t):
    n = b * t
    ts = SB_TILE
    nq = t // ts
    hs = SB_HEADS_PER_STEP
    width = hs * HEAD_DIM
    nhp = N_HEADS // hs
    return pl.pallas_call(
        _sb_attn_kernel,
        grid=(b, nhp, nq),
        in_specs=[
            pl.BlockSpec((ts, width), lambda bi, h, c: (bi * nq + c, h)),
            pl.BlockSpec((t, width), lambda bi, h, c: (bi, nhp + h), pipeline_mode=pl.Buffered(1)),
            pl.BlockSpec((t, width), lambda bi, h, c: (bi, 2 * nhp + h), pipeline_mode=pl.Buffered(1)),
        ],
        out_specs=pl.BlockSpec((ts, width), lambda bi, h, c: (bi * nq + c, h)),
        out_shape=jax.ShapeDtypeStruct((n, N_HEADS * HEAD_DIM), BF16),
        scratch_shapes=[pltpu.VMEM((hs, ts, HEAD_DIM), F32), pltpu.VMEM((hs, ts, 1), F32)],
        compiler_params=_params(("parallel", "parallel", "arbitrary")),
        name="sb_attn",
    )(proj, proj, proj)


def _row_copy(src_hbm, dst_ref, sem, src_row, dst_row):
    return pltpu.make_async_copy(src_hbm.at[pl.ds(src_row, 1), :], dst_ref.at[pl.ds(dst_row, 1), :], sem)


def _row_gather_start(src_hbm, dst_ref, sem, idx_ref, base, count):
    def body(r, carry):
        _row_copy(src_hbm, dst_ref, sem, idx_ref[base + r], r).start()
        return carry

    lax.fori_loop(0, count, body, 0, unroll=8)


def _row_gather_wait(src_hbm, dst_ref, sem):
    pltpu.make_async_copy(src_hbm.at[pl.ds(0, dst_ref.shape[0]), :], dst_ref, sem).wait()


def _expert_kernel(src_ref, te_ref, nu_ref, x_hbm, g_ref, wg_ref, wu_ref, wd_ref, o_ref, xbuf, wgb, wub, wdb, sem):
    tm = MOE_TILE
    i = pl.program_id(0)
    nt = pl.num_programs(0)
    n_used = nu_ref[0]
    slot = i % MOE_SLOTS
    nxt = (i + 2) % MOE_SLOTS
    next_base = jnp.where(i + 2 < nt, i + 2, i + 2 - nt) * tm

    @pl.when((i < n_used) & ((i == 0) | (te_ref[i] != te_ref[jnp.maximum(i - 1, 0)])))
    def _():
        wgb[...] = wg_ref[0, 0].astype(BF16)
        wub[...] = wu_ref[0, 0].astype(BF16)
        wdb[...] = wd_ref[0, 0].astype(BF16)

    @pl.when(i == 0)
    def _():
        _row_gather_start(x_hbm, xbuf.at[0], sem.at[0], src_ref, 0, tm)
        _row_gather_start(x_hbm, xbuf.at[1], sem.at[1], src_ref, tm, tm)

    @pl.when(i < n_used)
    def _():
        _row_gather_wait(x_hbm, xbuf.at[slot], sem.at[slot])
        x = xbuf[slot]
        ms = jnp.mean(x * x, axis=-1, keepdims=True)
        h = (x * lax.rsqrt(ms + EPS) * g_ref[...]).astype(BF16)
        for r in range(tm):
            _row_copy(x_hbm, xbuf.at[nxt], sem.at[nxt], src_ref[next_base + r], r).start()
        gate = _dot(h, wgb[...])
        up = _dot(h, wub[...])
        act = (gate * jax.nn.sigmoid(gate) * up).astype(BF16)
        o_ref[...] = _dot(act, wdb[...])

    @pl.when(i >= n_used)
    def _():
        o_ref[...] = jnp.zeros(o_ref.shape, o_ref.dtype)

    @pl.when(i == n_used - 1)
    def _():
        prev = (i + 1) % MOE_SLOTS
        _row_gather_wait(x_hbm, xbuf.at[prev], sem.at[prev])
        _row_gather_wait(x_hbm, xbuf.at[nxt], sem.at[nxt])


def _experts(src_tok, tile_expert, n_used, x, g, wg, wu, wd, layer):
    n, d = x.shape
    tm = MOE_TILE
    p_rows = src_tok.shape[0]
    de = wg.shape[3]
    grid_spec = pltpu.PrefetchScalarGridSpec(
        num_scalar_prefetch=3,
        grid=(p_rows // tm,),
        in_specs=[
            pl.BlockSpec(memory_space=pl.ANY),
            pl.BlockSpec((1, d), lambda i, s, te, nu: (0, 0)),
            pl.BlockSpec((1, 1, d, de), lambda i, s, te, nu: (layer, te[i], 0, 0)),
            pl.BlockSpec((1, 1, d, de), lambda i, s, te, nu: (layer, te[i], 0, 0)),
            pl.BlockSpec((1, 1, de, d), lambda i, s, te, nu: (layer, te[i], 0, 0)),
        ],
        out_specs=pl.BlockSpec((tm, d), lambda i, s, te, nu: (i, 0)),
        scratch_shapes=[
            pltpu.VMEM((MOE_SLOTS, tm, d), F32),
            pltpu.VMEM((d, de), BF16),
            pltpu.VMEM((d, de), BF16),
            pltpu.VMEM((de, d), BF16),
            pltpu.SemaphoreType.DMA((MOE_SLOTS,)),
        ],
    )
    return pl.pallas_call(
        _expert_kernel,
        grid_spec=grid_spec,
        out_shape=jax.ShapeDtypeStruct((p_rows, d), F32),
        compiler_params=_params(("arbitrary",)),
        name="moe_experts",
    )(src_tok, tile_expert, n_used, x, g.reshape(1, d), wg, wu, wd)


def _combine_kernel(pos_ref, x_ref, rt_ref, y_hbm, o_ref, ybuf, sem):
    tm = x_ref.shape[0]
    i = pl.program_id(0)
    nt = pl.num_programs(0)

    def start(tile, slot, unrolled):
        base = 2 * tile * tm
        for s in range(2):
            if unrolled:
                for r in range(tm):
                    _row_copy(y_hbm, ybuf.at[slot, s], sem.at[slot], pos_ref[base + 2 * r + s], r).start()
            else:
                def body(r, carry):
                    _row_copy(y_hbm, ybuf.at[slot, s], sem.at[slot], pos_ref[base + 2 * r + s], r).start()
                    return carry

                lax.fori_loop(0, tm, body, 0, unroll=8)

    @pl.when(i == 0)
    def _():
        start(0, 0, False)

    @pl.when(i + 1 < nt)
    def _():
        start(i + 1, (i + 1) % 2, True)

    slot = i % 2
    for s in range(2):
        _row_gather_wait(y_hbm, ybuf.at[slot, s], sem.at[slot])
    rt = rt_ref[...]
    o_ref[...] = x_ref[...] + rt[:, 2:3] * ybuf[slot, 0] + rt[:, 3:4] * ybuf[slot, 1]


def _combine(pos, x, routing, ys, *, tm):
    n, d = x.shape
    grid_spec = pltpu.PrefetchScalarGridSpec(
        num_scalar_prefetch=1,
        grid=(n // tm,),
        in_specs=[
            pl.BlockSpec((tm, d), lambda i, p: (i, 0)),
            pl.BlockSpec((tm, LANES), lambda i, p: (i, 0)),
            pl.BlockSpec(memory_space=pl.ANY),
        ],
        out_specs=pl.BlockSpec((tm, d), lambda i, p: (i, 0)),
        scratch_shapes=[pltpu.VMEM((2, 2, tm, d), F32), pltpu.SemaphoreType.DMA((2,))],
    )
    return pl.pallas_call(
        _combine_kernel,
        grid_spec=grid_spec,
        out_shape=jax.ShapeDtypeStruct((n, d), F32),
        compiler_params=_params(("arbitrary",)),
        name="moe_combine",
    )(pos, x, routing, ys)


def _source_tokens_kernel(pos_ref, src_ref):
    def zero(q, carry):
        src_ref[q] = 0
        return carry

    def put(p, carry):
        src_ref[pos_ref[p]] = p >> 1
        return carry

    lax.fori_loop(0, src_ref.shape[0], zero, 0, unroll=16)
    lax.fori_loop(0, pos_ref.shape[0], put, 0, unroll=16)


def _source_tokens(pos, p_rows):
    return pl.pallas_call(
        _source_tokens_kernel,
        in_specs=[pl.BlockSpec(memory_space=pltpu.SMEM)],
        out_specs=pl.BlockSpec(memory_space=pltpu.SMEM),
        out_shape=jax.ShapeDtypeStruct((p_rows,), jnp.int32),
        name="moe_source_tokens",
    )(pos)


def _hier_moe(x, routing, g, w_gate, w_up, w_down, layer):
    n, d = x.shape
    tm = MOE_TILE

    e_flat = routing[:, :2].astype(jnp.int32).reshape(-1)
    onehot = (e_flat[:, None] == jnp.arange(N_EXPERTS, dtype=jnp.int32)[None, :]).astype(jnp.int32)
    counts = jnp.sum(onehot, axis=0)
    rank = jnp.sum((jnp.cumsum(onehot, axis=0) - onehot) * onehot, axis=1)
    padded = ((counts + tm - 1) // tm) * tm
    ends = jnp.cumsum(padded)
    pos = (ends - padded)[e_flat] + rank
    p_rows = 2 * n + N_EXPERTS * tm
    pos = pos.astype(jnp.int32)
    src_tok = _source_tokens(pos, p_rows)
    tile_start = jnp.arange(p_rows // tm, dtype=jnp.int32) * tm
    tile_expert = jnp.minimum(jnp.sum((ends[None, :] <= tile_start[:, None]).astype(jnp.int32), axis=1), N_EXPERTS - 1)
    n_used = (ends[-1] // tm).astype(jnp.int32).reshape(1)

    ys = _experts(src_tok, tile_expert, n_used, x, g, w_gate, w_up, w_down, layer)
    return _combine(pos, x, routing, ys, tm=256 if n % 256 == 0 else n)


def _alibi_slopes():
    return jnp.exp2(-8.0 * jnp.arange(1, N_HEADS + 1, dtype=F32) / N_HEADS)


def _cmp_to_slc(n_cmp_pad, n_slc):
    cs = np.arange(n_cmp_pad) * CMP_STRIDE
    ss = np.arange(SLC_LANES) * SLC_BLOCK
    lo = np.maximum(cs[:, None], ss[None, :])
    hi = np.minimum(cs[:, None] + CMP_BLOCK, ss[None, :] + SLC_BLOCK)
    m = np.maximum(hi - lo, 0).astype(np.float32) / CMP_BLOCK
    m[:, n_slc:] = 0.0
    return m


def _nsa_mixer(x, g_norm, w_in, qk_gain, cmp_pe, cmp_w1, cmp_w2, w_out, router, *, b, t):
    n, d = x.shape
    q_dim = N_HEADS * HEAD_DIM
    kv_dim = N_KV * HEAD_DIM
    scale = HEAD_DIM ** -0.5 * LOG2E
    main = q_dim + 6 * kv_dim
    ones = jnp.ones((kv_dim,), F32)
    tile4 = lambda v: jnp.tile(v, N_KV)
    cgain = jnp.concatenate([jnp.tile(qk_gain[0], N_HEADS) * scale, ones, ones, tile4(qk_gain[2]), ones, tile4(qk_gain[3]), ones])
    tiles_per = lambda cols: cols // PROJ_COLS
    q_tiles = tiles_per(q_dim)
    kv_tiles = tiles_per(kv_dim)
    norm_tiles = list(range(q_tiles)) + [q_tiles + 2 * kv_tiles + j for j in range(kv_tiles)] + [q_tiles + 4 * kv_tiles + j for j in range(kv_tiles)]
    n_gate = 3 * N_HEADS
    w_gate = jnp.pad(w_in[:, main:], ((0, 0), (0, LANES - n_gate))).astype(BF16)
    cmp_tiles = [q_tiles + j for j in range(2 * kv_tiles)]
    proj, gate_logits, kv_cmp = _norm_matmul(x, g_norm, w_in[:, :main].astype(BF16), cgain, norm_tiles, extra_w=w_gate, f32_tiles=cmp_tiles)
    gr = gate_logits[:, :n_gate].reshape(n, 3, N_KV, GQA_REP).transpose(2, 0, 1, 3).reshape(N_KV, n, 3 * GQA_REP)
    gates_r = jnp.pad(gr, ((0, 0), (0, 0), (0, LANES - 3 * GQA_REP)))

    nc = t // CMP_STRIDE
    half = CMP_BLOCK // 2
    w1cat = jnp.concatenate([cmp_w1[:, :half], cmp_w1[:, half:]], axis=-1)
    pe_rows = jnp.pad(jnp.stack([cmp_pe[:, :half], cmp_pe[:, half:]], axis=2), ((0, 0), (0, 0), (0, 6), (0, 0)))
    cmp_gain = jnp.stack([qk_gain[1], jnp.ones((HEAD_DIM,), F32)]).reshape(2, 1, HEAD_DIM)
    cmp_kv = _compress(kv_cmp, w1cat.astype(BF16), pe_rows.astype(BF16), cmp_w2.astype(BF16), cmp_gain, b=b, t=t)

    slopes = _alibi_slopes() * LOG2E
    pieces = _slope_pieces(slopes)
    pad_flag = jnp.full((N_HEADS, 1), NEG, F32).astype(BF16)
    srow = _slope_rows(jnp.concatenate([pieces, pad_flag], axis=1), 0)
    n_slc = t // SLC_BLOCK
    msel = jnp.asarray(_cmp_to_slc(nc, n_slc), BF16)
    q_blocks = q_dim // HEAD_DIM
    o_c, selbias = _cmp_attention(proj, cmp_kv, gates_r, msel, srow, b=b, t=t, top_k=min(SLC_TOPK, n_slc))
    reach = _alibi_reach(slopes, qk_gain[0], qk_gain[2])
    o_s = _slc_attention(slopes, reach, proj, selbias, gates_r, b=b, t=t, k_col=q_blocks + 2 * N_KV, v_col=q_blocks + 3 * N_KV)
    o_w = _win_attention(proj, gates_r, srow, b=b, t=t, k_col=q_blocks + 4 * N_KV, v_col=q_blocks + 5 * N_KV)
    return _proj_residual([o_c, o_s, o_w], w_out.astype(BF16), x, router)


def _sb_mixer(x, g_norm, w_in, w_out, router, *, b, t):
    q_dim = N_HEADS * HEAD_DIM
    scale = HEAD_DIM ** -0.5 * LOG2E
    cgain = jnp.concatenate([jnp.full((q_dim,), scale, F32), jnp.ones((2 * q_dim,), F32)])
    proj = _norm_matmul(x, g_norm, w_in.astype(BF16), cgain, ())
    o = _sb_attention(proj, b=b, t=t)
    return _proj_residual([o], w_out.astype(BF16), x, router)


def kernel(x, norm_mix, norm_ffn, nsa_w_in, nsa_qk_gain, nsa_cmp_pe, nsa_cmp_w1, nsa_cmp_w2, nsa_w_out, sb_w_in, sb_w_out, moe_w_group, moe_b_group, moe_w_router, moe_b_router, moe_w_gate, moe_w_up, moe_w_down):
    b, t, d = x.shape
    depth = norm_mix.shape[0]
    xf = x.reshape(b * t, d)
    for i in range(depth):
        j = i // 2
        router = _router_params(norm_ffn[i], moe_w_group[i], moe_b_group[i], moe_w_router[i], moe_b_router[i])
        if i % 2 == 0:
            xf, routing = _nsa_mixer(xf, norm_mix[i], nsa_w_in[j], nsa_qk_gain[j], nsa_cmp_pe[j], nsa_cmp_w1[j], nsa_cmp_w2[j], nsa_w_out[j], router, b=b, t=t)
        else:
            xf, routing = _sb_mixer(xf, norm_mix[i], sb_w_in[j], sb_w_out[j], router, b=b, t=t)
        xf = _hier_moe(xf, routing, norm_ffn[i], moe_w_gate, moe_w_up, moe_w_down, i)
    return xf.reshape(b, t, d)
```

```python
import functools

import jax
import jax.numpy as jnp
import numpy as np
from jax import lax
from jax.experimental import pallas as pl
from jax.experimental.pallas import tpu as pltpu

F32 = jnp.float32
BF16 = jnp.bfloat16

HEAD_DIM = 128
N_HEADS = 16
N_KV = 4
GQA_REP = N_HEADS // N_KV
CMP_BLOCK = 32
CMP_STRIDE = 16
SLC_BLOCK = 64
SLC_TOPK = 16
WINDOW = 512
Q_BLOCK = 256
WIN_SUB = 2
N_GROUPS = 4
EXPERTS_PER_GROUP = 8
N_EXPERTS = N_GROUPS * EXPERTS_PER_GROUP
EPS = 1e-6
NEG = -1e30
LANES = 128
VMEM_LIMIT = 56 * 1024 * 1024

PROJ_ROWS = 512
PROJ_COLS = 512
CMP_Q_BLOCK = 256
CMP_SUB = 2
SLC_Q_BLOCK = 256
SLC_LANES = 128
SB_TILE = 256
SB_SWEEP_TILE = 256
SB_HEADS_PER_STEP = 8
SLC_KEY_TILE = 1024
LOG2E = 1.4426950408889634
SB_UNDERFLOW = -105.0
MOE_TILE = 256
MOE_SLOTS = 3
EXP2_UNDERFLOW = 152.0
AUG_PAD_LANE = 6


def _dot(a, b):
    return jnp.dot(a, b, preferred_element_type=F32)


def _dot_nt(a, b):
    return lax.dot_general(a, b, (((1,), (1,)), ((), ())), preferred_element_type=F32)


def _params(sem):
    return pltpu.CompilerParams(dimension_semantics=sem, vmem_limit_bytes=VMEM_LIMIT)


def _resident(shape):
    return pl.BlockSpec(shape, lambda *_: (0,) * len(shape), pipeline_mode=pl.Buffered(1))


def _norm_matmul_kernel(x_ref, g_ref, w_ref, cg_ref, *rest, norm_tiles, f32_tiles, has_extra):
    if has_extra:
        wx_ref, o_ref, ox_ref, of_ref = rest
    else:
        (o_ref,) = rest
    x = x_ref[...]
    ms = jnp.mean(x * x, axis=-1, keepdims=True)
    h = (x * lax.rsqrt(ms + EPS) * g_ref[...]).astype(BF16)
    tn = PROJ_COLS
    for j in range(w_ref.shape[1] // tn):
        y = _dot(h, w_ref[:, j * tn:(j + 1) * tn])
        if j in f32_tiles:
            k = f32_tiles.index(j)
            of_ref[:, k * tn:(k + 1) * tn] = y
        for hh in range(tn // LANES):
            yh = y[:, hh * LANES:(hh + 1) * LANES]
            if j in norm_tiles:
                yh = yh * lax.rsqrt(jnp.mean(yh * yh, axis=-1, keepdims=True) + EPS)
            sl = slice(j * tn + hh * LANES, j * tn + (hh + 1) * LANES)
            o_ref[:, sl] = (yh * cg_ref[:, sl]).astype(o_ref.dtype)
    if has_extra:
        ox_ref[...] = _dot(h, wx_ref[...])


def _norm_matmul(x, g, w, cgain, norm_tiles, extra_w=None, f32_tiles=()):
    n, d = x.shape
    m = w.shape[1]
    tm = PROJ_ROWS
    has_extra = extra_w is not None
    in_specs = [pl.BlockSpec((tm, d), lambda i: (i, 0)), _resident((1, d)), _resident((d, m)), _resident((1, m))]
    out_specs = [pl.BlockSpec((tm, m), lambda i: (i, 0))]
    out_shape = [jax.ShapeDtypeStruct((n, m), BF16)]
    args = [x, g.reshape(1, d), w, cgain.reshape(1, m)]
    if has_extra:
        mx = extra_w.shape[1]
        mf = len(f32_tiles) * PROJ_COLS
        in_specs.append(_resident((d, mx)))
        out_specs += [pl.BlockSpec((tm, mx), lambda i: (i, 0)), pl.BlockSpec((tm, mf), lambda i: (i, 0))]
        out_shape += [jax.ShapeDtypeStruct((n, mx), F32), jax.ShapeDtypeStruct((n, mf), F32)]
        args.append(extra_w)
    out = pl.pallas_call(
        functools.partial(_norm_matmul_kernel, norm_tiles=tuple(norm_tiles), f32_tiles=tuple(f32_tiles), has_extra=has_extra),
        grid=(n // tm,),
        in_specs=in_specs,
        out_specs=out_specs,
        out_shape=out_shape,
        compiler_params=_params(("parallel",)),
        name="norm_matmul",
    )(*args)
    return out if has_extra else out[0]


def _first_max(vals, lane):
    m = jnp.max(vals, axis=-1, keepdims=True)
    idx = jnp.min(jnp.where(vals == m, lane, LANES), axis=-1, keepdims=True)
    return m, idx


def _route(x, g_ref, w_ref, b_ref):
    ms = jnp.mean(x * x, axis=-1, keepdims=True)
    h = x * lax.rsqrt(ms + EPS) * g_ref[...]
    h_hi = h.astype(BF16)
    h_lo = (h - h_hi.astype(F32)).astype(BF16)
    logits = _dot(h_hi, w_ref[0]) + _dot(h_hi, w_ref[1]) + _dot(h_lo, w_ref[0]) + b_ref[...]
    lane = lax.broadcasted_iota(jnp.int32, (1, LANES), 1)
    gl = jnp.where(lane < N_GROUPS, logits, NEG)
    ge = jnp.exp(gl - jnp.max(gl, axis=-1, keepdims=True))
    pg_all = ge / jnp.sum(ge, axis=-1, keepdims=True)
    pg, gsel = _first_max(jnp.where(lane < N_GROUPS, pg_all, -1.0), lane)
    lo = N_GROUPS + gsel * EXPERTS_PER_GROUP
    in_group = (lane >= lo) & (lane < lo + EXPERTS_PER_GROUP)
    el = jnp.where(in_group, logits, NEG)
    ee = jnp.exp(el - jnp.max(el, axis=-1, keepdims=True))
    pe_all = jnp.where(in_group, ee / jnp.sum(ee, axis=-1, keepdims=True), -1.0)
    p1, i1 = _first_max(pe_all, lane)
    p2, i2 = _first_max(jnp.where(lane == i1, -1.0, pe_all), lane)
    denom = p1 + p2
    w1 = pg * p1 / denom
    w2 = pg * p2 / denom
    e1 = (i1 - N_GROUPS).astype(F32)
    e2 = (i2 - N_GROUPS).astype(F32)
    return jnp.where(lane == 0, e1, jnp.where(lane == 1, e2, jnp.where(lane == 2, w1, jnp.where(lane == 3, w2, 0.0))))


def _proj_residual_kernel(*refs, n_in):
    a_refs = refs[:n_in]
    w_ref, r_ref, g_ref, wr_ref, br_ref, o_ref, rt_ref = refs[n_in:]
    if n_in == 1:
        a = a_refs[0][...]
    else:
        acc = a_refs[0][...].astype(F32)
        for a_ref in a_refs[1:]:
            acc = acc + a_ref[...].astype(F32)
        a = acc.astype(BF16)
    tn = PROJ_COLS
    for j in range(w_ref.shape[1] // tn):
        sl = slice(j * tn, (j + 1) * tn)
        o_ref[:, sl] = r_ref[:, sl] + _dot(a, w_ref[:, sl])
    rt_ref[...] = _route(o_ref[...], g_ref, wr_ref, br_ref)


def _proj_residual(branches, w, res, router):
    n, k = branches[0].shape
    m = w.shape[1]
    tm = PROJ_ROWS
    n_in = len(branches)
    g_ffn, w_split, b_r = router
    return pl.pallas_call(
        functools.partial(_proj_residual_kernel, n_in=n_in),
        grid=(n // tm,),
        in_specs=[pl.BlockSpec((tm, k), lambda i: (i, 0)) for _ in range(n_in)]
        + [_resident((k, m)), pl.BlockSpec((tm, m), lambda i: (i, 0)), _resident((1, m)), _resident(w_split.shape), _resident((1, LANES))],
        out_specs=[pl.BlockSpec((tm, m), lambda i: (i, 0)), pl.BlockSpec((tm, LANES), lambda i: (i, 0))],
        out_shape=[jax.ShapeDtypeStruct((n, m), F32), jax.ShapeDtypeStruct((n, LANES), F32)],
        compiler_params=_params(("parallel",)),
        name="proj_residual",
    )(*branches, w, res, g_ffn.reshape(1, m), w_split, b_r)


def _router_params(g_ffn, w_group, b_group, w_router, b_router):
    n_logit = N_GROUPS + N_EXPERTS
    w_r = jnp.pad(jnp.concatenate([w_group, w_router], axis=1), ((0, 0), (0, LANES - n_logit)))
    b_r = jnp.pad(jnp.concatenate([b_group, b_router]), (0, LANES - n_logit)).reshape(1, LANES)
    w_hi = w_r.astype(BF16)
    return g_ffn, jnp.stack([w_hi, (w_r - w_hi.astype(F32)).astype(BF16)]), b_r


def _cmp_kernel(kv_ref, w1_ref, pe_ref, w2_ref, gain_ref, o_ref):
    nc = kv_ref.shape[0] // CMP_STRIDE
    p = jnp.zeros((nc, 2 * HEAD_DIM), F32)
    pb = jnp.zeros((8, 2 * HEAD_DIM), F32)
    for l in range(CMP_STRIDE):
        w1 = w1_ref[0, l]
        p = p + _dot(kv_ref[pl.ds(l, nc, stride=CMP_STRIDE), :].astype(BF16), w1)
        pb = pb + _dot(pe_ref[0, l], w1)
    bias = pb[0:1, :HEAD_DIM] + pb[1:2, HEAD_DIM:]
    second = pltpu.roll(p[:, HEAD_DIM:], shift=nc - 1, axis=0)
    hid = jax.nn.gelu(p[:, :HEAD_DIM] + second + bias)
    c = _dot(hid.astype(BF16), w2_ref[0])
    nf = (pl.program_id(1) == 0).astype(F32)
    ms = jnp.mean(c * c, axis=-1, keepdims=True)
    fac = nf * lax.rsqrt(ms + EPS) + (1.0 - nf)
    o_ref[0, 0, 0] = (c * fac * gain_ref[0]).astype(o_ref.dtype)


def _compress(kv, w1cat, pe_rows, w2, gains, *, b, t):
    g, cs, hd = N_KV, CMP_STRIDE, HEAD_DIM
    nc = t // cs
    return pl.pallas_call(
        _cmp_kernel,
        grid=(b, 2, g),
        in_specs=[
            pl.BlockSpec((t, hd), lambda bi, k, gi: (bi, k * g + gi)),
            pl.BlockSpec((1, cs, hd, 2 * HEAD_DIM), lambda bi, k, gi: (k, 0, 0, 0)),
            pl.BlockSpec((1, cs, 8, hd), lambda bi, k, gi: (k, 0, 0, 0)),
            pl.BlockSpec((1, HEAD_DIM, HEAD_DIM), lambda bi, k, gi: (k, 0, 0)),
            pl.BlockSpec((1, 1, HEAD_DIM), lambda bi, k, gi: (k, 0, 0)),
        ],
        out_specs=pl.BlockSpec((1, 1, 1, nc, HEAD_DIM), lambda bi, k, gi: (bi, k, gi, 0, 0)),
        out_shape=jax.ShapeDtypeStruct((b, 2, g, nc, HEAD_DIM), BF16),
        compiler_params=_params(("parallel", "parallel", "parallel")),
        name="nsa_compress",
    )(kv, w1cat, pe_rows, w2, gains)


def _cmp_attn_kernel(q_ref, kc_ref, vc_ref, gate_ref, msel_ref, srow_ref, oc_ref, sb_ref, qa_ref, ka_ref, va_ref, *, top_k):
    c = pl.program_id(2)
    qb = CMP_Q_BLOCK
    t_step = c * (CMP_SUB * qb)
    rows = GQA_REP * qb
    nc = ka_ref.shape[0]
    subs = range(CMP_SUB)

    @pl.when(c == 0)
    def _():
        n_idx = lax.broadcasted_iota(jnp.int32, (nc, LANES), 0)
        lane = lax.broadcasted_iota(jnp.int32, (nc, LANES), 1)
        aug = jnp.where(lane < 3, ((n_idx >> 7) << 11).astype(F32), jnp.where(lane < 6, ((n_idx & 127) << 4).astype(F32), 0.0))
        ka_ref[:, :HEAD_DIM] = kc_ref[0, 0, 0]
        ka_ref[:, HEAD_DIM:] = aug.astype(BF16)
        va_ref[:, :HEAD_DIM] = vc_ref[0, 0, 0]
        va_ref[:, HEAD_DIM:] = (lane == 0).astype(BF16)

    for u in subs:
        for r in range(GQA_REP):
            dst = slice((u * GQA_REP + r) * qb, (u * GQA_REP + r + 1) * qb)
            qa_ref[dst, :HEAD_DIM] = q_ref[u * qb:(u + 1) * qb, r * HEAD_DIM:(r + 1) * HEAD_DIM]
            qa_ref[dst, HEAD_DIM:] = jnp.broadcast_to(srow_ref[0, r:r + 1, :], (qb, LANES))
    row_q = lax.broadcasted_iota(jnp.int32, (rows, 1), 0) & (qb - 1)
    lane_q = lax.broadcasted_iota(jnp.int32, (1, qb), 1)
    gates = jax.nn.sigmoid(gate_ref[0])

    def attend(k):
        ncol = k * LANES
        nrow = ncol * CMP_STRIDE // SLC_BLOCK
        cend = lax.broadcasted_iota(jnp.int32, (1, ncol), 1) * CMP_STRIDE + (CMP_BLOCK - 1)
        j_col = lax.broadcasted_iota(jnp.int32, (nrow, 1), 0)
        msel = msel_ref[:ncol, :]
        scores = [_dot_nt(qa_ref[u * rows:(u + 1) * rows, :], ka_ref[:ncol, :]) for u in subs]
        es, ms = [], []
        for u in subs:
            s = jnp.where(cend <= t_step + u * qb + row_q, scores[u], NEG)
            m = jnp.max(s, axis=-1, keepdims=True)
            ms.append(m)
            es.append(jnp.exp2(s - m))
        accs = [_dot(es[u].astype(BF16), va_ref[:ncol, :]) for u in subs]
        imps = []
        for u in subs:
            inv = jnp.where(ms[u] > 0.5 * NEG, 1.0 / accs[u][:, HEAD_DIM:HEAD_DIM + 1], 0.0)
            psum = jnp.zeros((qb, ncol), F32)
            for r in range(GQA_REP):
                rs = slice(r * qb, (r + 1) * qb)
                psum = psum + es[u][rs] * inv[rs]
                gate = gates[u * qb:(u + 1) * qb, r:r + 1]
                oc_ref[u * qb:(u + 1) * qb, r * HEAD_DIM:(r + 1) * HEAD_DIM] = (accs[u][rs, :HEAD_DIM] * (inv[rs] * gate)).astype(oc_ref.dtype)
            p_hi = psum.astype(BF16)
            rem = psum - p_hi.astype(F32)
            p_mid = rem.astype(BF16)
            p_lo = (rem - p_mid.astype(F32)).astype(BF16)
            imps.append(_dot(p_hi, msel) + _dot(p_mid, msel) + _dot(p_lo, msel))

        score, keep = [], []
        for u in subs:
            t_row = t_step + u * qb + lane_q
            forced = (j_col == 0) | (j_col == (t_row >> 6))
            causal = (j_col << 6) <= t_row
            score.append(jnp.where(forced, -NEG, jnp.where(causal, imps[u].T[:nrow], NEG)))
            keep.append(jnp.zeros((nrow, qb), F32))
        for _ in range(top_k):
            for u in subs:
                mx = jnp.max(score[u], axis=0, keepdims=True)
                first = jnp.min(jnp.where(score[u] == mx, j_col, SLC_LANES), axis=0, keepdims=True)
                hit = j_col == first
                keep[u] = jnp.where(hit, (mx > 0.5 * NEG).astype(F32), keep[u])
                score[u] = jnp.where(hit, -3e38, score[u])
        for u in subs:
            bias_t = jnp.where(keep[u] > 0.5, 0.0, NEG)
            if nrow < SLC_LANES:
                bias_t = jnp.concatenate([bias_t, jnp.full((SLC_LANES - nrow, qb), NEG, F32)], axis=0)
            sb_ref[0, u * qb:(u + 1) * qb, :] = bias_t.T.astype(sb_ref.dtype)

    n_variants = nc // LANES
    needed = jnp.minimum(lax.div(lax.div(t_step + CMP_SUB * qb - CMP_BLOCK, CMP_STRIDE), LANES) + 1, n_variants)
    for k in range(1, n_variants + 1):
        pl.when(needed == k)(functools.partial(attend, k))


def _cmp_attention(proj, cmp_kv, gates_r, msel, srow, *, b, t, top_k):
    n = b * t
    qb = CMP_SUB * CMP_Q_BLOCK
    nqb = t // qb
    rows = GQA_REP * qb
    nc = cmp_kv.shape[3]
    rowblk = lambda bi, gi, c: bi * nqb + c
    return pl.pallas_call(
        functools.partial(_cmp_attn_kernel, top_k=top_k),
        grid=(b, N_KV, nqb),
        in_specs=[
            pl.BlockSpec((qb, GQA_REP * HEAD_DIM), lambda bi, gi, c: (rowblk(bi, gi, c), gi)),
            pl.BlockSpec((1, 1, 1, nc, HEAD_DIM), lambda bi, gi, c: (bi, 0, gi, 0, 0)),
            pl.BlockSpec((1, 1, 1, nc, HEAD_DIM), lambda bi, gi, c: (bi, 1, gi, 0, 0)),
            pl.BlockSpec((1, qb, LANES), lambda bi, gi, c: (gi, rowblk(bi, gi, c), 0)),
            pl.BlockSpec((nc, SLC_LANES), lambda bi, gi, c: (0, 0)),
            pl.BlockSpec((1, 8, LANES), lambda bi, gi, c: (gi, 0, 0)),
        ],
        out_specs=[
            pl.BlockSpec((qb, GQA_REP * HEAD_DIM), lambda bi, gi, c: (rowblk(bi, gi, c), gi)),
            pl.BlockSpec((1, qb, SLC_LANES), lambda bi, gi, c: (gi, rowblk(bi, gi, c), 0)),
        ],
        out_shape=[
            jax.ShapeDtypeStruct((n, N_HEADS * HEAD_DIM), BF16),
            jax.ShapeDtypeStruct((N_KV, n, SLC_LANES), BF16),
        ],
        scratch_shapes=[
            pltpu.VMEM((rows, 2 * HEAD_DIM), BF16),
            pltpu.VMEM((nc, 2 * HEAD_DIM), BF16),
            pltpu.VMEM((nc, 2 * HEAD_DIM), BF16),
        ],
        compiler_params=_params(("parallel", "parallel", "arbitrary")),
        name="nsa_cmp_attn",
    )(proj, cmp_kv, cmp_kv, gates_r, msel, srow)


def _win_attn_kernel(q_ref, k_ref, v_ref, gate_ref, srow_ref, o_ref, qa_ref, ka_ref, va_ref):
    c = pl.program_id(2)
    t0 = pl.multiple_of(c * (WIN_SUB * Q_BLOCK), WIN_SUB * Q_BLOCK)
    rows = GQA_REP * Q_BLOCK
    span = Q_BLOCK + WINDOW
    t_len = k_ref.shape[0]

    @pl.when(c == 0)
    def _():
        pos = lax.broadcasted_iota(jnp.int32, (t_len, LANES), 0)
        lane = lax.broadcasted_iota(jnp.int32, (t_len, LANES), 1)
        aug = jnp.where(lane < 3, ((pos >> 6) << 6).astype(F32), jnp.where(lane < 6, (pos & 63).astype(F32), 0.0))
        pad_lane = lax.broadcasted_iota(jnp.int32, (WINDOW, LANES), 1)
        ka_ref[:WINDOW, :HEAD_DIM] = jnp.zeros((WINDOW, HEAD_DIM), BF16)
        ka_ref[:WINDOW, HEAD_DIM:] = (pad_lane == AUG_PAD_LANE).astype(BF16)
        ka_ref[WINDOW:, :HEAD_DIM] = k_ref[...]
        ka_ref[WINDOW:, HEAD_DIM:] = aug.astype(BF16)
        va_ref[:WINDOW, :] = jnp.zeros((WINDOW, 2 * HEAD_DIM), BF16)
        va_ref[WINDOW:, :HEAD_DIM] = v_ref[...]
        va_ref[WINDOW:, HEAD_DIM:] = (lane == 0).astype(BF16)

    i_col = lax.broadcasted_iota(jnp.int32, (rows, 1), 0) & (Q_BLOCK - 1)
    kk = lax.broadcasted_iota(jnp.int32, (1, Q_BLOCK), 1)
    gates = jax.nn.sigmoid(gate_ref[0])
    subs = range(WIN_SUB)
    for u in subs:
        for r in range(GQA_REP):
            dst = slice((u * GQA_REP + r) * Q_BLOCK, (u * GQA_REP + r + 1) * Q_BLOCK)
            qa_ref[dst, :HEAD_DIM] = q_ref[u * Q_BLOCK:(u + 1) * Q_BLOCK, r * HEAD_DIM:(r + 1) * HEAD_DIM]
            qa_ref[dst, HEAD_DIM:] = jnp.broadcast_to(srow_ref[0, r:r + 1, :], (Q_BLOCK, LANES))
    starts = [t0 + u * Q_BLOCK for u in subs]
    scores = [_dot_nt(qa_ref[u * rows:(u + 1) * rows, :], ka_ref[pl.ds(starts[u], span), :]) for u in subs]
    weights = []
    for u in subs:
        s = scores[u]
        first = jnp.where(kk > i_col, s[:, :Q_BLOCK], NEG)
        mid = s[:, Q_BLOCK:WINDOW]
        last = jnp.where(kk <= i_col, s[:, WINDOW:], NEG)
        m = jnp.maximum(jnp.maximum(jnp.max(first, axis=-1, keepdims=True), jnp.max(mid, axis=-1, keepdims=True)),
                        jnp.max(last, axis=-1, keepdims=True))
        weights.append([jnp.exp2(part - m).astype(BF16) for part in (first, mid, last)])
    accs = []
    for u in subs:
        p_first, p_mid, p_last = weights[u]
        accs.append(_dot(p_first, va_ref[pl.ds(starts[u], Q_BLOCK), :])
                    + _dot(p_mid, va_ref[pl.ds(starts[u] + Q_BLOCK, WINDOW - Q_BLOCK), :])
                    + _dot(p_last, va_ref[pl.ds(starts[u] + WINDOW, Q_BLOCK), :]))
    for u in subs:
        o = accs[u][:, :HEAD_DIM] / accs[u][:, HEAD_DIM:HEAD_DIM + 1]
        for r in range(GQA_REP):
            gate = gates[u * Q_BLOCK:(u + 1) * Q_BLOCK, 2 * GQA_REP + r:2 * GQA_REP + r + 1]
            o_ref[u * Q_BLOCK:(u + 1) * Q_BLOCK, r * HEAD_DIM:(r + 1) * HEAD_DIM] = (o[r * Q_BLOCK:(r + 1) * Q_BLOCK] * gate).astype(o_ref.dtype)


def _win_attention(proj, gates_r, srow, *, b, t, k_col, v_col):
    n = b * t
    step_q = WIN_SUB * Q_BLOCK
    nqb = t // step_q
    rows = GQA_REP * Q_BLOCK
    rowblk = lambda bi, gi, c: bi * nqb + c
    return pl.pallas_call(
        _win_attn_kernel,
        grid=(b, N_KV, nqb),
        in_specs=[
            pl.BlockSpec((step_q, GQA_REP * HEAD_DIM), lambda bi, gi, c: (rowblk(bi, gi, c), gi)),
            pl.BlockSpec((t, HEAD_DIM), lambda bi, gi, c: (bi, k_col + gi)),
            pl.BlockSpec((t, HEAD_DIM), lambda bi, gi, c: (bi, v_col + gi)),
            pl.BlockSpec((1, step_q, LANES), lambda bi, gi, c: (gi, rowblk(bi, gi, c), 0)),
            pl.BlockSpec((1, 8, LANES), lambda bi, gi, c: (gi, 0, 0)),
        ],
        out_specs=pl.BlockSpec((step_q, GQA_REP * HEAD_DIM), lambda bi, gi, c: (rowblk(bi, gi, c), gi)),
        out_shape=jax.ShapeDtypeStruct((n, N_HEADS * HEAD_DIM), BF16),
        scratch_shapes=[
            pltpu.VMEM((WIN_SUB * rows, 2 * HEAD_DIM), BF16),
            pltpu.VMEM((t + WINDOW, 2 * HEAD_DIM), BF16),
            pltpu.VMEM((t + WINDOW, 2 * HEAD_DIM), BF16),
        ],
        compiler_params=_params(("parallel", "parallel", "arbitrary")),
        name="nsa_win_attn",
    )(proj, proj, proj, gates_r, srow)


def _slc_attn_kernel(reach_ref, q_ref, k_ref, v_ref, sb_ref, gate_ref, srow_ref, scol_ref, o_ref, qa_ref, ka_ref, va_ref, m_ref, acc_ref, s_ref):
    c = pl.program_id(2)
    qb = SLC_Q_BLOCK
    t0 = c * qb
    rows = GQA_REP * qb
    tk = SLC_KEY_TILE
    nb = tk // SLC_BLOCK
    t_len = k_ref.shape[0]

    @pl.when(c == 0)
    def _():
        pos = lax.broadcasted_iota(jnp.int32, (t_len, LANES), 0)
        lane = lax.broadcasted_iota(jnp.int32, (t_len, LANES), 1)
        off = pos & (tk - 1)
        blk = (pos >> 6) & (nb - 1)
        aug = jnp.where(lane < nb, (blk == lane).astype(F32),
                        jnp.where(lane < nb + 3, ((off >> 3) << 3).astype(F32),
                                  jnp.where(lane < nb + 6, (off & 7).astype(F32), 0.0)))
        ka_ref[:, :HEAD_DIM] = k_ref[...]
        ka_ref[:, HEAD_DIM:] = aug.astype(BF16)
        va_ref[:, :HEAD_DIM] = v_ref[...]
        va_ref[:, HEAD_DIM:] = (lane == 0).astype(BF16)

    for r in range(GQA_REP):
        qa_ref[r * qb:(r + 1) * qb, :HEAD_DIM] = q_ref[:, r * HEAD_DIM:(r + 1) * HEAD_DIM]
    m_ref[...] = jnp.full(m_ref.shape, NEG, F32)
    acc_ref[...] = jnp.zeros(acc_ref.shape, F32)
    sb = sb_ref[0]
    scol = scol_ref[0]
    t_col = t0 + (lax.broadcasted_iota(jnp.int32, (rows, 1), 0) & (qb - 1))
    k_iota = lax.broadcasted_iota(jnp.int32, (1, tk), 1)
    p_row = lax.broadcasted_iota(jnp.int32, (SLC_LANES, LANES), 0)
    p_col = lax.broadcasted_iota(jnp.int32, (SLC_LANES, LANES), 1)

    def scores(kt):
        k0 = pl.multiple_of(kt * tk, tk)
        pick = ((p_row == kt * nb + p_col) & (p_col < nb)).astype(BF16)
        sbt = _dot(sb, pick).astype(BF16)
        for r in range(GQA_REP):
            qa_ref[r * qb:(r + 1) * qb, HEAD_DIM:] = sbt + srow_ref[0, r:r + 1, :]
        return _dot_nt(qa_ref[...], ka_ref[pl.ds(k0, tk), :])

    def softmax_pv(kt, s, causal):
        k0 = pl.multiple_of(kt * tk, tk)
        width = s.shape[1]
        if causal:
            s = jnp.where(k0 + k_iota[:, :width] <= t_col, s, NEG)
        shift = scol * (k0 - t0).astype(F32)
        m_old = m_ref[...]
        m_new = jnp.maximum(m_old, jnp.max(s, axis=-1, keepdims=True) + shift)
        alpha = jnp.exp2(m_old - m_new)
        p = jnp.exp2(s - (m_new - shift))
        acc_ref[...] = alpha * acc_ref[...] + _dot(p.astype(BF16), va_ref[pl.ds(k0, width), :])
        m_ref[...] = m_new

    n_below = lax.div(t0, tk)
    n_first = jnp.minimum(jnp.maximum(lax.div(t0 - reach_ref[pl.program_id(1)], tk), 0), n_below)
    s_ref[...] = scores(n_first)

    def body(kt, carry):
        s_cur = s_ref[...]
        s_next = scores(kt + 1)
        softmax_pv(kt, s_cur, False)
        s_ref[...] = s_next
        return carry

    lax.fori_loop(n_first, n_below, body, 0)
    groups = lax.div(t0 - n_below * tk, qb) + 1
    for j in range(1, tk // qb + 1):
        @pl.when(groups == j)
        def _(j=j):
            softmax_pv(n_below, s_ref[:, :j * qb], True)

    gates = jax.nn.sigmoid(gate_ref[0])
    acc = acc_ref[...]
    o = acc[:, :HEAD_DIM] / acc[:, HEAD_DIM:HEAD_DIM + 1]
    for r in range(GQA_REP):
        gate = gates[:, GQA_REP + r:GQA_REP + r + 1]
        o_ref[:, r * HEAD_DIM:(r + 1) * HEAD_DIM] = (o[r * qb:(r + 1) * qb] * gate).astype(o_ref.dtype)


def _slope_pieces(slopes):
    s_hi = slopes.astype(BF16)
    s_mid = (slopes - s_hi.astype(F32)).astype(BF16)
    s_lo = (slopes - s_hi.astype(F32) - s_mid.astype(F32)).astype(BF16)
    return jnp.stack([s_hi, s_mid, s_lo, s_hi, s_mid, s_lo], axis=1)


def _slope_rows(pieces, first_lane):
    rows = jnp.pad(pieces, ((0, 0), (first_lane, LANES - first_lane - pieces.shape[1]))).reshape(N_KV, GQA_REP, LANES)
    return jnp.pad(rows, ((0, 0), (0, 8 - GQA_REP), (0, 0)))


def _alibi_reach(slopes, gain_q, gain_k):
    bound = jnp.max(jnp.abs(gain_q)) * jnp.max(jnp.abs(gain_k)) * (HEAD_DIM ** 0.5 * LOG2E * 1.01)
    min_slope = jnp.min(slopes.reshape(N_KV, GQA_REP), axis=1)
    reach = jnp.ceil((2.0 * bound + EXP2_UNDERFLOW) / min_slope)
    return jnp.minimum(reach, 2.0 ** 30).astype(jnp.int32)


def _slc_attention(slopes, reach, proj, selbias, gates_r, *, b, t, k_col, v_col):
    n = b * t
    qb = SLC_Q_BLOCK
    nqb = t // qb
    rows = GQA_REP * qb
    nb = SLC_KEY_TILE // SLC_BLOCK
    rowblk = lambda bi, gi, c: bi * nqb + c
    srow = _slope_rows(_slope_pieces(slopes), nb)
    scol = jnp.repeat(slopes.reshape(N_KV, GQA_REP), qb, axis=1).reshape(N_KV, rows, 1)
    return pl.pallas_call(
        _slc_attn_kernel,
        grid=(b, N_KV, nqb),
        in_specs=[
            pl.BlockSpec(memory_space=pltpu.SMEM),
            pl.BlockSpec((qb, GQA_REP * HEAD_DIM), lambda bi, gi, c: (rowblk(bi, gi, c), gi)),
            pl.BlockSpec((t, HEAD_DIM), lambda bi, gi, c: (bi, k_col + gi)),
            pl.BlockSpec((t, HEAD_DIM), lambda bi, gi, c: (bi, v_col + gi)),
            pl.BlockSpec((1, qb, SLC_LANES), lambda bi, gi, c: (gi, rowblk(bi, gi, c), 0)),
            pl.BlockSpec((1, qb, LANES), lambda bi, gi, c: (gi, rowblk(bi, gi, c), 0)),
            pl.BlockSpec((1, 8, LANES), lambda bi, gi, c: (gi, 0, 0)),
            pl.BlockSpec((1, rows, 1), lambda bi, gi, c: (gi, 0, 0)),
        ],
        out_specs=pl.BlockSpec((qb, GQA_REP * HEAD_DIM), lambda bi, gi, c: (rowblk(bi, gi, c), gi)),
        out_shape=jax.ShapeDtypeStruct((n, N_HEADS * HEAD_DIM), BF16),
        scratch_shapes=[
            pltpu.VMEM((rows, 2 * HEAD_DIM), BF16),
            pltpu.VMEM((t, 2 * HEAD_DIM), BF16),
            pltpu.VMEM((t, 2 * HEAD_DIM), BF16),
            pltpu.VMEM((rows, 1), F32),
            pltpu.VMEM((rows, 2 * HEAD_DIM), F32),
            pltpu.VMEM((rows, SLC_KEY_TILE), F32),
        ],
        compiler_params=_params(("parallel", "parallel", "arbitrary")),
        name="nsa_slc_attn",
    )(reach, proj, proj, proj, selbias, gates_r, srow, scol)


def _sb_attn_kernel(q_ref, k_ref, v_ref, o_ref, acc_ref, carry_ref):
    ts = SB_TILE
    c = pl.program_id(2)
    acc_ref[...] = jnp.zeros(acc_ref.shape, F32)
    carry_ref[...] = jnp.zeros(carry_ref.shape, F32)
    r_idx = lax.broadcasted_iota(jnp.int32, (ts, 1), 0)
    c_idx = lax.broadcasted_iota(jnp.int32, (1, ts), 1)
    tri = (r_idx > c_idx).astype(BF16)
    diag = c_idx < r_idx

    def tile(jt, kw, valid):
        k0 = pl.multiple_of(jt * kw, kw)
        heads = range(SB_HEADS_PER_STEP)
        cols = [slice(e * HEAD_DIM, (e + 1) * HEAD_DIM) for e in heads]
        carries = [carry_ref[e] for e in heads]
        zs = [_dot_nt(q_ref[:, cols[e]], k_ref[pl.ds(k0, kw), cols[e]]) for e in heads]
        lfs = []
        for e in heads:
            lf = -(jnp.maximum(zs[e], 0.0) + jnp.log2(1.0 + jnp.exp2(-jnp.abs(zs[e]))))
            lfs.append(lf if valid is None else jnp.where(valid, lf, 0.0))
        afters = []
        for e in heads:
            hi = lfs[e].astype(BF16)
            lo = (lfs[e] - hi.astype(F32)).astype(BF16)
            afters.append(_dot(hi, tri[:kw, :kw]) + _dot(lo, tri[:kw, :kw]))
        avs = []
        for e in heads:
            a = jnp.exp2(zs[e] + lfs[e] + afters[e] + carries[e])
            if valid is not None:
                a = jnp.where(valid, a, 0.0)
            avs.append(_dot(a.astype(BF16), v_ref[pl.ds(k0, kw), cols[e]]))
        for e in heads:
            acc_ref[e] += avs[e]
            carry_ref[e] = carries[e] + jnp.sum(lfs[e], axis=-1, keepdims=True)

    def live():
        return (jnp.max(carry_ref[...]) > SB_UNDERFLOW * LOG2E).astype(jnp.int32)

    tile(c, ts, diag)

    def cond(state):
        jt, go = state
        return (jt >= 0) & (go > 0)

    def body(state):
        jt, _ = state
        tile(jt, SB_SWEEP_TILE, None)
        return jt - 1, live()

    lax.while_loop(cond, body, (c * (ts // SB_SWEEP_TILE) - 1, live()))
    for e in range(SB_HEADS_PER_STEP):
        o_ref[:, e * HEAD_DIM:(e + 1) * HEAD_DIM] = acc_ref[e].astype(o_ref.dtype)


def _sb_attention(proj, *, b, t):
    n = b * t
    ts = SB_TILE
    nq = t // ts
    hs = SB_HEADS_PER_STEP
    width = hs * HEAD_DIM
    nhp = N_HEADS // hs
    return pl.pallas_call(
        _sb_attn_kernel,
        grid=(b, nhp, nq),
        in_specs=[
            pl.BlockSpec((ts, width), lambda bi, h, c: (bi * nq + c, h)),
            pl.BlockSpec((t, width), lambda bi, h, c: (bi, nhp + h), pipeline_mode=pl.Buffered(1)),
            pl.BlockSpec((t, width), lambda bi, h, c: (bi, 2 * nhp + h), pipeline_mode=pl.Buffered(1)),
        ],
        out_specs=pl.BlockSpec((ts, width), lambda bi, h, c: (bi * nq + c, h)),
        out_shape=jax.ShapeDtypeStruct((n, N_HEADS * HEAD_DIM), BF16),
        scratch_shapes=[pltpu.VMEM((hs, ts, HEAD_DIM), F32), pltpu.VMEM((hs, ts, 1), F32)],
        compiler_params=_params(("parallel", "parallel", "arbitrary")),
        name="sb_attn",
    )(proj, proj, proj)


def _row_copy(src_hbm, dst_ref, sem, src_row, dst_row):
    return pltpu.make_async_copy(src_hbm.at[pl.ds(src_row, 1), :], dst_ref.at[pl.ds(dst_row, 1), :], sem)


def _row_gather_start(src_hbm, dst_ref, sem, idx_ref, base, count):
    def body(r, carry):
        _row_copy(src_hbm, dst_ref, sem, idx_ref[base + r], r).start()
        return carry

    lax.fori_loop(0, count, body, 0, unroll=8)


def _row_gather_wait(src_hbm, dst_ref, sem):
    pltpu.make_async_copy(src_hbm.at[pl.ds(0, dst_ref.shape[0]), :], dst_ref, sem).wait()


def _expert_weight_copies(hbm_refs, stage_refs, wsem, layer, expert, wslot):
    return [pltpu.make_async_copy(w.at[layer, expert], st.at[wslot], wsem.at[wslot]) for w, st in zip(hbm_refs, stage_refs)]


def _expert_kernel(src_ref, te_ref, nu_ref, par_ref, nxe_ref, x_hbm, g_ref, wg_hbm, wu_hbm, wd_hbm, o_ref,
                   xbuf, wsg, wsu, wsd, wgb, wub, wdb, sem, wsem, *, layer):
    tm = MOE_TILE
    i = pl.program_id(0)
    nt = pl.num_programs(0)
    n_used = nu_ref[0]
    slot = i % MOE_SLOTS
    nxt = (i + 2) % MOE_SLOTS
    next_base = jnp.where(i + 2 < nt, i + 2, i + 2 - nt) * tm

    weights_hbm = (wg_hbm, wu_hbm, wd_hbm)
    stages = (wsg, wsu, wsd)

    @pl.when((i < n_used) & ((i == 0) | (te_ref[i] != te_ref[jnp.maximum(i - 1, 0)])))
    def _():
        wslot = par_ref[i]

        @pl.when(i == 0)
        def _():
            for cp in _expert_weight_copies(weights_hbm, stages, wsem, layer, te_ref[0], wslot):
                cp.start()

        for cp in _expert_weight_copies(weights_hbm, stages, wsem, layer, te_ref[i], wslot):
            cp.wait()
        wgb[...] = wsg[wslot].astype(BF16)
        wub[...] = wsu[wslot].astype(BF16)
        wdb[...] = wsd[wslot].astype(BF16)

        @pl.when(nxe_ref[i] >= 0)
        def _():
            for cp in _expert_weight_copies(weights_hbm, stages, wsem, layer, nxe_ref[i], 1 - wslot):
                cp.start()

    @pl.when(i == 0)
    def _():
        _row_gather_start(x_hbm, xbuf.at[0], sem.at[0], src_ref, 0, tm)
        _row_gather_start(x_hbm, xbuf.at[1], sem.at[1], src_ref, tm, tm)

    @pl.when(i < n_used)
    def _():
        _row_gather_wait(x_hbm, xbuf.at[slot], sem.at[slot])
        x = xbuf[slot]
        ms = jnp.mean(x * x, axis=-1, keepdims=True)
        h = (x * lax.rsqrt(ms + EPS) * g_ref[...]).astype(BF16)
        for r in range(tm):
            _row_copy(x_hbm, xbuf.at[nxt], sem.at[nxt], src_ref[next_base + r], r).start()
        gate = _dot(h, wgb[...])
        up = _dot(h, wub[...])
        act = (gate * jax.nn.sigmoid(gate) * up).astype(BF16)
        o_ref[...] = _dot(act, wdb[...])

    @pl.when(i >= n_used)
    def _():
        o_ref[...] = jnp.zeros(o_ref.shape, o_ref.dtype)

    @pl.when(i == n_used - 1)
    def _():
        prev = (i + 1) % MOE_SLOTS
        _row_gather_wait(x_hbm, xbuf.at[prev], sem.at[prev])
        _row_gather_wait(x_hbm, xbuf.at[nxt], sem.at[nxt])


def _experts(src_tok, tile_expert, n_used, stage_slot, next_expert, x, g, wg, wu, wd, layer):
    n, d = x.shape
    tm = MOE_TILE
    p_rows = src_tok.shape[0]
    de = wg.shape[3]
    grid_spec = pltpu.PrefetchScalarGridSpec(
        num_scalar_prefetch=5,
        grid=(p_rows // tm,),
        in_specs=[
            pl.BlockSpec(memory_space=pl.ANY),
            pl.BlockSpec((1, d), lambda i, *_: (0, 0)),
            pl.BlockSpec(memory_space=pl.ANY),
            pl.BlockSpec(memory_space=pl.ANY),
            pl.BlockSpec(memory_space=pl.ANY),
        ],
        out_specs=pl.BlockSpec((tm, d), lambda i, *_: (i, 0)),
        scratch_shapes=[
            pltpu.VMEM((MOE_SLOTS, tm, d), F32),
            pltpu.VMEM((2, d, de), F32),
            pltpu.VMEM((2, d, de), F32),
            pltpu.VMEM((2, de, d), F32),
            pltpu.VMEM((d, de), BF16),
            pltpu.VMEM((d, de), BF16),
            pltpu.VMEM((de, d), BF16),
            pltpu.SemaphoreType.DMA((MOE_SLOTS,)),
            pltpu.SemaphoreType.DMA((2,)),
        ],
    )
    return pl.pallas_call(
        functools.partial(_expert_kernel, layer=layer),
        grid_spec=grid_spec,
        out_shape=jax.ShapeDtypeStruct((p_rows, d), F32),
        compiler_params=_params(("arbitrary",)),
        name="moe_experts",
    )(src_tok, tile_expert, n_used, stage_slot, next_expert, x, g.reshape(1, d), wg, wu, wd)


def _combine_kernel(pos_ref, x_ref, rt_ref, y_hbm, o_ref, ybuf, sem):
    tm = x_ref.shape[0]
    i = pl.program_id(0)
    nt = pl.num_programs(0)

    def start(tile, slot, unrolled):
        base = 2 * tile * tm
        for s in range(2):
            if unrolled:
                for r in range(tm):
                    _row_copy(y_hbm, ybuf.at[slot, s], sem.at[slot], pos_ref[base + 2 * r + s], r).start()
            else:
                def body(r, carry):
                    _row_copy(y_hbm, ybuf.at[slot, s], sem.at[slot], pos_ref[base + 2 * r + s], r).start()
                    return carry

                lax.fori_loop(0, tm, body, 0, unroll=8)

    @pl.when(i == 0)
    def _():
        start(0, 0, False)

    @pl.when(i + 1 < nt)
    def _():
        start(i + 1, (i + 1) % 2, True)

    slot = i % 2
    for s in range(2):
        _row_gather_wait(y_hbm, ybuf.at[slot, s], sem.at[slot])
    rt = rt_ref[...]
    o_ref[...] = x_ref[...] + rt[:, 2:3] * ybuf[slot, 0] + rt[:, 3:4] * ybuf[slot, 1]


def _combine(pos, x, routing, ys, *, tm):
    n, d = x.shape
    grid_spec = pltpu.PrefetchScalarGridSpec(
        num_scalar_prefetch=1,
        grid=(n // tm,),
        in_specs=[
            pl.BlockSpec((tm, d), lambda i, p: (i, 0)),
            pl.BlockSpec((tm, LANES), lambda i, p: (i, 0)),
            pl.BlockSpec(memory_space=pl.ANY),
        ],
        out_specs=pl.BlockSpec((tm, d), lambda i, p: (i, 0)),
        scratch_shapes=[pltpu.VMEM((2, 2, tm, d), F32), pltpu.SemaphoreType.DMA((2,))],
    )
    return pl.pallas_call(
        _combine_kernel,
        grid_spec=grid_spec,
        out_shape=jax.ShapeDtypeStruct((n, d), F32),
        compiler_params=_params(("arbitrary",)),
        name="moe_combine",
    )(pos, x, routing, ys)


def _source_tokens_kernel(pos_ref, src_ref):
    def zero(q, carry):
        src_ref[q] = 0
        return carry

    def put(p, carry):
        src_ref[pos_ref[p]] = p >> 1
        return carry

    lax.fori_loop(0, src_ref.shape[0], zero, 0, unroll=16)
    lax.fori_loop(0, pos_ref.shape[0], put, 0, unroll=16)


def _source_tokens(pos, p_rows):
    return pl.pallas_call(
        _source_tokens_kernel,
        in_specs=[pl.BlockSpec(memory_space=pltpu.SMEM)],
        out_specs=pl.BlockSpec(memory_space=pltpu.SMEM),
        out_shape=jax.ShapeDtypeStruct((p_rows,), jnp.int32),
        name="moe_source_tokens",
    )(pos)


def _hier_moe(x, routing, g, w_gate, w_up, w_down, layer):
    n, d = x.shape
    tm = MOE_TILE

    e_flat = routing[:, :2].astype(jnp.int32).reshape(-1)
    onehot = (e_flat[:, None] == jnp.arange(N_EXPERTS, dtype=jnp.int32)[None, :]).astype(jnp.int32)
    counts = jnp.sum(onehot, axis=0)
    rank = jnp.sum((jnp.cumsum(onehot, axis=0) - onehot) * onehot, axis=1)
    padded = ((counts + tm - 1) // tm) * tm
    ends = jnp.cumsum(padded)
    pos = (ends - padded)[e_flat] + rank
    p_rows = 2 * n + N_EXPERTS * tm
    pos = pos.astype(jnp.int32)
    src_tok = _source_tokens(pos, p_rows)
    tile_start = jnp.arange(p_rows // tm, dtype=jnp.int32) * tm
    tile_expert = jnp.minimum(jnp.sum((ends[None, :] <= tile_start[:, None]).astype(jnp.int32), axis=1), N_EXPERTS - 1)
    n_used = (ends[-1] // tm).astype(jnp.int32).reshape(1)

    n_tiles = p_rows // tm
    changed = jnp.concatenate([jnp.ones((1,), jnp.int32), (tile_expert[1:] != tile_expert[:-1]).astype(jnp.int32)])
    stage_slot = (jnp.cumsum(changed) - 1) % 2
    after = (ends // tm)[tile_expert]
    next_expert = jnp.where(after < n_used[0], tile_expert[jnp.minimum(after, n_tiles - 1)], -1)

    ys = _experts(src_tok, tile_expert, n_used, stage_slot.astype(jnp.int32), next_expert.astype(jnp.int32), x, g, w_gate, w_up, w_down, layer)
    return _combine(pos, x, routing, ys, tm=256 if n % 256 == 0 else n)


def _alibi_slopes():
    return jnp.exp2(-8.0 * jnp.arange(1, N_HEADS + 1, dtype=F32) / N_HEADS)


def _cmp_to_slc(n_cmp_pad, n_slc):
    cs = np.arange(n_cmp_pad) * CMP_STRIDE
    ss = np.arange(SLC_LANES) * SLC_BLOCK
    lo = np.maximum(cs[:, None], ss[None, :])
    hi = np.minimum(cs[:, None] + CMP_BLOCK, ss[None, :] + SLC_BLOCK)
    m = np.maximum(hi - lo, 0).astype(np.float32) / CMP_BLOCK
    m[:, n_slc:] = 0.0
    return m


def _nsa_mixer(x, g_norm, w_in, qk_gain, cmp_pe, cmp_w1, cmp_w2, w_out, router, *, b, t):
    n, d = x.shape
    q_dim = N_HEADS * HEAD_DIM
    kv_dim = N_KV * HEAD_DIM
    scale = HEAD_DIM ** -0.5 * LOG2E
    main = q_dim + 6 * kv_dim
    ones = jnp.ones((kv_dim,), F32)
    tile4 = lambda v: jnp.tile(v, N_KV)
    cgain = jnp.concatenate([jnp.tile(qk_gain[0], N_HEADS) * scale, ones, ones, tile4(qk_gain[2]), ones, tile4(qk_gain[3]), ones])
    tiles_per = lambda cols: cols // PROJ_COLS
    q_tiles = tiles_per(q_dim)
    kv_tiles = tiles_per(kv_dim)
    norm_tiles = list(range(q_tiles)) + [q_tiles + 2 * kv_tiles + j for j in range(kv_tiles)] + [q_tiles + 4 * kv_tiles + j for j in range(kv_tiles)]
    n_gate = 3 * N_HEADS
    w_gate = jnp.pad(w_in[:, main:], ((0, 0), (0, LANES - n_gate))).astype(BF16)
    cmp_tiles = [q_tiles + j for j in range(2 * kv_tiles)]
    proj, gate_logits, kv_cmp = _norm_matmul(x, g_norm, w_in[:, :main].astype(BF16), cgain, norm_tiles, extra_w=w_gate, f32_tiles=cmp_tiles)
    gr = gate_logits[:, :n_gate].reshape(n, 3, N_KV, GQA_REP).transpose(2, 0, 1, 3).reshape(N_KV, n, 3 * GQA_REP)
    gates_r = jnp.pad(gr, ((0, 0), (0, 0), (0, LANES - 3 * GQA_REP)))

    nc = t // CMP_STRIDE
    half = CMP_BLOCK // 2
    w1cat = jnp.concatenate([cmp_w1[:, :half], cmp_w1[:, half:]], axis=-1)
    pe_rows = jnp.pad(jnp.stack([cmp_pe[:, :half], cmp_pe[:, half:]], axis=2), ((0, 0), (0, 0), (0, 6), (0, 0)))
    cmp_gain = jnp.stack([qk_gain[1], jnp.ones((HEAD_DIM,), F32)]).reshape(2, 1, HEAD_DIM)
    cmp_kv = _compress(kv_cmp, w1cat.astype(BF16), pe_rows.astype(BF16), cmp_w2.astype(BF16), cmp_gain, b=b, t=t)

    slopes = _alibi_slopes() * LOG2E
    pieces = _slope_pieces(slopes)
    pad_flag = jnp.full((N_HEADS, 1), NEG, F32).astype(BF16)
    srow = _slope_rows(jnp.concatenate([pieces, pad_flag], axis=1), 0)
    n_slc = t // SLC_BLOCK
    msel = jnp.asarray(_cmp_to_slc(nc, n_slc), BF16)
    q_blocks = q_dim // HEAD_DIM
    o_c, selbias = _cmp_attention(proj, cmp_kv, gates_r, msel, srow, b=b, t=t, top_k=min(SLC_TOPK, n_slc))
    reach = _alibi_reach(slopes, qk_gain[0], qk_gain[2])
    o_s = _slc_attention(slopes, reach, proj, selbias, gates_r, b=b, t=t, k_col=q_blocks + 2 * N_KV, v_col=q_blocks + 3 * N_KV)
    o_w = _win_attention(proj, gates_r, srow, b=b, t=t, k_col=q_blocks + 4 * N_KV, v_col=q_blocks + 5 * N_KV)
    return _proj_residual([o_c, o_s, o_w], w_out.astype(BF16), x, router)


def _sb_mixer(x, g_norm, w_in, w_out, router, *, b, t):
    q_dim = N_HEADS * HEAD_DIM
    scale = HEAD_DIM ** -0.5 * LOG2E
    cgain = jnp.concatenate([jnp.full((q_dim,), scale, F32), jnp.ones((2 * q_dim,), F32)])
    proj = _norm_matmul(x, g_norm, w_in.astype(BF16), cgain, ())
    o = _sb_attention(proj, b=b, t=t)
    return _proj_residual([o], w_out.astype(BF16), x, router)


def kernel(x, norm_mix, norm_ffn, nsa_w_in, nsa_qk_gain, nsa_cmp_pe, nsa_cmp_w1, nsa_cmp_w2, nsa_w_out, sb_w_in, sb_w_out, moe_w_group, moe_b_group, moe_w_router, moe_b_router, moe_w_gate, moe_w_up, moe_w_down):
    b, t, d = x.shape
    depth = norm_mix.shape[0]
    xf = x.reshape(b * t, d)
    for i in range(depth):
        j = i // 2
        router = _router_params(norm_ffn[i], moe_w_group[i], moe_b_group[i], moe_w_router[i], moe_b_router[i])
        if i % 2 == 0:
            xf, routing = _nsa_mixer(xf, norm_mix[i], nsa_w_in[j], nsa_qk_gain[j], nsa_cmp_pe[j], nsa_cmp_w1[j], nsa_cmp_w2[j], nsa_w_out[j], router, b=b, t=t)
        else:
            xf, routing = _sb_mixer(xf, norm_mix[i], sb_w_in[j], sb_w_out[j], router, b=b, t=t)
        xf = _hier_moe(xf, routing, norm_ffn[i], moe_w_gate, moe_w_up, moe_w_down, i)
    return xf.reshape(b, t, d)
```

```python
import functools

import jax
import jax.numpy as jnp
import numpy as np
from jax import lax
from jax.experimental import pallas as pl
from jax.experimental.pallas import tpu as pltpu

F32 = jnp.float32
BF16 = jnp.bfloat16

HEAD_DIM = 128
N_HEADS = 16
N_KV = 4
GQA_REP = N_HEADS // N_KV
CMP_BLOCK = 32
CMP_STRIDE = 16
SLC_BLOCK = 64
SLC_TOPK = 16
WINDOW = 512
Q_BLOCK = 256
WIN_SUB = 4
N_GROUPS = 4
EXPERTS_PER_GROUP = 8
N_EXPERTS = N_GROUPS * EXPERTS_PER_GROUP
EPS = 1e-6
NEG = -1e30
LANES = 128
VMEM_LIMIT = 56 * 1024 * 1024

PROJ_ROWS = 512
PROJ_COLS = 512
CMP_Q_BLOCK = 256
CMP_SUB = 2
SLC_Q_BLOCK = 512
SLC_LANES = 128
SB_TILE = 256
SB_HEADS_PER_STEP = 8
SLC_KEY_TILE = 1024
LOG2E = 1.4426950408889634
SB_UNDERFLOW = -105.0
MOE_TILE = 256
MOE_SLOTS = 3
EXP2_UNDERFLOW = 152.0
AUG_PAD_LANE = 6


def _dot(a, b):
    return jnp.dot(a, b, preferred_element_type=F32)


def _dot_nt(a, b):
    return lax.dot_general(a, b, (((1,), (1,)), ((), ())), preferred_element_type=F32)


def _params(sem):
    return pltpu.CompilerParams(dimension_semantics=sem, vmem_limit_bytes=VMEM_LIMIT)


def _resident(shape):
    return pl.BlockSpec(shape, lambda *_: (0,) * len(shape), pipeline_mode=pl.Buffered(1))


def _norm_matmul_kernel(x_ref, g_ref, w_ref, cg_ref, *rest, norm_tiles, f32_tiles, has_extra):
    if has_extra:
        wx_ref, o_ref, ox_ref, of_ref = rest
    else:
        (o_ref,) = rest
    x = x_ref[...]
    ms = jnp.mean(x * x, axis=-1, keepdims=True)
    h = (x * lax.rsqrt(ms + EPS) * g_ref[...]).astype(BF16)
    tn = PROJ_COLS
    for j in range(w_ref.shape[1] // tn):
        y = _dot(h, w_ref[:, j * tn:(j + 1) * tn])
        if j in f32_tiles:
            k = f32_tiles.index(j)
            of_ref[:, k * tn:(k + 1) * tn] = y
        for hh in range(tn // LANES):
            yh = y[:, hh * LANES:(hh + 1) * LANES]
            if j in norm_tiles:
                yh = yh * lax.rsqrt(jnp.mean(yh * yh, axis=-1, keepdims=True) + EPS)
            sl = slice(j * tn + hh * LANES, j * tn + (hh + 1) * LANES)
            o_ref[:, sl] = (yh * cg_ref[:, sl]).astype(o_ref.dtype)
    if has_extra:
        ox_ref[...] = _dot(h, wx_ref[...])


def _norm_matmul(x, g, w, cgain, norm_tiles, extra_w=None, f32_tiles=()):
    n, d = x.shape
    m = w.shape[1]
    tm = PROJ_ROWS
    has_extra = extra_w is not None
    in_specs = [pl.BlockSpec((tm, d), lambda i: (i, 0)), _resident((1, d)), _resident((d, m)), _resident((1, m))]
    out_specs = [pl.BlockSpec((tm, m), lambda i: (i, 0))]
    out_shape = [jax.ShapeDtypeStruct((n, m), BF16)]
    args = [x, g.reshape(1, d), w, cgain.reshape(1, m)]
    if has_extra:
        mx = extra_w.shape[1]
        mf = len(f32_tiles) * PROJ_COLS
        in_specs.append(_resident((d, mx)))
        out_specs += [pl.BlockSpec((tm, mx), lambda i: (i, 0)), pl.BlockSpec((tm, mf), lambda i: (i, 0))]
        out_shape += [jax.ShapeDtypeStruct((n, mx), F32), jax.ShapeDtypeStruct((n, mf), F32)]
        args.append(extra_w)
    out = pl.pallas_call(
        functools.partial(_norm_matmul_kernel, norm_tiles=tuple(norm_tiles), f32_tiles=tuple(f32_tiles), has_extra=has_extra),
        grid=(n // tm,),
        in_specs=in_specs,
        out_specs=out_specs,
        out_shape=out_shape,
        compiler_params=_params(("parallel",)),
        name="norm_matmul",
    )(*args)
    return out if has_extra else out[0]


def _first_max(vals, lane):
    m = jnp.max(vals, axis=-1, keepdims=True)
    idx = jnp.min(jnp.where(vals == m, lane, LANES), axis=-1, keepdims=True)
    return m, idx


def _route(x, g_ref, w_ref, b_ref):
    ms = jnp.mean(x * x, axis=-1, keepdims=True)
    h = x * lax.rsqrt(ms + EPS) * g_ref[...]
    h_hi = h.astype(BF16)
    h_lo = (h - h_hi.astype(F32)).astype(BF16)
    logits = _dot(h_hi, w_ref[0]) + _dot(h_hi, w_ref[1]) + _dot(h_lo, w_ref[0]) + b_ref[...]
    lane = lax.broadcasted_iota(jnp.int32, (1, LANES), 1)
    gl = jnp.where(lane < N_GROUPS, logits, NEG)
    ge = jnp.exp(gl - jnp.max(gl, axis=-1, keepdims=True))
    pg_all = ge / jnp.sum(ge, axis=-1, keepdims=True)
    pg, gsel = _first_max(jnp.where(lane < N_GROUPS, pg_all, -1.0), lane)
    lo = N_GROUPS + gsel * EXPERTS_PER_GROUP
    in_group = (lane >= lo) & (lane < lo + EXPERTS_PER_GROUP)
    el = jnp.where(in_group, logits, NEG)
    ee = jnp.exp(el - jnp.max(el, axis=-1, keepdims=True))
    pe_all = jnp.where(in_group, ee / jnp.sum(ee, axis=-1, keepdims=True), -1.0)
    p1, i1 = _first_max(pe_all, lane)
    p2, i2 = _first_max(jnp.where(lane == i1, -1.0, pe_all), lane)
    denom = p1 + p2
    w1 = pg * p1 / denom
    w2 = pg * p2 / denom
    e1 = (i1 - N_GROUPS).astype(F32)
    e2 = (i2 - N_GROUPS).astype(F32)
    return jnp.where(lane == 0, e1, jnp.where(lane == 1, e2, jnp.where(lane == 2, w1, jnp.where(lane == 3, w2, 0.0))))


def _proj_residual_kernel(*refs, n_in):
    a_refs = refs[:n_in]
    w_ref, r_ref, g_ref, wr_ref, br_ref, o_ref, rt_ref = refs[n_in:]
    if n_in == 1:
        a = a_refs[0][...]
    else:
        acc = a_refs[0][...].astype(F32)
        for a_ref in a_refs[1:]:
            acc = acc + a_ref[...].astype(F32)
        a = acc.astype(BF16)
    tn = PROJ_COLS
    for j in range(w_ref.shape[1] // tn):
        sl = slice(j * tn, (j + 1) * tn)
        o_ref[:, sl] = r_ref[:, sl] + _dot(a, w_ref[:, sl])
    rt_ref[...] = _route(o_ref[...], g_ref, wr_ref, br_ref)


def _proj_residual(branches, w, res, router):
    n, k = branches[0].shape
    m = w.shape[1]
    tm = PROJ_ROWS
    n_in = len(branches)
    g_ffn, w_split, b_r = router
    return pl.pallas_call(
        functools.partial(_proj_residual_kernel, n_in=n_in),
        grid=(n // tm,),
        in_specs=[pl.BlockSpec((tm, k), lambda i: (i, 0)) for _ in range(n_in)]
        + [_resident((k, m)), pl.BlockSpec((tm, m), lambda i: (i, 0)), _resident((1, m)), _resident(w_split.shape), _resident((1, LANES))],
        out_specs=[pl.BlockSpec((tm, m), lambda i: (i, 0)), pl.BlockSpec((tm, LANES), lambda i: (i, 0))],
        out_shape=[jax.ShapeDtypeStruct((n, m), F32), jax.ShapeDtypeStruct((n, LANES), F32)],
        compiler_params=_params(("parallel",)),
        name="proj_residual",
    )(*branches, w, res, g_ffn.reshape(1, m), w_split, b_r)


def _router_params(g_ffn, w_group, b_group, w_router, b_router):
    n_logit = N_GROUPS + N_EXPERTS
    w_r = jnp.pad(jnp.concatenate([w_group, w_router], axis=1), ((0, 0), (0, LANES - n_logit)))
    b_r = jnp.pad(jnp.concatenate([b_group, b_router]), (0, LANES - n_logit)).reshape(1, LANES)
    w_hi = w_r.astype(BF16)
    return g_ffn, jnp.stack([w_hi, (w_r - w_hi.astype(F32)).astype(BF16)]), b_r


def _cmp_kernel(kv_ref, w1_ref, pe_ref, w2_ref, gain_ref, o_ref):
    nc = kv_ref.shape[0] // CMP_STRIDE
    p = jnp.zeros((nc, 2 * HEAD_DIM), F32)
    pb = jnp.zeros((8, 2 * HEAD_DIM), F32)
    for l in range(CMP_STRIDE):
        w1 = w1_ref[0, l]
        p = p + _dot(kv_ref[pl.ds(l, nc, stride=CMP_STRIDE), :].astype(BF16), w1)
        pb = pb + _dot(pe_ref[0, l], w1)
    bias = pb[0:1, :HEAD_DIM] + pb[1:2, HEAD_DIM:]
    second = pltpu.roll(p[:, HEAD_DIM:], shift=nc - 1, axis=0)
    hid = jax.nn.gelu(p[:, :HEAD_DIM] + second + bias)
    c = _dot(hid.astype(BF16), w2_ref[0])
    nf = (pl.program_id(1) == 0).astype(F32)
    ms = jnp.mean(c * c, axis=-1, keepdims=True)
    fac = nf * lax.rsqrt(ms + EPS) + (1.0 - nf)
    o_ref[0, 0, 0] = (c * fac * gain_ref[0]).astype(o_ref.dtype)


def _compress(kv, w1cat, pe_rows, w2, gains, *, b, t):
    g, cs, hd = N_KV, CMP_STRIDE, HEAD_DIM
    nc = t // cs
    return pl.pallas_call(
        _cmp_kernel,
        grid=(b, 2, g),
        in_specs=[
            pl.BlockSpec((t, hd), lambda bi, k, gi: (bi, k * g + gi)),
            pl.BlockSpec((1, cs, hd, 2 * HEAD_DIM), lambda bi, k, gi: (k, 0, 0, 0)),
            pl.BlockSpec((1, cs, 8, hd), lambda bi, k, gi: (k, 0, 0, 0)),
            pl.BlockSpec((1, HEAD_DIM, HEAD_DIM), lambda bi, k, gi: (k, 0, 0)),
            pl.BlockSpec((1, 1, HEAD_DIM), lambda bi, k, gi: (k, 0, 0)),
        ],
        out_specs=pl.BlockSpec((1, 1, 1, nc, HEAD_DIM), lambda bi, k, gi: (bi, k, gi, 0, 0)),
        out_shape=jax.ShapeDtypeStruct((b, 2, g, nc, HEAD_DIM), BF16),
        compiler_params=_params(("parallel", "parallel", "parallel")),
        name="nsa_compress",
    )(kv, w1cat, pe_rows, w2, gains)


def _cmp_attn_kernel(q_ref, kc_ref, vc_ref, gate_ref, msel_ref, srow_ref, oc_ref, sb_ref, qa_ref, ka_ref, va_ref, *, top_k):
    c = pl.program_id(2)
    qb = CMP_Q_BLOCK
    t_step = c * (CMP_SUB * qb)
    rows = GQA_REP * qb
    nc = ka_ref.shape[0]
    subs = range(CMP_SUB)

    @pl.when(c == 0)
    def _():
        n_idx = lax.broadcasted_iota(jnp.int32, (nc, LANES), 0)
        lane = lax.broadcasted_iota(jnp.int32, (nc, LANES), 1)
        aug = jnp.where(lane < 3, ((n_idx >> 7) << 11).astype(F32), jnp.where(lane < 6, ((n_idx & 127) << 4).astype(F32), 0.0))
        ka_ref[:, :HEAD_DIM] = kc_ref[0, 0, 0]
        ka_ref[:, HEAD_DIM:] = aug.astype(BF16)
        va_ref[:, :HEAD_DIM] = vc_ref[0, 0, 0]
        va_ref[:, HEAD_DIM:] = (lane == 0).astype(BF16)

    for u in subs:
        for r in range(GQA_REP):
            dst = slice((u * GQA_REP + r) * qb, (u * GQA_REP + r + 1) * qb)
            qa_ref[dst, :HEAD_DIM] = q_ref[u * qb:(u + 1) * qb, r * HEAD_DIM:(r + 1) * HEAD_DIM]
            qa_ref[dst, HEAD_DIM:] = jnp.broadcast_to(srow_ref[0, r:r + 1, :], (qb, LANES))
    row_q = lax.broadcasted_iota(jnp.int32, (rows, 1), 0) & (qb - 1)
    lane_q = lax.broadcasted_iota(jnp.int32, (1, qb), 1)
    gates = jax.nn.sigmoid(gate_ref[0])

    def attend(k):
        ncol = k * LANES
        nrow = ncol * CMP_STRIDE // SLC_BLOCK
        cend = lax.broadcasted_iota(jnp.int32, (1, ncol), 1) * CMP_STRIDE + (CMP_BLOCK - 1)
        j_col = lax.broadcasted_iota(jnp.int32, (nrow, 1), 0)
        msel = msel_ref[:ncol, :]
        scores = [_dot_nt(qa_ref[u * rows:(u + 1) * rows, :], ka_ref[:ncol, :]) for u in subs]
        es, ms = [], []
        for u in subs:
            s = jnp.where(cend <= t_step + u * qb + row_q, scores[u], NEG)
            m = jnp.max(s, axis=-1, keepdims=True)
            ms.append(m)
            es.append(jnp.exp2(s - m))
        accs = [_dot(es[u].astype(BF16), va_ref[:ncol, :]) for u in subs]
        imps = []
        for u in subs:
            inv = jnp.where(ms[u] > 0.5 * NEG, 1.0 / accs[u][:, HEAD_DIM:HEAD_DIM + 1], 0.0)
            psum = jnp.zeros((qb, ncol), F32)
            for r in range(GQA_REP):
                rs = slice(r * qb, (r + 1) * qb)
                psum = psum + es[u][rs] * inv[rs]
                gate = gates[u * qb:(u + 1) * qb, r:r + 1]
                oc_ref[u * qb:(u + 1) * qb, r * HEAD_DIM:(r + 1) * HEAD_DIM] = (accs[u][rs, :HEAD_DIM] * (inv[rs] * gate)).astype(oc_ref.dtype)
            p_hi = psum.astype(BF16)
            rem = psum - p_hi.astype(F32)
            p_mid = rem.astype(BF16)
            p_lo = (rem - p_mid.astype(F32)).astype(BF16)
            imps.append(_dot(p_hi, msel) + _dot(p_mid, msel) + _dot(p_lo, msel))

        score, keep = [], []
        for u in subs:
            t_row = t_step + u * qb + lane_q
            forced = (j_col == 0) | (j_col == (t_row >> 6))
            causal = (j_col << 6) <= t_row
            score.append(jnp.where(forced, -NEG, jnp.where(causal, imps[u].T[:nrow], NEG)))
            keep.append(jnp.zeros((nrow, qb), F32))
        for _ in range(top_k):
            for u in subs:
                mx = jnp.max(score[u], axis=0, keepdims=True)
                first = jnp.min(jnp.where(score[u] == mx, j_col, SLC_LANES), axis=0, keepdims=True)
                hit = j_col == first
                keep[u] = jnp.where(hit, (mx > 0.5 * NEG).astype(F32), keep[u])
                score[u] = jnp.where(hit, -3e38, score[u])
        for u in subs:
            bias_t = jnp.where(keep[u] > 0.5, 0.0, NEG)
            if nrow < SLC_LANES:
                bias_t = jnp.concatenate([bias_t, jnp.full((SLC_LANES - nrow, qb), NEG, F32)], axis=0)
            sb_ref[0, u * qb:(u + 1) * qb, :] = bias_t.T.astype(sb_ref.dtype)

    n_variants = nc // LANES
    needed = jnp.minimum(lax.div(lax.div(t_step + CMP_SUB * qb - CMP_BLOCK, CMP_STRIDE), LANES) + 1, n_variants)
    for k in range(1, n_variants + 1):
        pl.when(needed == k)(functools.partial(attend, k))


def _cmp_attention(proj, cmp_kv, gates_r, msel, srow, *, b, t, top_k):
    n = b * t
    qb = CMP_SUB * CMP_Q_BLOCK
    nqb = t // qb
    rows = GQA_REP * qb
    nc = cmp_kv.shape[3]
    rowblk = lambda bi, gi, c: bi * nqb + c
    return pl.pallas_call(
        functools.partial(_cmp_attn_kernel, top_k=top_k),
        grid=(b, N_KV, nqb),
        in_specs=[
            pl.BlockSpec((qb, GQA_REP * HEAD_DIM), lambda bi, gi, c: (rowblk(bi, gi, c), gi)),
            pl.BlockSpec((1, 1, 1, nc, HEAD_DIM), lambda bi, gi, c: (bi, 0, gi, 0, 0)),
            pl.BlockSpec((1, 1, 1, nc, HEAD_DIM), lambda bi, gi, c: (bi, 1, gi, 0, 0)),
            pl.BlockSpec((1, qb, LANES), lambda bi, gi, c: (gi, rowblk(bi, gi, c), 0)),
            pl.BlockSpec((nc, SLC_LANES), lambda bi, gi, c: (0, 0)),
            pl.BlockSpec((1, 8, LANES), lambda bi, gi, c: (gi, 0, 0)),
        ],
        out_specs=[
            pl.BlockSpec((qb, GQA_REP * HEAD_DIM), lambda bi, gi, c: (rowblk(bi, gi, c), gi)),
            pl.BlockSpec((1, qb, SLC_LANES), lambda bi, gi, c: (gi, rowblk(bi, gi, c), 0)),
        ],
        out_shape=[
            jax.ShapeDtypeStruct((n, N_HEADS * HEAD_DIM), BF16),
            jax.ShapeDtypeStruct((N_KV, n, SLC_LANES), BF16),
        ],
        scratch_shapes=[
            pltpu.VMEM((rows, 2 * HEAD_DIM), BF16),
            pltpu.VMEM((nc, 2 * HEAD_DIM), BF16),
            pltpu.VMEM((nc, 2 * HEAD_DIM), BF16),
        ],
        compiler_params=_params(("parallel", "parallel", "arbitrary")),
        name="nsa_cmp_attn",
    )(proj, cmp_kv, cmp_kv, gates_r, msel, srow)


def _win_attn_kernel(q_ref, k_ref, v_ref, gate_ref, srow_ref, o_ref, qa_ref, ka_ref, va_ref):
    c = pl.program_id(2)
    t0 = pl.multiple_of(c * (WIN_SUB * Q_BLOCK), WIN_SUB * Q_BLOCK)
    rows = GQA_REP * Q_BLOCK
    span = Q_BLOCK + WINDOW
    t_len = k_ref.shape[0]

    @pl.when(c == 0)
    def _():
        pos = lax.broadcasted_iota(jnp.int32, (t_len, LANES), 0)
        lane = lax.broadcasted_iota(jnp.int32, (t_len, LANES), 1)
        aug = jnp.where(lane < 3, ((pos >> 6) << 6).astype(F32), jnp.where(lane < 6, (pos & 63).astype(F32), 0.0))
        pad_lane = lax.broadcasted_iota(jnp.int32, (WINDOW, LANES), 1)
        ka_ref[:WINDOW, :HEAD_DIM] = jnp.zeros((WINDOW, HEAD_DIM), BF16)
        ka_ref[:WINDOW, HEAD_DIM:] = (pad_lane == AUG_PAD_LANE).astype(BF16)
        ka_ref[WINDOW:, :HEAD_DIM] = k_ref[...]
        ka_ref[WINDOW:, HEAD_DIM:] = aug.astype(BF16)
        va_ref[:WINDOW, :] = jnp.zeros((WINDOW, 2 * HEAD_DIM), BF16)
        va_ref[WINDOW:, :HEAD_DIM] = v_ref[...]
        va_ref[WINDOW:, HEAD_DIM:] = (lane == 0).astype(BF16)

    i_col = lax.broadcasted_iota(jnp.int32, (rows, 1), 0) & (Q_BLOCK - 1)
    kk = lax.broadcasted_iota(jnp.int32, (1, Q_BLOCK), 1)
    gates = jax.nn.sigmoid(gate_ref[0])
    subs = range(WIN_SUB)
    for u in subs:
        for r in range(GQA_REP):
            dst = slice((u * GQA_REP + r) * Q_BLOCK, (u * GQA_REP + r + 1) * Q_BLOCK)
            qa_ref[dst, :HEAD_DIM] = q_ref[u * Q_BLOCK:(u + 1) * Q_BLOCK, r * HEAD_DIM:(r + 1) * HEAD_DIM]
            qa_ref[dst, HEAD_DIM:] = jnp.broadcast_to(srow_ref[0, r:r + 1, :], (Q_BLOCK, LANES))
    starts = [t0 + u * Q_BLOCK for u in subs]
    scores = [_dot_nt(qa_ref[u * rows:(u + 1) * rows, :], ka_ref[pl.ds(starts[u], span), :]) for u in subs]
    weights = []
    for u in subs:
        s = scores[u]
        first = jnp.where(kk > i_col, s[:, :Q_BLOCK], NEG)
        mid = s[:, Q_BLOCK:WINDOW]
        last = jnp.where(kk <= i_col, s[:, WINDOW:], NEG)
        m = jnp.maximum(jnp.maximum(jnp.max(first, axis=-1, keepdims=True), jnp.max(mid, axis=-1, keepdims=True)),
                        jnp.max(last, axis=-1, keepdims=True))
        weights.append([jnp.exp2(part - m).astype(BF16) for part in (first, mid, last)])
    accs = []
    for u in subs:
        p_first, p_mid, p_last = weights[u]
        accs.append(_dot(p_first, va_ref[pl.ds(starts[u], Q_BLOCK), :])
                    + _dot(p_mid, va_ref[pl.ds(starts[u] + Q_BLOCK, WINDOW - Q_BLOCK), :])
                    + _dot(p_last, va_ref[pl.ds(starts[u] + WINDOW, Q_BLOCK), :]))
    for u in subs:
        o = accs[u][:, :HEAD_DIM] / accs[u][:, HEAD_DIM:HEAD_DIM + 1]
        for r in range(GQA_REP):
            gate = gates[u * Q_BLOCK:(u + 1) * Q_BLOCK, 2 * GQA_REP + r:2 * GQA_REP + r + 1]
            o_ref[u * Q_BLOCK:(u + 1) * Q_BLOCK, r * HEAD_DIM:(r + 1) * HEAD_DIM] = (o[r * Q_BLOCK:(r + 1) * Q_BLOCK] * gate).astype(o_ref.dtype)


def _win_attention(proj, gates_r, srow, *, b, t, k_col, v_col):
    n = b * t
    step_q = WIN_SUB * Q_BLOCK
    nqb = t // step_q
    rows = GQA_REP * Q_BLOCK
    rowblk = lambda bi, gi, c: bi * nqb + c
    return pl.pallas_call(
        _win_attn_kernel,
        grid=(b, N_KV, nqb),
        in_specs=[
            pl.BlockSpec((step_q, GQA_REP * HEAD_DIM), lambda bi, gi, c: (rowblk(bi, gi, c), gi)),
            pl.BlockSpec((t, HEAD_DIM), lambda bi, gi, c: (bi, k_col + gi)),
            pl.BlockSpec((t, HEAD_DIM), lambda bi, gi, c: (bi, v_col + gi)),
            pl.BlockSpec((1, step_q, LANES), lambda bi, gi, c: (gi, rowblk(bi, gi, c), 0)),
            pl.BlockSpec((1, 8, LANES), lambda bi, gi, c: (gi, 0, 0)),
        ],
        out_specs=pl.BlockSpec((step_q, GQA_REP * HEAD_DIM), lambda bi, gi, c: (rowblk(bi, gi, c), gi)),
        out_shape=jax.ShapeDtypeStruct((n, N_HEADS * HEAD_DIM), BF16),
        scratch_shapes=[
            pltpu.VMEM((WIN_SUB * rows, 2 * HEAD_DIM), BF16),
            pltpu.VMEM((t + WINDOW, 2 * HEAD_DIM), BF16),
            pltpu.VMEM((t + WINDOW, 2 * HEAD_DIM), BF16),
        ],
        compiler_params=_params(("parallel", "parallel", "arbitrary")),
        name="nsa_win_attn",
    )(proj, proj, proj, gates_r, srow)


def _slc_attn_kernel(reach_ref, q_ref, k_ref, v_ref, sb_ref, gate_ref, srow_ref, scol_ref, o_ref, qa_ref, ka_ref, va_ref, m_ref, acc_ref, s_ref):
    c = pl.program_id(2)
    qb = SLC_Q_BLOCK
    t0 = c * qb
    rows = GQA_REP * qb
    tk = SLC_KEY_TILE
    nb = tk // SLC_BLOCK
    t_len = k_ref.shape[0]

    @pl.when(c == 0)
    def _():
        pos = lax.broadcasted_iota(jnp.int32, (t_len, LANES), 0)
        lane = lax.broadcasted_iota(jnp.int32, (t_len, LANES), 1)
        off = pos & (tk - 1)
        blk = (pos >> 6) & (nb - 1)
        aug = jnp.where(lane < nb, (blk == lane).astype(F32),
                        jnp.where(lane < nb + 3, ((off >> 3) << 3).astype(F32),
                                  jnp.where(lane < nb + 6, (off & 7).astype(F32), 0.0)))
        ka_ref[:, :HEAD_DIM] = k_ref[...]
        ka_ref[:, HEAD_DIM:] = aug.astype(BF16)
        va_ref[:, :HEAD_DIM] = v_ref[...]
        va_ref[:, HEAD_DIM:] = (lane == 0).astype(BF16)

    for r in range(GQA_REP):
        qa_ref[r * qb:(r + 1) * qb, :HEAD_DIM] = q_ref[:, r * HEAD_DIM:(r + 1) * HEAD_DIM]
    m_ref[...] = jnp.full(m_ref.shape, NEG, F32)
    acc_ref[...] = jnp.zeros(acc_ref.shape, F32)
    sb = sb_ref[0]
    scol = scol_ref[0]
    t_col = t0 + (lax.broadcasted_iota(jnp.int32, (rows, 1), 0) & (qb - 1))
    k_iota = lax.broadcasted_iota(jnp.int32, (1, tk), 1)
    p_row = lax.broadcasted_iota(jnp.int32, (SLC_LANES, LANES), 0)
    p_col = lax.broadcasted_iota(jnp.int32, (SLC_LANES, LANES), 1)

    def scores(kt):
        k0 = pl.multiple_of(kt * tk, tk)
        pick = ((p_row == kt * nb + p_col) & (p_col < nb)).astype(BF16)
        sbt = _dot(sb, pick).astype(BF16)
        for r in range(GQA_REP):
            qa_ref[r * qb:(r + 1) * qb, HEAD_DIM:] = sbt + srow_ref[0, r:r + 1, :]
        return _dot_nt(qa_ref[...], ka_ref[pl.ds(k0, tk), :])

    def softmax_pv(kt, s, causal):
        k0 = pl.multiple_of(kt * tk, tk)
        width = s.shape[1]
        if causal:
            s = jnp.where(k0 + k_iota[:, :width] <= t_col, s, NEG)
        shift = scol * (k0 - t0).astype(F32)
        m_old = m_ref[...]
        m_new = jnp.maximum(m_old, jnp.max(s, axis=-1, keepdims=True) + shift)
        alpha = jnp.exp2(m_old - m_new)
        p = jnp.exp2(s - (m_new - shift))
        acc_ref[...] = alpha * acc_ref[...] + _dot(p.astype(BF16), va_ref[pl.ds(k0, width), :])
        m_ref[...] = m_new

    n_below = lax.div(t0, tk)
    n_first = jnp.minimum(jnp.maximum(lax.div(t0 - reach_ref[pl.program_id(1)], tk), 0), n_below)
    s_ref[...] = scores(n_first)

    def body(kt, carry):
        s_cur = s_ref[...]
        s_next = scores(kt + 1)
        softmax_pv(kt, s_cur, False)
        s_ref[...] = s_next
        return carry

    lax.fori_loop(n_first, n_below, body, 0)
    groups = lax.div(t0 - n_below * tk, qb) + 1
    for j in range(1, tk // qb + 1):
        @pl.when(groups == j)
        def _(j=j):
            softmax_pv(n_below, s_ref[:, :j * qb], True)

    gates = jax.nn.sigmoid(gate_ref[0])
    acc = acc_ref[...]
    o = acc[:, :HEAD_DIM] / acc[:, HEAD_DIM:HEAD_DIM + 1]
    for r in range(GQA_REP):
        gate = gates[:, GQA_REP + r:GQA_REP + r + 1]
        o_ref[:, r * HEAD_DIM:(r + 1) * HEAD_DIM] = (o[r * qb:(r + 1) * qb] * gate).astype(o_ref.dtype)


def _slope_pieces(slopes):
    s_hi = slopes.astype(BF16)
    s_mid = (slopes - s_hi.astype(F32)).astype(BF16)
    s_lo = (slopes - s_hi.astype(F32) - s_mid.astype(F32)).astype(BF16)
    return jnp.stack([s_hi, s_mid, s_lo, s_hi, s_mid, s_lo], axis=1)


def _slope_rows(pieces, first_lane):
    rows = jnp.pad(pieces, ((0, 0), (first_lane, LANES - first_lane - pieces.shape[1]))).reshape(N_KV, GQA_REP, LANES)
    return jnp.pad(rows, ((0, 0), (0, 8 - GQA_REP), (0, 0)))


def _alibi_reach(slopes, gain_q, gain_k):
    bound = jnp.max(jnp.abs(gain_q)) * jnp.max(jnp.abs(gain_k)) * (HEAD_DIM ** 0.5 * LOG2E * 1.01)
    min_slope = jnp.min(slopes.reshape(N_KV, GQA_REP), axis=1)
    reach = jnp.ceil((2.0 * bound + EXP2_UNDERFLOW) / min_slope)
    return jnp.minimum(reach, 2.0 ** 30).astype(jnp.int32)


def _slc_attention(slopes, reach, proj, selbias, gates_r, *, b, t, k_col, v_col):
    n = b * t
    qb = SLC_Q_BLOCK
    nqb = t // qb
    rows = GQA_REP * qb
    nb = SLC_KEY_TILE // SLC_BLOCK
    rowblk = lambda bi, gi, c: bi * nqb + c
    srow = _slope_rows(_slope_pieces(slopes), nb)
    scol = jnp.repeat(slopes.reshape(N_KV, GQA_REP), qb, axis=1).reshape(N_KV, rows, 1)
    return pl.pallas_call(
        _slc_attn_kernel,
        grid=(b, N_KV, nqb),
        in_specs=[
            pl.BlockSpec(memory_space=pltpu.SMEM),
            pl.BlockSpec((qb, GQA_REP * HEAD_DIM), lambda bi, gi, c: (rowblk(bi, gi, c), gi)),
            pl.BlockSpec((t, HEAD_DIM), lambda bi, gi, c: (bi, k_col + gi)),
            pl.BlockSpec((t, HEAD_DIM), lambda bi, gi, c: (bi, v_col + gi)),
            pl.BlockSpec((1, qb, SLC_LANES), lambda bi, gi, c: (gi, rowblk(bi, gi, c), 0)),
            pl.BlockSpec((1, qb, LANES), lambda bi, gi, c: (gi, rowblk(bi, gi, c), 0)),
            pl.BlockSpec((1, 8, LANES), lambda bi, gi, c: (gi, 0, 0)),
            pl.BlockSpec((1, rows, 1), lambda bi, gi, c: (gi, 0, 0)),
        ],
        out_specs=pl.BlockSpec((qb, GQA_REP * HEAD_DIM), lambda bi, gi, c: (rowblk(bi, gi, c), gi)),
        out_shape=jax.ShapeDtypeStruct((n, N_HEADS * HEAD_DIM), BF16),
        scratch_shapes=[
            pltpu.VMEM((rows, 2 * HEAD_DIM), BF16),
            pltpu.VMEM((t, 2 * HEAD_DIM), BF16),
            pltpu.VMEM((t, 2 * HEAD_DIM), BF16),
            pltpu.VMEM((rows, 1), F32),
            pltpu.VMEM((rows, 2 * HEAD_DIM), F32),
            pltpu.VMEM((rows, SLC_KEY_TILE), F32),
        ],
        compiler_params=_params(("parallel", "parallel", "arbitrary")),
        name="nsa_slc_attn",
    )(reach, proj, proj, proj, selbias, gates_r, srow, scol)


def _sb_attn_kernel(q_ref, k_ref, v_ref, o_ref, acc_ref, carry_ref):
    ts = SB_TILE
    c = pl.program_id(2)
    acc_ref[...] = jnp.zeros(acc_ref.shape, F32)
    carry_ref[...] = jnp.zeros(carry_ref.shape, F32)
    r_idx = lax.broadcasted_iota(jnp.int32, (ts, 1), 0)
    c_idx = lax.broadcasted_iota(jnp.int32, (1, ts), 1)
    tri = (r_idx > c_idx).astype(BF16)
    diag = c_idx < r_idx

    def tile(jt, valid):
        k0 = pl.multiple_of(jt * ts, ts)
        heads = range(SB_HEADS_PER_STEP)
        cols = [slice(e * HEAD_DIM, (e + 1) * HEAD_DIM) for e in heads]
        carries = [carry_ref[e] for e in heads]
        zs = [_dot_nt(q_ref[:, cols[e]], k_ref[pl.ds(k0, ts), cols[e]]) for e in heads]
        lfs = []
        for e in heads:
            lf = -(jnp.maximum(zs[e], 0.0) + jnp.log2(1.0 + jnp.exp2(-jnp.abs(zs[e]))))
            lfs.append(lf if valid is None else jnp.where(valid, lf, 0.0))
        afters = []
        for e in heads:
            hi = lfs[e].astype(BF16)
            lo = (lfs[e] - hi.astype(F32)).astype(BF16)
            afters.append(_dot(hi, tri) + _dot(lo, tri))
        avs = []
        for e in heads:
            a = jnp.exp2(zs[e] + lfs[e] + afters[e] + carries[e])
            if valid is not None:
                a = jnp.where(valid, a, 0.0)
            avs.append(_dot(a.astype(BF16), v_ref[pl.ds(k0, ts), cols[e]]))
        for e in heads:
            acc_ref[e] += avs[e]
            carry_ref[e] = carries[e] + jnp.sum(lfs[e], axis=-1, keepdims=True)

    def live():
        return (jnp.max(carry_ref[...]) > SB_UNDERFLOW * LOG2E).astype(jnp.int32)

    tile(c, diag)

    def cond(state):
        jt, go = state
        return (jt >= 0) & (go > 0)

    def body(state):
        jt, _ = state
        tile(jt, None)
        return jt - 1, live()

    lax.while_loop(cond, body, (c - 1, live()))
    for e in range(SB_HEADS_PER_STEP):
        o_ref[:, e * HEAD_DIM:(e + 1) * HEAD_DIM] = acc_ref[e].astype(o_ref.dtype)


def _sb_attention(proj, *, b, t):
    n = b * t
    ts = SB_TILE
    nq = t // ts
    hs = SB_HEADS_PER_STEP
    width = hs * HEAD_DIM
    nhp = N_HEADS // hs
    return pl.pallas_call(
        _sb_attn_kernel,
        grid=(b, nhp, nq),
        in_specs=[
            pl.BlockSpec((ts, width), lambda bi, h, c: (bi * nq + c, h)),
            pl.BlockSpec((t, width), lambda bi, h, c: (bi, nhp + h), pipeline_mode=pl.Buffered(1)),
            pl.BlockSpec((t, width), lambda bi, h, c: (bi, 2 * nhp + h), pipeline_mode=pl.Buffered(1)),
        ],
        out_specs=pl.BlockSpec((ts, width), lambda bi, h, c: (bi * nq + c, h)),
        out_shape=jax.ShapeDtypeStruct((n, N_HEADS * HEAD_DIM), BF16),
        scratch_shapes=[pltpu.VMEM((hs, ts, HEAD_DIM), F32), pltpu.VMEM((hs, ts, 1), F32)],
        compiler_params=_params(("parallel", "parallel", "arbitrary")),
        name="sb_attn",
    )(proj, proj, proj)


def _row_copy(src_hbm, dst_ref, sem, src_row, dst_row):
    return pltpu.make_async_copy(src_hbm.at[pl.ds(src_row, 1), :], dst_ref.at[pl.ds(dst_row, 1), :], sem)


def _row_gather_start(src_hbm, dst_ref, sem, idx_ref, base, count):
    def body(r, carry):
        _row_copy(src_hbm, dst_ref, sem, idx_ref[base + r], r).start()
        return carry

    lax.fori_loop(0, count, body, 0, unroll=8)


def _row_gather_wait(src_hbm, dst_ref, sem):
    pltpu.make_async_copy(src_hbm.at[pl.ds(0, dst_ref.shape[0]), :], dst_ref, sem).wait()


def _expert_kernel(src_ref, te_ref, nu_ref, x_hbm, g_ref, wg_ref, wu_ref, wd_ref, o_ref, xbuf, wgb, wub, wdb, sem):
    tm = MOE_TILE
    i = pl.program_id(0)
    nt = pl.num_programs(0)
    n_used = nu_ref[0]
    slot = i % MOE_SLOTS
    nxt = (i + 2) % MOE_SLOTS
    next_base = jnp.where(i + 2 < nt, i + 2, i + 2 - nt) * tm

    @pl.when((i < n_used) & ((i == 0) | (te_ref[i] != te_ref[jnp.maximum(i - 1, 0)])))
    def _():
        wgb[...] = wg_ref[0, 0].astype(BF16)
        wub[...] = wu_ref[0, 0].astype(BF16)
        wdb[...] = wd_ref[0, 0].astype(BF16)

    @pl.when(i == 0)
    def _():
        _row_gather_start(x_hbm, xbuf.at[0], sem.at[0], src_ref, 0, tm)
        _row_gather_start(x_hbm, xbuf.at[1], sem.at[1], src_ref, tm, tm)

    @pl.when(i < n_used)
    def _():
        _row_gather_wait(x_hbm, xbuf.at[slot], sem.at[slot])
        x = xbuf[slot]
        ms = jnp.mean(x * x, axis=-1, keepdims=True)
        h = (x * lax.rsqrt(ms + EPS) * g_ref[...]).astype(BF16)
        for r in range(tm):
            _row_copy(x_hbm, xbuf.at[nxt], sem.at[nxt], src_ref[next_base + r], r).start()
        gate = _dot(h, wgb[...])
        up = _dot(h, wub[...])
        act = (gate * jax.nn.sigmoid(gate) * up).astype(BF16)
        o_ref[...] = _dot(act, wdb[...])

    @pl.when(i >= n_used)
    def _():
        o_ref[...] = jnp.zeros(o_ref.shape, o_ref.dtype)

    @pl.when(i == n_used - 1)
    def _():
        prev = (i + 1) % MOE_SLOTS
        _row_gather_wait(x_hbm, xbuf.at[prev], sem.at[prev])
        _row_gather_wait(x_hbm, xbuf.at[nxt], sem.at[nxt])


def _experts(src_tok, tile_expert, n_used, x, g, wg, wu, wd, layer):
    n, d = x.shape
    tm = MOE_TILE
    p_rows = src_tok.shape[0]
    de = wg.shape[3]
    grid_spec = pltpu.PrefetchScalarGridSpec(
        num_scalar_prefetch=3,
        grid=(p_rows // tm,),
        in_specs=[
            pl.BlockSpec(memory_space=pl.ANY),
            pl.BlockSpec((1, d), lambda i, s, te, nu: (0, 0)),
            pl.BlockSpec((1, 1, d, de), lambda i, s, te, nu: (layer, te[i], 0, 0)),
            pl.BlockSpec((1, 1, d, de), lambda i, s, te, nu: (layer, te[i], 0, 0)),
            pl.BlockSpec((1, 1, de, d), lambda i, s, te, nu: (layer, te[i], 0, 0)),
        ],
        out_specs=pl.BlockSpec((tm, d), lambda i, s, te, nu: (i, 0)),
        scratch_shapes=[
            pltpu.VMEM((MOE_SLOTS, tm, d), F32),
            pltpu.VMEM((d, de), BF16),
            pltpu.VMEM((d, de), BF16),
            pltpu.VMEM((de, d), BF16),
            pltpu.SemaphoreType.DMA((MOE_SLOTS,)),
        ],
    )
    return pl.pallas_call(
        _expert_kernel,
        grid_spec=grid_spec,
        out_shape=jax.ShapeDtypeStruct((p_rows, d), F32),
        compiler_params=_params(("arbitrary",)),
        name="moe_experts",
    )(src_tok, tile_expert, n_used, x, g.reshape(1, d), wg, wu, wd)


def _combine_kernel(pos_ref, x_ref, rt_ref, y_hbm, o_ref, ybuf, sem):
    tm = x_ref.shape[0]
    i = pl.program_id(0)
    nt = pl.num_programs(0)

    def start(tile, slot, unrolled):
        base = 2 * tile * tm
        for s in range(2):
            if unrolled:
                for r in range(tm):
                    _row_copy(y_hbm, ybuf.at[slot, s], sem.at[slot], pos_ref[base + 2 * r + s], r).start()
            else:
                def body(r, carry):
                    _row_copy(y_hbm, ybuf.at[slot, s], sem.at[slot], pos_ref[base + 2 * r + s], r).start()
                    return carry

                lax.fori_loop(0, tm, body, 0, unroll=8)

    @pl.when(i == 0)
    def _():
        start(0, 0, False)

    @pl.when(i + 1 < nt)
    def _():
        start(i + 1, (i + 1) % 2, True)

    slot = i % 2
    for s in range(2):
        _row_gather_wait(y_hbm, ybuf.at[slot, s], sem.at[slot])
    rt = rt_ref[...]
    o_ref[...] = x_ref[...] + rt[:, 2:3] * ybuf[slot, 0] + rt[:, 3:4] * ybuf[slot, 1]


def _combine(pos, x, routing, ys, *, tm):
    n, d = x.shape
    grid_spec = pltpu.PrefetchScalarGridSpec(
        num_scalar_prefetch=1,
        grid=(n // tm,),
        in_specs=[
            pl.BlockSpec((tm, d), lambda i, p: (i, 0)),
            pl.BlockSpec((tm, LANES), lambda i, p: (i, 0)),
            pl.BlockSpec(memory_space=pl.ANY),
        ],
        out_specs=pl.BlockSpec((tm, d), lambda i, p: (i, 0)),
        scratch_shapes=[pltpu.VMEM((2, 2, tm, d), F32), pltpu.SemaphoreType.DMA((2,))],
    )
    return pl.pallas_call(
        _combine_kernel,
        grid_spec=grid_spec,
        out_shape=jax.ShapeDtypeStruct((n, d), F32),
        compiler_params=_params(("arbitrary",)),
        name="moe_combine",
    )(pos, x, routing, ys)


def _source_tokens_kernel(pos_ref, src_ref):
    def zero(q, carry):
        src_ref[q] = 0
        return carry

    def put(p, carry):
        src_ref[pos_ref[p]] = p >> 1
        return carry

    lax.fori_loop(0, src_ref.shape[0], zero, 0, unroll=16)
    lax.fori_loop(0, pos_ref.shape[0], put, 0, unroll=16)


def _source_tokens(pos, p_rows):
    return pl.pallas_call(
        _source_tokens_kernel,
        in_specs=[pl.BlockSpec(memory_space=pltpu.SMEM)],
        out_specs=pl.BlockSpec(memory_space=pltpu.SMEM),
        out_shape=jax.ShapeDtypeStruct((p_rows,), jnp.int32),
        name="moe_source_tokens",
    )(pos)


def _hier_moe(x, routing, g, w_gate, w_up, w_down, layer):
    n, d = x.shape
    tm = MOE_TILE

    e_flat = routing[:, :2].astype(jnp.int32).reshape(-1)
    onehot = (e_flat[:, None] == jnp.arange(N_EXPERTS, dtype=jnp.int32)[None, :]).astype(jnp.int32)
    counts = jnp.sum(onehot, axis=0)
    rank = jnp.sum((jnp.cumsum(onehot, axis=0) - onehot) * onehot, axis=1)
    padded = ((counts + tm - 1) // tm) * tm
    ends = jnp.cumsum(padded)
    pos = (ends - padded)[e_flat] + rank
    p_rows = 2 * n + N_EXPERTS * tm
    pos = pos.astype(jnp.int32)
    src_tok = _source_tokens(pos, p_rows)
    tile_start = jnp.arange(p_rows // tm, dtype=jnp.int32) * tm
    tile_expert = jnp.minimum(jnp.sum((ends[None, :] <= tile_start[:, None]).astype(jnp.int32), axis=1), N_EXPERTS - 1)
    n_used = (ends[-1] // tm).astype(jnp.int32).reshape(1)

    ys = _experts(src_tok, tile_expert, n_used, x, g, w_gate, w_up, w_down, layer)
    return _combine(pos, x, routing, ys, tm=256 if n % 256 == 0 else n)


def _alibi_slopes():
    return jnp.exp2(-8.0 * jnp.arange(1, N_HEADS + 1, dtype=F32) / N_HEADS)


def _cmp_to_slc(n_cmp_pad, n_slc):
    cs = np.arange(n_cmp_pad) * CMP_STRIDE
    ss = np.arange(SLC_LANES) * SLC_BLOCK
    lo = np.maximum(cs[:, None], ss[None, :])
    hi = np.minimum(cs[:, None] + CMP_BLOCK, ss[None, :] + SLC_BLOCK)
    m = np.maximum(hi - lo, 0).astype(np.float32) / CMP_BLOCK
    m[:, n_slc:] = 0.0
    return m


def _nsa_mixer(x, g_norm, w_in, qk_gain, cmp_pe, cmp_w1, cmp_w2, w_out, router, *, b, t):
    n, d = x.shape
    q_dim = N_HEADS * HEAD_DIM
    kv_dim = N_KV * HEAD_DIM
    scale = HEAD_DIM ** -0.5 * LOG2E
    main = q_dim + 6 * kv_dim
    ones = jnp.ones((kv_dim,), F32)
    tile4 = lambda v: jnp.tile(v, N_KV)
    cgain = jnp.concatenate([jnp.tile(qk_gain[0], N_HEADS) * scale, ones, ones, tile4(qk_gain[2]), ones, tile4(qk_gain[3]), ones])
    tiles_per = lambda cols: cols // PROJ_COLS
    q_tiles = tiles_per(q_dim)
    kv_tiles = tiles_per(kv_dim)
    norm_tiles = list(range(q_tiles)) + [q_tiles + 2 * kv_tiles + j for j in range(kv_tiles)] + [q_tiles + 4 * kv_tiles + j for j in range(kv_tiles)]
    n_gate = 3 * N_HEADS
    w_gate = jnp.pad(w_in[:, main:], ((0, 0), (0, LANES - n_gate))).astype(BF16)
    cmp_tiles = [q_tiles + j for j in range(2 * kv_tiles)]
    proj, gate_logits, kv_cmp = _norm_matmul(x, g_norm, w_in[:, :main].astype(BF16), cgain, norm_tiles, extra_w=w_gate, f32_tiles=cmp_tiles)
    gr = gate_logits[:, :n_gate].reshape(n, 3, N_KV, GQA_REP).transpose(2, 0, 1, 3).reshape(N_KV, n, 3 * GQA_REP)
    gates_r = jnp.pad(gr, ((0, 0), (0, 0), (0, LANES - 3 * GQA_REP)))

    nc = t // CMP_STRIDE
    half = CMP_BLOCK // 2
    w1cat = jnp.concatenate([cmp_w1[:, :half], cmp_w1[:, half:]], axis=-1)
    pe_rows = jnp.pad(jnp.stack([cmp_pe[:, :half], cmp_pe[:, half:]], axis=2), ((0, 0), (0, 0), (0, 6), (0, 0)))
    cmp_gain = jnp.stack([qk_gain[1], jnp.ones((HEAD_DIM,), F32)]).reshape(2, 1, HEAD_DIM)
    cmp_kv = _compress(kv_cmp, w1cat.astype(BF16), pe_rows.astype(BF16), cmp_w2.astype(BF16), cmp_gain, b=b, t=t)

    slopes = _alibi_slopes() * LOG2E
    pieces = _slope_pieces(slopes)
    pad_flag = jnp.full((N_HEADS, 1), NEG, F32).astype(BF16)
    srow = _slope_rows(jnp.concatenate([pieces, pad_flag], axis=1), 0)
    n_slc = t // SLC_BLOCK
    msel = jnp.asarray(_cmp_to_slc(nc, n_slc), BF16)
    q_blocks = q_dim // HEAD_DIM
    o_c, selbias = _cmp_attention(proj, cmp_kv, gates_r, msel, srow, b=b, t=t, top_k=min(SLC_TOPK, n_slc))
    reach = _alibi_reach(slopes, qk_gain[0], qk_gain[2])
    o_s = _slc_attention(slopes, reach, proj, selbias, gates_r, b=b, t=t, k_col=q_blocks + 2 * N_KV, v_col=q_blocks + 3 * N_KV)
    o_w = _win_attention(proj, gates_r, srow, b=b, t=t, k_col=q_blocks + 4 * N_KV, v_col=q_blocks + 5 * N_KV)
    return _proj_residual([o_c, o_s, o_w], w_out.astype(BF16), x, router)


def _sb_mixer(x, g_norm, w_in, w_out, router, *, b, t):
    q_dim = N_HEADS * HEAD_DIM
    scale = HEAD_DIM ** -0.5 * LOG2E
    cgain = jnp.concatenate([jnp.full((q_dim,), scale, F32), jnp.ones((2 * q_dim,), F32)])
    proj = _norm_matmul(x, g_norm, w_in.astype(BF16), cgain, ())
    o = _sb_attention(proj, b=b, t=t)
    return _proj_residual([o], w_out.astype(BF16), x, router)


def kernel(x, norm_mix, norm_ffn, nsa_w_in, nsa_qk_gain, nsa_cmp_pe, nsa_cmp_w1, nsa_cmp_w2, nsa_w_out, sb_w_in, sb_w_out, moe_w_group, moe_b_group, moe_w_router, moe_b_router, moe_w_gate, moe_w_up, moe_w_down):
    b, t, d = x.shape
    depth = norm_mix.shape[0]
    xf = x.reshape(b * t, d)
    for i in range(depth):
        j = i // 2
        router = _router_params(norm_ffn[i], moe_w_group[i], moe_b_group[i], moe_w_router[i], moe_b_router[i])
        if i % 2 == 0:
            xf, routing = _nsa_mixer(xf, norm_mix[i], nsa_w_in[j], nsa_qk_gain[j], nsa_cmp_pe[j], nsa_cmp_w1[j], nsa_cmp_w2[j], nsa_w_out[j], router, b=b, t=t)
        else:
            xf, routing = _sb_mixer(xf, norm_mix[i], sb_w_in[j], sb_w_out[j], router, b=b, t=t)
        xf = _hier_moe(xf, routing, norm_ffn[i], moe_w_gate, moe_w_up, moe_w_down, i)
    return xf.reshape(b, t, d)
```

```python
import functools

import jax
import jax.numpy as jnp
import numpy as np
from jax import lax
from jax.experimental import pallas as pl
from jax.experimental.pallas import tpu as pltpu

F32 = jnp.float32
BF16 = jnp.bfloat16

HEAD_DIM = 128
N_HEADS = 16
N_KV = 4
GQA_REP = N_HEADS // N_KV
CMP_BLOCK = 32
CMP_STRIDE = 16
SLC_BLOCK = 64
SLC_TOPK = 16
WINDOW = 512
Q_BLOCK = 256
WIN_SUB = 4
N_GROUPS = 4
EXPERTS_PER_GROUP = 8
N_EXPERTS = N_GROUPS * EXPERTS_PER_GROUP
EPS = 1e-6
NEG = -1e30
LANES = 128
VMEM_LIMIT = 56 * 1024 * 1024

PROJ_ROWS = 512
PROJ_COLS = 512
CMP_Q_BLOCK = 256
CMP_SUB = 2
SLC_Q_BLOCK = 512
SLC_LANES = 128
SB_TILE = 256
SB_HEADS_PER_STEP = 8
SLC_KEY_TILE = 1024
LOG2E = 1.4426950408889634
SB_UNDERFLOW = -105.0
MOE_TILE = 256
MOE_SLOTS = 3
EXP2_UNDERFLOW = 152.0
AUG_PAD_LANE = 6


def _dot(a, b):
    return jnp.dot(a, b, preferred_element_type=F32)


def _dot_nt(a, b):
    return lax.dot_general(a, b, (((1,), (1,)), ((), ())), preferred_element_type=F32)


def _params(sem):
    return pltpu.CompilerParams(dimension_semantics=sem, vmem_limit_bytes=VMEM_LIMIT)


def _resident(shape):
    return pl.BlockSpec(shape, lambda *_: (0,) * len(shape), pipeline_mode=pl.Buffered(1))


def _norm_matmul_kernel(x_ref, g_ref, w_ref, cg_ref, *rest, norm_tiles, f32_tiles, has_extra):
    if has_extra:
        wx_ref, o_ref, ox_ref, of_ref = rest
    else:
        (o_ref,) = rest
    x = x_ref[...]
    ms = jnp.mean(x * x, axis=-1, keepdims=True)
    h = (x * lax.rsqrt(ms + EPS) * g_ref[...]).astype(BF16)
    tn = PROJ_COLS
    for j in range(w_ref.shape[1] // tn):
        y = _dot(h, w_ref[:, j * tn:(j + 1) * tn])
        if j in f32_tiles:
            k = f32_tiles.index(j)
            of_ref[:, k * tn:(k + 1) * tn] = y
        for hh in range(tn // LANES):
            yh = y[:, hh * LANES:(hh + 1) * LANES]
            if j in norm_tiles:
                yh = yh * lax.rsqrt(jnp.mean(yh * yh, axis=-1, keepdims=True) + EPS)
            sl = slice(j * tn + hh * LANES, j * tn + (hh + 1) * LANES)
            o_ref[:, sl] = (yh * cg_ref[:, sl]).astype(o_ref.dtype)
    if has_extra:
        ox_ref[...] = _dot(h, wx_ref[...])


def _norm_matmul(x, g, w, cgain, norm_tiles, extra_w=None, f32_tiles=()):
    n, d = x.shape
    m = w.shape[1]
    tm = PROJ_ROWS
    has_extra = extra_w is not None
    in_specs = [pl.BlockSpec((tm, d), lambda i: (i, 0)), _resident((1, d)), _resident((d, m)), _resident((1, m))]
    out_specs = [pl.BlockSpec((tm, m), lambda i: (i, 0))]
    out_shape = [jax.ShapeDtypeStruct((n, m), BF16)]
    args = [x, g.reshape(1, d), w, cgain.reshape(1, m)]
    if has_extra:
        mx = extra_w.shape[1]
        mf = len(f32_tiles) * PROJ_COLS
        in_specs.append(_resident((d, mx)))
        out_specs += [pl.BlockSpec((tm, mx), lambda i: (i, 0)), pl.BlockSpec((tm, mf), lambda i: (i, 0))]
        out_shape += [jax.ShapeDtypeStruct((n, mx), F32), jax.ShapeDtypeStruct((n, mf), F32)]
        args.append(extra_w)
    out = pl.pallas_call(
        functools.partial(_norm_matmul_kernel, norm_tiles=tuple(norm_tiles), f32_tiles=tuple(f32_tiles), has_extra=has_extra),
        grid=(n // tm,),
        in_specs=in_specs,
        out_specs=out_specs,
        out_shape=out_shape,
        compiler_params=_params(("parallel",)),
        name="norm_matmul",
    )(*args)
    return out if has_extra else out[0]


def _first_max(vals, lane):
    m = jnp.max(vals, axis=-1, keepdims=True)
    idx = jnp.min(jnp.where(vals == m, lane, LANES), axis=-1, keepdims=True)
    return m, idx


def _route(x, g_ref, w_ref, b_ref):
    ms = jnp.mean(x * x, axis=-1, keepdims=True)
    h = x * lax.rsqrt(ms + EPS) * g_ref[...]
    h_hi = h.astype(BF16)
    h_lo = (h - h_hi.astype(F32)).astype(BF16)
    logits = _dot(h_hi, w_ref[0]) + _dot(h_hi, w_ref[1]) + _dot(h_lo, w_ref[0]) + b_ref[...]
    lane = lax.broadcasted_iota(jnp.int32, (1, LANES), 1)
    gl = jnp.where(lane < N_GROUPS, logits, NEG)
    ge = jnp.exp(gl - jnp.max(gl, axis=-1, keepdims=True))
    pg_all = ge / jnp.sum(ge, axis=-1, keepdims=True)
    pg, gsel = _first_max(jnp.where(lane < N_GROUPS, pg_all, -1.0), lane)
    lo = N_GROUPS + gsel * EXPERTS_PER_GROUP
    in_group = (lane >= lo) & (lane < lo + EXPERTS_PER_GROUP)
    el = jnp.where(in_group, logits, NEG)
    ee = jnp.exp(el - jnp.max(el, axis=-1, keepdims=True))
    pe_all = jnp.where(in_group, ee / jnp.sum(ee, axis=-1, keepdims=True), -1.0)
    p1, i1 = _first_max(pe_all, lane)
    p2, i2 = _first_max(jnp.where(lane == i1, -1.0, pe_all), lane)
    denom = p1 + p2
    w1 = pg * p1 / denom
    w2 = pg * p2 / denom
    e1 = (i1 - N_GROUPS).astype(F32)
    e2 = (i2 - N_GROUPS).astype(F32)
    return jnp.where(lane == 0, e1, jnp.where(lane == 1, e2, jnp.where(lane == 2, w1, jnp.where(lane == 3, w2, 0.0))))


def _proj_residual_kernel(*refs, n_in):
    a_refs = refs[:n_in]
    w_ref, r_ref, g_ref, wr_ref, br_ref, o_ref, rt_ref = refs[n_in:]
    if n_in == 1:
        a = a_refs[0][...]
    else:
        acc = a_refs[0][...].astype(F32)
        for a_ref in a_refs[1:]:
            acc = acc + a_ref[...].astype(F32)
        a = acc.astype(BF16)
    tn = PROJ_COLS
    for j in range(w_ref.shape[1] // tn):
        sl = slice(j * tn, (j + 1) * tn)
        o_ref[:, sl] = r_ref[:, sl] + _dot(a, w_ref[:, sl])
    rt_ref[...] = _route(o_ref[...], g_ref, wr_ref, br_ref)


def _proj_residual(branches, w, res, router):
    n, k = branches[0].shape
    m = w.shape[1]
    tm = PROJ_ROWS
    n_in = len(branches)
    g_ffn, w_split, b_r = router
    return pl.pallas_call(
        functools.partial(_proj_residual_kernel, n_in=n_in),
        grid=(n // tm,),
        in_specs=[pl.BlockSpec((tm, k), lambda i: (i, 0)) for _ in range(n_in)]
        + [_resident((k, m)), pl.BlockSpec((tm, m), lambda i: (i, 0)), _resident((1, m)), _resident(w_split.shape), _resident((1, LANES))],
        out_specs=[pl.BlockSpec((tm, m), lambda i: (i, 0)), pl.BlockSpec((tm, LANES), lambda i: (i, 0))],
        out_shape=[jax.ShapeDtypeStruct((n, m), F32), jax.ShapeDtypeStruct((n, LANES), F32)],
        compiler_params=_params(("parallel",)),
        name="proj_residual",
    )(*branches, w, res, g_ffn.reshape(1, m), w_split, b_r)


def _router_params(g_ffn, w_group, b_group, w_router, b_router):
    n_logit = N_GROUPS + N_EXPERTS
    w_r = jnp.pad(jnp.concatenate([w_group, w_router], axis=1), ((0, 0), (0, LANES - n_logit)))
    b_r = jnp.pad(jnp.concatenate([b_group, b_router]), (0, LANES - n_logit)).reshape(1, LANES)
    w_hi = w_r.astype(BF16)
    return g_ffn, jnp.stack([w_hi, (w_r - w_hi.astype(F32)).astype(BF16)]), b_r


def _cmp_kernel(kv_ref, w1_ref, pe_ref, w2_ref, gain_ref, o_ref):
    nc = kv_ref.shape[0] // CMP_STRIDE
    p = jnp.zeros((nc, 2 * HEAD_DIM), F32)
    pb = jnp.zeros((8, 2 * HEAD_DIM), F32)
    for l in range(CMP_STRIDE):
        w1 = w1_ref[0, l]
        p = p + _dot(kv_ref[pl.ds(l, nc, stride=CMP_STRIDE), :].astype(BF16), w1)
        pb = pb + _dot(pe_ref[0, l], w1)
    bias = pb[0:1, :HEAD_DIM] + pb[1:2, HEAD_DIM:]
    second = pltpu.roll(p[:, HEAD_DIM:], shift=nc - 1, axis=0)
    hid = jax.nn.gelu(p[:, :HEAD_DIM] + second + bias)
    c = _dot(hid.astype(BF16), w2_ref[0])
    nf = (pl.program_id(1) == 0).astype(F32)
    ms = jnp.mean(c * c, axis=-1, keepdims=True)
    fac = nf * lax.rsqrt(ms + EPS) + (1.0 - nf)
    o_ref[0, 0, 0] = (c * fac * gain_ref[0]).astype(o_ref.dtype)


def _compress(kv, w1cat, pe_rows, w2, gains, *, b, t):
    g, cs, hd = N_KV, CMP_STRIDE, HEAD_DIM
    nc = t // cs
    return pl.pallas_call(
        _cmp_kernel,
        grid=(b, 2, g),
        in_specs=[
            pl.BlockSpec((t, hd), lambda bi, k, gi: (bi, k * g + gi)),
            pl.BlockSpec((1, cs, hd, 2 * HEAD_DIM), lambda bi, k, gi: (k, 0, 0, 0)),
            pl.BlockSpec((1, cs, 8, hd), lambda bi, k, gi: (k, 0, 0, 0)),
            pl.BlockSpec((1, HEAD_DIM, HEAD_DIM), lambda bi, k, gi: (k, 0, 0)),
            pl.BlockSpec((1, 1, HEAD_DIM), lambda bi, k, gi: (k, 0, 0)),
        ],
        out_specs=pl.BlockSpec((1, 1, 1, nc, HEAD_DIM), lambda bi, k, gi: (bi, k, gi, 0, 0)),
        out_shape=jax.ShapeDtypeStruct((b, 2, g, nc, HEAD_DIM), BF16),
        compiler_params=_params(("parallel", "parallel", "parallel")),
        name="nsa_compress",
    )(kv, w1cat, pe_rows, w2, gains)


def _cmp_attn_kernel(q_ref, kc_ref, vc_ref, gate_ref, msel_ref, srow_ref, oc_ref, sb_ref, qa_ref, ka_ref, va_ref, *, top_k):
    c = pl.program_id(2)
    qb = CMP_Q_BLOCK
    t_step = c * (CMP_SUB * qb)
    rows = GQA_REP * qb
    nc = ka_ref.shape[0]
    subs = range(CMP_SUB)

    @pl.when(c == 0)
    def _():
        n_idx = lax.broadcasted_iota(jnp.int32, (nc, LANES), 0)
        lane = lax.broadcasted_iota(jnp.int32, (nc, LANES), 1)
        aug = jnp.where(lane < 3, ((n_idx >> 7) << 11).astype(F32), jnp.where(lane < 6, ((n_idx & 127) << 4).astype(F32), 0.0))
        ka_ref[:, :HEAD_DIM] = kc_ref[0, 0, 0]
        ka_ref[:, HEAD_DIM:] = aug.astype(BF16)
        va_ref[:, :HEAD_DIM] = vc_ref[0, 0, 0]
        va_ref[:, HEAD_DIM:] = (lane == 0).astype(BF16)

    for u in subs:
        for r in range(GQA_REP):
            dst = slice((u * GQA_REP + r) * qb, (u * GQA_REP + r + 1) * qb)
            qa_ref[dst, :HEAD_DIM] = q_ref[u * qb:(u + 1) * qb, r * HEAD_DIM:(r + 1) * HEAD_DIM]
            qa_ref[dst, HEAD_DIM:] = jnp.broadcast_to(srow_ref[0, r:r + 1, :], (qb, LANES))
    row_q = lax.broadcasted_iota(jnp.int32, (rows, 1), 0) & (qb - 1)
    lane_q = lax.broadcasted_iota(jnp.int32, (1, qb), 1)
    gates = jax.nn.sigmoid(gate_ref[0])

    def attend(k):
        ncol = k * LANES
        nrow = ncol * CMP_STRIDE // SLC_BLOCK
        cend = lax.broadcasted_iota(jnp.int32, (1, ncol), 1) * CMP_STRIDE + (CMP_BLOCK - 1)
        j_col = lax.broadcasted_iota(jnp.int32, (nrow, 1), 0)
        msel = msel_ref[:ncol, :]
        scores = [_dot_nt(qa_ref[u * rows:(u + 1) * rows, :], ka_ref[:ncol, :]) for u in subs]
        es, ms = [], []
        for u in subs:
            s = jnp.where(cend <= t_step + u * qb + row_q, scores[u], NEG)
            m = jnp.max(s, axis=-1, keepdims=True)
            ms.append(m)
            es.append(jnp.exp2(s - m))
        accs = [_dot(es[u].astype(BF16), va_ref[:ncol, :]) for u in subs]
        imps = []
        for u in subs:
            inv = jnp.where(ms[u] > 0.5 * NEG, 1.0 / accs[u][:, HEAD_DIM:HEAD_DIM + 1], 0.0)
            psum = jnp.zeros((qb, ncol), F32)
            for r in range(GQA_REP):
                rs = slice(r * qb, (r + 1) * qb)
                psum = psum + es[u][rs] * inv[rs]
                gate = gates[u * qb:(u + 1) * qb, r:r + 1]
                oc_ref[u * qb:(u + 1) * qb, r * HEAD_DIM:(r + 1) * HEAD_DIM] = (accs[u][rs, :HEAD_DIM] * (inv[rs] * gate)).astype(oc_ref.dtype)
            p_hi = psum.astype(BF16)
            rem = psum - p_hi.astype(F32)
            p_mid = rem.astype(BF16)
            p_lo = (rem - p_mid.astype(F32)).astype(BF16)
            imps.append(_dot(p_hi, msel) + _dot(p_mid, msel) + _dot(p_lo, msel))

        score, keep = [], []
        for u in subs:
            t_row = t_step + u * qb + lane_q
            forced = (j_col == 0) | (j_col == (t_row >> 6))
            causal = (j_col << 6) <= t_row
            score.append(jnp.where(forced, -NEG, jnp.where(causal, imps[u].T[:nrow], NEG)))
            keep.append(jnp.zeros((nrow, qb), F32))
        for _ in range(top_k):
            for u in subs:
                mx = jnp.max(score[u], axis=0, keepdims=True)
                first = jnp.min(jnp.where(score[u] == mx, j_col, SLC_LANES), axis=0, keepdims=True)
                hit = j_col == first
                keep[u] = jnp.where(hit, (mx > 0.5 * NEG).astype(F32), keep[u])
                score[u] = jnp.where(hit, -3e38, score[u])
        for u in subs:
            bias_t = jnp.where(keep[u] > 0.5, 0.0, NEG)
            if nrow < SLC_LANES:
                bias_t = jnp.concatenate([bias_t, jnp.full((SLC_LANES - nrow, qb), NEG, F32)], axis=0)
            sb_ref[0, u * qb:(u + 1) * qb, :] = bias_t.T.astype(sb_ref.dtype)

    n_variants = nc // LANES
    needed = jnp.minimum(lax.div(lax.div(t_step + CMP_SUB * qb - CMP_BLOCK, CMP_STRIDE), LANES) + 1, n_variants)
    for k in range(1, n_variants + 1):
        pl.when(needed == k)(functools.partial(attend, k))


def _cmp_attention(proj, cmp_kv, gates_r, msel, srow, *, b, t, top_k):
    n = b * t
    qb = CMP_SUB * CMP_Q_BLOCK
    nqb = t // qb
    rows = GQA_REP * qb
    nc = cmp_kv.shape[3]
    rowblk = lambda bi, gi, c: bi * nqb + c
    return pl.pallas_call(
        functools.partial(_cmp_attn_kernel, top_k=top_k),
        grid=(b, N_KV, nqb),
        in_specs=[
            pl.BlockSpec((qb, GQA_REP * HEAD_DIM), lambda bi, gi, c: (rowblk(bi, gi, c), gi)),
            pl.BlockSpec((1, 1, 1, nc, HEAD_DIM), lambda bi, gi, c: (bi, 0, gi, 0, 0)),
            pl.BlockSpec((1, 1, 1, nc, HEAD_DIM), lambda bi, gi, c: (bi, 1, gi, 0, 0)),
            pl.BlockSpec((1, qb, LANES), lambda bi, gi, c: (gi, rowblk(bi, gi, c), 0)),
            pl.BlockSpec((nc, SLC_LANES), lambda bi, gi, c: (0, 0)),
            pl.BlockSpec((1, 8, LANES), lambda bi, gi, c: (gi, 0, 0)),
        ],
        out_specs=[
            pl.BlockSpec((qb, GQA_REP * HEAD_DIM), lambda bi, gi, c: (rowblk(bi, gi, c), gi)),
            pl.BlockSpec((1, qb, SLC_LANES), lambda bi, gi, c: (gi, rowblk(bi, gi, c), 0)),
        ],
        out_shape=[
            jax.ShapeDtypeStruct((n, N_HEADS * HEAD_DIM), BF16),
            jax.ShapeDtypeStruct((N_KV, n, SLC_LANES), BF16),
        ],
        scratch_shapes=[
            pltpu.VMEM((rows, 2 * HEAD_DIM), BF16),
            pltpu.VMEM((nc, 2 * HEAD_DIM), BF16),
            pltpu.VMEM((nc, 2 * HEAD_DIM), BF16),
        ],
        compiler_params=_params(("parallel", "parallel", "arbitrary")),
        name="nsa_cmp_attn",
    )(proj, cmp_kv, cmp_kv, gates_r, msel, srow)


def _win_attn_kernel(q_ref, k_ref, v_ref, gate_ref, srow_ref, o_ref, qa_ref, ka_ref, va_ref):
    c = pl.program_id(2)
    t0 = pl.multiple_of(c * (WIN_SUB * Q_BLOCK), WIN_SUB * Q_BLOCK)
    rows = GQA_REP * Q_BLOCK
    span = Q_BLOCK + WINDOW
    t_len = k_ref.shape[0]

    @pl.when(c == 0)
    def _():
        pos = lax.broadcasted_iota(jnp.int32, (t_len, LANES), 0)
        lane = lax.broadcasted_iota(jnp.int32, (t_len, LANES), 1)
        aug = jnp.where(lane < 3, ((pos >> 6) << 6).astype(F32), jnp.where(lane < 6, (pos & 63).astype(F32), 0.0))
        pad_lane = lax.broadcasted_iota(jnp.int32, (WINDOW, LANES), 1)
        ka_ref[:WINDOW, :HEAD_DIM] = jnp.zeros((WINDOW, HEAD_DIM), BF16)
        ka_ref[:WINDOW, HEAD_DIM:] = (pad_lane == AUG_PAD_LANE).astype(BF16)
        ka_ref[WINDOW:, :HEAD_DIM] = k_ref[...]
        ka_ref[WINDOW:, HEAD_DIM:] = aug.astype(BF16)
        va_ref[:WINDOW, :] = jnp.zeros((WINDOW, 2 * HEAD_DIM), BF16)
        va_ref[WINDOW:, :HEAD_DIM] = v_ref[...]
        va_ref[WINDOW:, HEAD_DIM:] = (lane == 0).astype(BF16)

    i_col = lax.broadcasted_iota(jnp.int32, (rows, 1), 0) & (Q_BLOCK - 1)
    kk = lax.broadcasted_iota(jnp.int32, (1, Q_BLOCK), 1)
    gates = jax.nn.sigmoid(gate_ref[0])
    subs = range(WIN_SUB)
    for u in subs:
        for r in range(GQA_REP):
            dst = slice((u * GQA_REP + r) * Q_BLOCK, (u * GQA_REP + r + 1) * Q_BLOCK)
            qa_ref[dst, :HEAD_DIM] = q_ref[u * Q_BLOCK:(u + 1) * Q_BLOCK, r * HEAD_DIM:(r + 1) * HEAD_DIM]
            qa_ref[dst, HEAD_DIM:] = jnp.broadcast_to(srow_ref[0, r:r + 1, :], (Q_BLOCK, LANES))
    starts = [t0 + u * Q_BLOCK for u in subs]
    scores = [_dot_nt(qa_ref[u * rows:(u + 1) * rows, :], ka_ref[pl.ds(starts[u], span), :]) for u in subs]
    weights = []
    for u in subs:
        s = scores[u]
        first = jnp.where(kk > i_col, s[:, :Q_BLOCK], NEG)
        mid = s[:, Q_BLOCK:WINDOW]
        last = jnp.where(kk <= i_col, s[:, WINDOW:], NEG)
        m = jnp.maximum(jnp.maximum(jnp.max(first, axis=-1, keepdims=True), jnp.max(mid, axis=-1, keepdims=True)),
                        jnp.max(last, axis=-1, keepdims=True))
        weights.append([jnp.exp2(part - m).astype(BF16) for part in (first, mid, last)])
    accs = []
    for u in subs:
        p_first, p_mid, p_last = weights[u]
        accs.append(_dot(p_first, va_ref[pl.ds(starts[u], Q_BLOCK), :])
                    + _dot(p_mid, va_ref[pl.ds(starts[u] + Q_BLOCK, WINDOW - Q_BLOCK), :])
                    + _dot(p_last, va_ref[pl.ds(starts[u] + WINDOW, Q_BLOCK), :]))
    for u in subs:
        o = accs[u][:, :HEAD_DIM] / accs[u][:, HEAD_DIM:HEAD_DIM + 1]
        for r in range(GQA_REP):
            gate = gates[u * Q_BLOCK:(u + 1) * Q_BLOCK, 2 * GQA_REP + r:2 * GQA_REP + r + 1]
            o_ref[u * Q_BLOCK:(u + 1) * Q_BLOCK, r * HEAD_DIM:(r + 1) * HEAD_DIM] = (o[r * Q_BLOCK:(r + 1) * Q_BLOCK] * gate).astype(o_ref.dtype)


def _win_attention(proj, gates_r, srow, *, b, t, k_col, v_col):
    n = b * t
    step_q = WIN_SUB * Q_BLOCK
    nqb = t // step_q
    rows = GQA_REP * Q_BLOCK
    rowblk = lambda bi, gi, c: bi * nqb + c
    return pl.pallas_call(
        _win_attn_kernel,
        grid=(b, N_KV, nqb),
        in_specs=[
            pl.BlockSpec((step_q, GQA_REP * HEAD_DIM), lambda bi, gi, c: (rowblk(bi, gi, c), gi)),
            pl.BlockSpec((t, HEAD_DIM), lambda bi, gi, c: (bi, k_col + gi)),
            pl.BlockSpec((t, HEAD_DIM), lambda bi, gi, c: (bi, v_col + gi)),
            pl.BlockSpec((1, step_q, LANES), lambda bi, gi, c: (gi, rowblk(bi, gi, c), 0)),
            pl.BlockSpec((1, 8, LANES), lambda bi, gi, c: (gi, 0, 0)),
        ],
        out_specs=pl.BlockSpec((step_q, GQA_REP * HEAD_DIM), lambda bi, gi, c: (rowblk(bi, gi, c), gi)),
        out_shape=jax.ShapeDtypeStruct((n, N_HEADS * HEAD_DIM), BF16),
        scratch_shapes=[
            pltpu.VMEM((WIN_SUB * rows, 2 * HEAD_DIM), BF16),
            pltpu.VMEM((t + WINDOW, 2 * HEAD_DIM), BF16),
            pltpu.VMEM((t + WINDOW, 2 * HEAD_DIM), BF16),
        ],
        compiler_params=_params(("parallel", "parallel", "arbitrary")),
        name="nsa_win_attn",
    )(proj, proj, proj, gates_r, srow)


def _slc_attn_kernel(reach_ref, q_ref, k_ref, v_ref, sb_ref, gate_ref, srow_ref, scol_ref, o_ref, qa_ref, ka_ref, va_ref, m_ref, acc_ref, s_ref):
    c = pl.program_id(2)
    qb = SLC_Q_BLOCK
    t0 = c * qb
    rows = GQA_REP * qb
    tk = SLC_KEY_TILE
    nb = tk // SLC_BLOCK
    t_len = k_ref.shape[0]

    @pl.when(c == 0)
    def _():
        pos = lax.broadcasted_iota(jnp.int32, (t_len, LANES), 0)
        lane = lax.broadcasted_iota(jnp.int32, (t_len, LANES), 1)
        off = pos & (tk - 1)
        blk = (pos >> 6) & (nb - 1)
        aug = jnp.where(lane < nb, (blk == lane).astype(F32),
                        jnp.where(lane < nb + 3, ((off >> 3) << 3).astype(F32),
                                  jnp.where(lane < nb + 6, (off & 7).astype(F32), 0.0)))
        ka_ref[:, :HEAD_DIM] = k_ref[...]
        ka_ref[:, HEAD_DIM:] = aug.astype(BF16)
        va_ref[:, :HEAD_DIM] = v_ref[...]
        va_ref[:, HEAD_DIM:] = (lane == 0).astype(BF16)

    for r in range(GQA_REP):
        qa_ref[r * qb:(r + 1) * qb, :HEAD_DIM] = q_ref[:, r * HEAD_DIM:(r + 1) * HEAD_DIM]
    m_ref[...] = jnp.full(m_ref.shape, NEG, F32)
    acc_ref[...] = jnp.zeros(acc_ref.shape, F32)
    sb = sb_ref[0]
    scol = scol_ref[0]
    t_col = t0 + (lax.broadcasted_iota(jnp.int32, (rows, 1), 0) & (qb - 1))
    k_iota = lax.broadcasted_iota(jnp.int32, (1, tk), 1)
    p_row = lax.broadcasted_iota(jnp.int32, (SLC_LANES, LANES), 0)
    p_col = lax.broadcasted_iota(jnp.int32, (SLC_LANES, LANES), 1)

    def scores(kt):
        k0 = pl.multiple_of(kt * tk, tk)
        pick = ((p_row == kt * nb + p_col) & (p_col < nb)).astype(BF16)
        sbt = _dot(sb, pick).astype(BF16)
        for r in range(GQA_REP):
            qa_ref[r * qb:(r + 1) * qb, HEAD_DIM:] = sbt + srow_ref[0, r:r + 1, :]
        return _dot_nt(qa_ref[...], ka_ref[pl.ds(k0, tk), :])

    def softmax_pv(kt, s, causal):
        k0 = pl.multiple_of(kt * tk, tk)
        width = s.shape[1]
        if causal:
            s = jnp.where(k0 + k_iota[:, :width] <= t_col, s, NEG)
        shift = scol * (k0 - t0).astype(F32)
        m_old = m_ref[...]
        m_new = jnp.maximum(m_old, jnp.max(s, axis=-1, keepdims=True) + shift)
        alpha = jnp.exp2(m_old - m_new)
        p = jnp.exp2(s - (m_new - shift))
        acc_ref[...] = alpha * acc_ref[...] + _dot(p.astype(BF16), va_ref[pl.ds(k0, width), :])
        m_ref[...] = m_new

    n_below = lax.div(t0, tk)
    n_first = jnp.minimum(jnp.maximum(lax.div(t0 - reach_ref[pl.program_id(1)], tk), 0), n_below)
    s_ref[...] = scores(n_first)

    def body(kt, carry):
        s_cur = s_ref[...]
        s_next = scores(kt + 1)
        softmax_pv(kt, s_cur, False)
        s_ref[...] = s_next
        return carry

    lax.fori_loop(n_first, n_below, body, 0)
    groups = lax.div(t0 - n_below * tk, qb) + 1
    for j in range(1, tk // qb + 1):
        @pl.when(groups == j)
        def _(j=j):
            softmax_pv(n_below, s_ref[:, :j * qb], True)

    gates = jax.nn.sigmoid(gate_ref[0])
    acc = acc_ref[...]
    o = acc[:, :HEAD_DIM] / acc[:, HEAD_DIM:HEAD_DIM + 1]
    for r in range(GQA_REP):
        gate = gates[:, GQA_REP + r:GQA_REP + r + 1]
        o_ref[:, r * HEAD_DIM:(r + 1) * HEAD_DIM] = (o[r * qb:(r + 1) * qb] * gate).astype(o_ref.dtype)


def _slope_pieces(slopes):
    s_hi = slopes.astype(BF16)
    s_mid = (slopes - s_hi.astype(F32)).astype(BF16)
    s_lo = (slopes - s_hi.astype(F32) - s_mid.astype(F32)).astype(BF16)
    return jnp.stack([s_hi, s_mid, s_lo, s_hi, s_mid, s_lo], axis=1)


def _slope_rows(pieces, first_lane):
    rows = jnp.pad(pieces, ((0, 0), (first_lane, LANES - first_lane - pieces.shape[1]))).reshape(N_KV, GQA_REP, LANES)
    return jnp.pad(rows, ((0, 0), (0, 8 - GQA_REP), (0, 0)))


def _alibi_reach(slopes, gain_q, gain_k):
    bound = jnp.max(jnp.abs(gain_q)) * jnp.max(jnp.abs(gain_k)) * (HEAD_DIM ** 0.5 * LOG2E * 1.01)
    min_slope = jnp.min(slopes.reshape(N_KV, GQA_REP), axis=1)
    reach = jnp.ceil((2.0 * bound + EXP2_UNDERFLOW) / min_slope)
    return jnp.minimum(reach, 2.0 ** 30).astype(jnp.int32)


def _slc_attention(slopes, reach, proj, selbias, gates_r, *, b, t, k_col, v_col):
    n = b * t
    qb = SLC_Q_BLOCK
    nqb = t // qb
    rows = GQA_REP * qb
    nb = SLC_KEY_TILE // SLC_BLOCK
    rowblk = lambda bi, gi, c: bi * nqb + c
    srow = _slope_rows(_slope_pieces(slopes), nb)
    scol = jnp.repeat(slopes.reshape(N_KV, GQA_REP), qb, axis=1).reshape(N_KV, rows, 1)
    return pl.pallas_call(
        _slc_attn_kernel,
        grid=(b, N_KV, nqb),
        in_specs=[
            pl.BlockSpec(memory_space=pltpu.SMEM),
            pl.BlockSpec((qb, GQA_REP * HEAD_DIM), lambda bi, gi, c: (rowblk(bi, gi, c), gi)),
            pl.BlockSpec((t, HEAD_DIM), lambda bi, gi, c: (bi, k_col + gi)),
            pl.BlockSpec((t, HEAD_DIM), lambda bi, gi, c: (bi, v_col + gi)),
            pl.BlockSpec((1, qb, SLC_LANES), lambda bi, gi, c: (gi, rowblk(bi, gi, c), 0)),
            pl.BlockSpec((1, qb, LANES), lambda bi, gi, c: (gi, rowblk(bi, gi, c), 0)),
            pl.BlockSpec((1, 8, LANES), lambda bi, gi, c: (gi, 0, 0)),
            pl.BlockSpec((1, rows, 1), lambda bi, gi, c: (gi, 0, 0)),
        ],
        out_specs=pl.BlockSpec((qb, GQA_REP * HEAD_DIM), lambda bi, gi, c: (rowblk(bi, gi, c), gi)),
        out_shape=jax.ShapeDtypeStruct((n, N_HEADS * HEAD_DIM), BF16),
        scratch_shapes=[
            pltpu.VMEM((rows, 2 * HEAD_DIM), BF16),
            pltpu.VMEM((t, 2 * HEAD_DIM), BF16),
            pltpu.VMEM((t, 2 * HEAD_DIM), BF16),
            pltpu.VMEM((rows, 1), F32),
            pltpu.VMEM((rows, 2 * HEAD_DIM), F32),
            pltpu.VMEM((rows, SLC_KEY_TILE), F32),
        ],
        compiler_params=_params(("parallel", "parallel", "arbitrary")),
        name="nsa_slc_attn",
    )(reach, proj, proj, proj, selbias, gates_r, srow, scol)


def _sb_attn_kernel(q_ref, k_ref, v_ref, o_ref, acc_ref, carry_ref):
    ts = SB_TILE
    c = pl.program_id(2)
    acc_ref[...] = jnp.zeros(acc_ref.shape, F32)
    carry_ref[...] = jnp.zeros(carry_ref.shape, F32)
    r_idx = lax.broadcasted_iota(jnp.int32, (ts, 1), 0)
    c_idx = lax.broadcasted_iota(jnp.int32, (1, ts), 1)
    tri = (r_idx > c_idx).astype(BF16)
    diag = c_idx < r_idx

    def tile(jt, valid):
        k0 = pl.multiple_of(jt * ts, ts)
        heads = range(SB_HEADS_PER_STEP)
        cols = [slice(e * HEAD_DIM, (e + 1) * HEAD_DIM) for e in heads]
        carries = [carry_ref[e] for e in heads]
        zs = [_dot_nt(q_ref[:, cols[e]], k_ref[pl.ds(k0, ts), cols[e]]) for e in heads]
        lfs = []
        for e in heads:
            lf = -(jnp.maximum(zs[e], 0.0) + jnp.log2(1.0 + jnp.exp2(-jnp.abs(zs[e]))))
            lfs.append(lf if valid is None else jnp.where(valid, lf, 0.0))
        afters = []
        for e in heads:
            hi = lfs[e].astype(BF16)
            lo = (lfs[e] - hi.astype(F32)).astype(BF16)
            afters.append(_dot(hi, tri) + _dot(lo, tri))
        avs = []
        for e in heads:
            a = jnp.exp2(zs[e] + lfs[e] + afters[e] + carries[e])
            if valid is not None:
                a = jnp.where(valid, a, 0.0)
            avs.append(_dot(a.astype(BF16), v_ref[pl.ds(k0, ts), cols[e]]))
        for e in heads:
            acc_ref[e] += avs[e]
            carry_ref[e] = carries[e] + jnp.sum(lfs[e], axis=-1, keepdims=True)

    def live():
        return (jnp.max(carry_ref[...]) > SB_UNDERFLOW * LOG2E).astype(jnp.int32)

    tile(c, diag)

    def cond(state):
        jt, go = state
        return (jt >= 0) & (go > 0)

    def body(state):
        jt, _ = state
        tile(jt, None)
        return jt - 1, live()

    lax.while_loop(cond, body, (c - 1, live()))
    for e in range(SB_HEADS_PER_STEP):
        o_ref[:, e * HEAD_DIM:(e + 1) * HEAD_DIM] = acc_ref[e].astype(o_ref.dtype)


def _sb_attention(proj, *, b, t):
    n = b * t
    ts = SB_TILE
    nq = t // ts
    hs = SB_HEADS_PER_STEP
    width = hs * HEAD_DIM
    nhp = N_HEADS // hs
    return pl.pallas_call(
        _sb_attn_kernel,
        grid=(b, nhp, nq),
        in_specs=[
            pl.BlockSpec((ts, width), lambda bi, h, c: (bi * nq + c, h)),
            pl.BlockSpec((t, width), lambda bi, h, c: (bi, nhp + h), pipeline_mode=pl.Buffered(1)),
            pl.BlockSpec((t, width), lambda bi, h, c: (bi, 2 * nhp + h), pipeline_mode=pl.Buffered(1)),
        ],
        out_specs=pl.BlockSpec((ts, width), lambda bi, h, c: (bi * nq + c, h)),
        out_shape=jax.ShapeDtypeStruct((n, N_HEADS * HEAD_DIM), BF16),
        scratch_shapes=[pltpu.VMEM((hs, ts, HEAD_DIM), F32), pltpu.VMEM((hs, ts, 1), F32)],
        compiler_params=_params(("parallel", "parallel", "arbitrary")),
        name="sb_attn",
    )(proj, proj, proj)


def _row_copy(src_hbm, dst_ref, sem, src_row, dst_row):
    return pltpu.make_async_copy(src_hbm.at[pl.ds(src_row, 1), :], dst_ref.at[pl.ds(dst_row, 1), :], sem)


def _row_gather_start(src_hbm, dst_ref, sem, idx_ref, base, count):
    def body(r, carry):
        _row_copy(src_hbm, dst_ref, sem, idx_ref[base + r], r).start()
        return carry

    lax.fori_loop(0, count, body, 0, unroll=8)


def _row_gather_wait(src_hbm, dst_ref, sem):
    pltpu.make_async_copy(src_hbm.at[pl.ds(0, dst_ref.shape[0]), :], dst_ref, sem).wait()


def _expert_kernel(src_ref, te_ref, nu_ref, x_hbm, g_ref, wg_ref, wu_ref, wd_ref, o_ref, xbuf, wgb, wub, wdb, sem):
    tm = MOE_TILE
    i = pl.program_id(0)
    nt = pl.num_programs(0)
    n_used = nu_ref[0]
    slot = i % MOE_SLOTS
    nxt = (i + 2) % MOE_SLOTS
    next_base = jnp.where(i + 2 < nt, i + 2, i + 2 - nt) * tm

    @pl.when((i < n_used) & ((i == 0) | (te_ref[i] != te_ref[jnp.maximum(i - 1, 0)])))
    def _():
        wgb[...] = wg_ref[0, 0].astype(BF16)
        wub[...] = wu_ref[0, 0].astype(BF16)
        wdb[...] = wd_ref[0, 0].astype(BF16)

    @pl.when(i == 0)
    def _():
        _row_gather_start(x_hbm, xbuf.at[0], sem.at[0], src_ref, 0, tm)
        _row_gather_start(x_hbm, xbuf.at[1], sem.at[1], src_ref, tm, tm)

    @pl.when(i < n_used)
    def _():
        _row_gather_wait(x_hbm, xbuf.at[slot], sem.at[slot])
        x = xbuf[slot]
        ms = jnp.mean(x * x, axis=-1, keepdims=True)
        h = (x * lax.rsqrt(ms + EPS) * g_ref[...]).astype(BF16)
        for r in range(tm):
            _row_copy(x_hbm, xbuf.at[nxt], sem.at[nxt], src_ref[next_base + r], r).start()
        gate = _dot(h, wgb[...])
        up = _dot(h, wub[...])
        act = (gate * jax.nn.sigmoid(gate) * up).astype(BF16)
        o_ref[...] = _dot(act, wdb[...])

    @pl.when(i >= n_used)
    def _():
        o_ref[...] = jnp.zeros(o_ref.shape, o_ref.dtype)

    @pl.when(i == n_used - 1)
    def _():
        prev = (i + 1) % MOE_SLOTS
        _row_gather_wait(x_hbm, xbuf.at[prev], sem.at[prev])
        _row_gather_wait(x_hbm, xbuf.at[nxt], sem.at[nxt])


def _experts(src_tok, tile_expert, n_used, x, g, wg, wu, wd, layer):
    n, d = x.shape
    tm = MOE_TILE
    p_rows = src_tok.shape[0]
    de = wg.shape[3]
    grid_spec = pltpu.PrefetchScalarGridSpec(
        num_scalar_prefetch=3,
        grid=(p_rows // tm,),
        in_specs=[
            pl.BlockSpec(memory_space=pl.ANY),
            pl.BlockSpec((1, d), lambda i, s, te, nu: (0, 0)),
            pl.BlockSpec((1, 1, d, de), lambda i, s, te, nu: (layer, te[i], 0, 0)),
            pl.BlockSpec((1, 1, d, de), lambda i, s, te, nu: (layer, te[i], 0, 0)),
            pl.BlockSpec((1, 1, de, d), lambda i, s, te, nu: (layer, te[i], 0, 0)),
        ],
        out_specs=pl.BlockSpec((tm, d), lambda i, s, te, nu: (i, 0)),
        scratch_shapes=[
            pltpu.VMEM((MOE_SLOTS, tm, d), F32),
            pltpu.VMEM((d, de), BF16),
            pltpu.VMEM((d, de), BF16),
            pltpu.VMEM((de, d), BF16),
            pltpu.SemaphoreType.DMA((MOE_SLOTS,)),
        ],
    )
    return pl.pallas_call(
        _expert_kernel,
        grid_spec=grid_spec,
        out_shape=jax.ShapeDtypeStruct((p_rows, d), F32),
        compiler_params=_params(("arbitrary",)),
        name="moe_experts",
    )(src_tok, tile_expert, n_used, x, g.reshape(1, d), wg, wu, wd)


def _combine_kernel(pos_ref, x_ref, rt_ref, y_hbm, o_ref, ybuf, sem):
    tm = x_ref.shape[0]
    i = pl.program_id(0)
    nt = pl.num_programs(0)

    def start(tile, slot, unrolled):
        base = 2 * tile * tm
        for s in range(2):
            if unrolled:
                for r in range(tm):
                    _row_copy(y_hbm, ybuf.at[slot, s], sem.at[slot], pos_ref[base + 2 * r + s], r).start(priority=r % 2)
            else:
                def body(r, carry):
                    _row_copy(y_hbm, ybuf.at[slot, s], sem.at[slot], pos_ref[base + 2 * r + s], r).start()
                    return carry

                lax.fori_loop(0, tm, body, 0, unroll=8)

    @pl.when(i == 0)
    def _():
        start(0, 0, False)

    @pl.when(i + 1 < nt)
    def _():
        start(i + 1, (i + 1) % 2, True)

    slot = i % 2
    for s in range(2):
        _row_gather_wait(y_hbm, ybuf.at[slot, s], sem.at[slot])
    rt = rt_ref[...]
    o_ref[...] = x_ref[...] + rt[:, 2:3] * ybuf[slot, 0] + rt[:, 3:4] * ybuf[slot, 1]


def _combine(pos, x, routing, ys, *, tm):
    n, d = x.shape
    grid_spec = pltpu.PrefetchScalarGridSpec(
        num_scalar_prefetch=1,
        grid=(n // tm,),
        in_specs=[
            pl.BlockSpec((tm, d), lambda i, p: (i, 0)),
            pl.BlockSpec((tm, LANES), lambda i, p: (i, 0)),
            pl.BlockSpec(memory_space=pl.ANY),
        ],
        out_specs=pl.BlockSpec((tm, d), lambda i, p: (i, 0)),
        scratch_shapes=[pltpu.VMEM((2, 2, tm, d), F32), pltpu.SemaphoreType.DMA((2,))],
    )
    return pl.pallas_call(
        _combine_kernel,
        grid_spec=grid_spec,
        out_shape=jax.ShapeDtypeStruct((n, d), F32),
        compiler_params=_params(("arbitrary",)),
        name="moe_combine",
    )(pos, x, routing, ys)


def _source_tokens_kernel(pos_ref, src_ref):
    def zero(q, carry):
        src_ref[q] = 0
        return carry

    def put(p, carry):
        src_ref[pos_ref[p]] = p >> 1
        return carry

    lax.fori_loop(0, src_ref.shape[0], zero, 0, unroll=16)
    lax.fori_loop(0, pos_ref.shape[0], put, 0, unroll=16)


def _source_tokens(pos, p_rows):
    return pl.pallas_call(
        _source_tokens_kernel,
        in_specs=[pl.BlockSpec(memory_space=pltpu.SMEM)],
        out_specs=pl.BlockSpec(memory_space=pltpu.SMEM),
        out_shape=jax.ShapeDtypeStruct((p_rows,), jnp.int32),
        name="moe_source_tokens",
    )(pos)


def _hier_moe(x, routing, g, w_gate, w_up, w_down, layer):
    n, d = x.shape
    tm = MOE_TILE

    e_flat = routing[:, :2].astype(jnp.int32).reshape(-1)
    onehot = (e_flat[:, None] == jnp.arange(N_EXPERTS, dtype=jnp.int32)[None, :]).astype(jnp.int32)
    counts = jnp.sum(onehot, axis=0)
    rank = jnp.sum((jnp.cumsum(onehot, axis=0) - onehot) * onehot, axis=1)
    padded = ((counts + tm - 1) // tm) * tm
    ends = jnp.cumsum(padded)
    pos = (ends - padded)[e_flat] + rank
    p_rows = 2 * n + N_EXPERTS * tm
    pos = pos.astype(jnp.int32)
    src_tok = _source_tokens(pos, p_rows)
    tile_start = jnp.arange(p_rows // tm, dtype=jnp.int32) * tm
    tile_expert = jnp.minimum(jnp.sum((ends[None, :] <= tile_start[:, None]).astype(jnp.int32), axis=1), N_EXPERTS - 1)
    n_used = (ends[-1] // tm).astype(jnp.int32).reshape(1)

    ys = _experts(src_tok, tile_expert, n_used, x, g, w_gate, w_up, w_down, layer)
    return _combine(pos, x, routing, ys, tm=256 if n % 256 == 0 else n)


def _alibi_slopes():
    return jnp.exp2(-8.0 * jnp.arange(1, N_HEADS + 1, dtype=F32) / N_HEADS)


def _cmp_to_slc(n_cmp_pad, n_slc):
    cs = np.arange(n_cmp_pad) * CMP_STRIDE
    ss = np.arange(SLC_LANES) * SLC_BLOCK
    lo = np.maximum(cs[:, None], ss[None, :])
    hi = np.minimum(cs[:, None] + CMP_BLOCK, ss[None, :] + SLC_BLOCK)
    m = np.maximum(hi - lo, 0).astype(np.float32) / CMP_BLOCK
    m[:, n_slc:] = 0.0
    return m


def _nsa_mixer(x, g_norm, w_in, qk_gain, cmp_pe, cmp_w1, cmp_w2, w_out, router, *, b, t):
    n, d = x.shape
    q_dim = N_HEADS * HEAD_DIM
    kv_dim = N_KV * HEAD_DIM
    scale = HEAD_DIM ** -0.5 * LOG2E
    main = q_dim + 6 * kv_dim
    ones = jnp.ones((kv_dim,), F32)
    tile4 = lambda v: jnp.tile(v, N_KV)
    cgain = jnp.concatenate([jnp.tile(qk_gain[0], N_HEADS) * scale, ones, ones, tile4(qk_gain[2]), ones, tile4(qk_gain[3]), ones])
    tiles_per = lambda cols: cols // PROJ_COLS
    q_tiles = tiles_per(q_dim)
    kv_tiles = tiles_per(kv_dim)
    norm_tiles = list(range(q_tiles)) + [q_tiles + 2 * kv_tiles + j for j in range(kv_tiles)] + [q_tiles + 4 * kv_tiles + j for j in range(kv_tiles)]
    n_gate = 3 * N_HEADS
    w_gate = jnp.pad(w_in[:, main:], ((0, 0), (0, LANES - n_gate))).astype(BF16)
    cmp_tiles = [q_tiles + j for j in range(2 * kv_tiles)]
    proj, gate_logits, kv_cmp = _norm_matmul(x, g_norm, w_in[:, :main].astype(BF16), cgain, norm_tiles, extra_w=w_gate, f32_tiles=cmp_tiles)
    gr = gate_logits[:, :n_gate].reshape(n, 3, N_KV, GQA_REP).transpose(2, 0, 1, 3).reshape(N_KV, n, 3 * GQA_REP)
    gates_r = jnp.pad(gr, ((0, 0), (0, 0), (0, LANES - 3 * GQA_REP)))

    nc = t // CMP_STRIDE
    half = CMP_BLOCK // 2
    w1cat = jnp.concatenate([cmp_w1[:, :half], cmp_w1[:, half:]], axis=-1)
    pe_rows = jnp.pad(jnp.stack([cmp_pe[:, :half], cmp_pe[:, half:]], axis=2), ((0, 0), (0, 0), (0, 6), (0, 0)))
    cmp_gain = jnp.stack([qk_gain[1], jnp.ones((HEAD_DIM,), F32)]).reshape(2, 1, HEAD_DIM)
    cmp_kv = _compress(kv_cmp, w1cat.astype(BF16), pe_rows.astype(BF16), cmp_w2.astype(BF16), cmp_gain, b=b, t=t)

    slopes = _alibi_slopes() * LOG2E
    pieces = _slope_pieces(slopes)
    pad_flag = jnp.full((N_HEADS, 1), NEG, F32).astype(BF16)
    srow = _slope_rows(jnp.concatenate([pieces, pad_flag], axis=1), 0)
    n_slc = t // SLC_BLOCK
    msel = jnp.asarray(_cmp_to_slc(nc, n_slc), BF16)
    q_blocks = q_dim // HEAD_DIM
    o_c, selbias = _cmp_attention(proj, cmp_kv, gates_r, msel, srow, b=b, t=t, top_k=min(SLC_TOPK, n_slc))
    reach = _alibi_reach(slopes, qk_gain[0], qk_gain[2])
    o_s = _slc_attention(slopes, reach, proj, selbias, gates_r, b=b, t=t, k_col=q_blocks + 2 * N_KV, v_col=q_blocks + 3 * N_KV)
    o_w = _win_attention(proj, gates_r, srow, b=b, t=t, k_col=q_blocks + 4 * N_KV, v_col=q_blocks + 5 * N_KV)
    return _proj_residual([o_c, o_s, o_w], w_out.astype(BF16), x, router)


def _sb_mixer(x, g_norm, w_in, w_out, router, *, b, t):
    q_dim = N_HEADS * HEAD_DIM
    scale = HEAD_DIM ** -0.5 * LOG2E
    cgain = jnp.concatenate([jnp.full((q_dim,), scale, F32), jnp.ones((2 * q_dim,), F32)])
    proj = _norm_matmul(x, g_norm, w_in.astype(BF16), cgain, ())
    o = _sb_attention(proj, b=b, t=t)
    return _proj_residual([o], w_out.astype(BF16), x, router)


def kernel(x, norm_mix, norm_ffn, nsa_w_in, nsa_qk_gain, nsa_cmp_pe, nsa_cmp_w1, nsa_cmp_w2, nsa_w_out, sb_w_in, sb_w_out, moe_w_group, moe_b_group, moe_w_router, moe_b_router, moe_w_gate, moe_w_up, moe_w_down):
    b, t, d = x.shape
    depth = norm_mix.shape[0]
    xf = x.reshape(b * t, d)
    for i in range(depth):
        j = i // 2
        router = _router_params(norm_ffn[i], moe_w_group[i], moe_b_group[i], moe_w_router[i], moe_b_router[i])
        if i % 2 == 0:
            xf, routing = _nsa_mixer(xf, norm_mix[i], nsa_w_in[j], nsa_qk_gain[j], nsa_cmp_pe[j], nsa_cmp_w1[j], nsa_cmp_w2[j], nsa_w_out[j], router, b=b, t=t)
        else:
            xf, routing = _sb_mixer(xf, norm_mix[i], sb_w_in[j], sb_w_out[j], router, b=b, t=t)
        xf = _hier_moe(xf, routing, norm_ffn[i], moe_w_gate, moe_w_up, moe_w_down, i)
    return xf.reshape(b, t, d)
```
